```python
import math
import jax, jax.numpy as jnp
from jax import lax
import numpy as np

D_MODEL = 1024
BATCH = 16
SEQ = 2048
DEPTH = 4
DEC_BATCH = 16
DEC_SEQ = 64
PAST_LEN = 4096

CHUNK = 64
Q_BLOCK = 128
N_MIXERS = 2
N_DIFF_LAYERS = (DEPTH + 1) // 2
N_FOX_LAYERS = DEPTH // 2
DIFF_HEADS = 8
DIFF_HD = D_MODEL // (2 * DIFF_HEADS)
FOX_HEADS = 16
FOX_HD = D_MODEL // FOX_HEADS
N_GROUPS = 4
EXPERTS_PER_GROUP = 4
N_EXPERTS = N_GROUPS * EXPERTS_PER_GROUP
TOP_K_IN_GROUP = 2
D_EXPERT = 512
EPS = 1e-6
NEG_INF = -1e30

kernel_name = 'hybrid_diff_fox_hmoe_stream_step'


def rms_norm(x, gain):
    xf = x.astype(jnp.float32)
    y = xf * lax.rsqrt(jnp.mean(xf * xf, axis=-1, keepdims=True) + EPS)
    return (y * gain.astype(jnp.float32)).astype(x.dtype)


def masked_softmax(logits, mask):
    return jax.nn.softmax(jnp.where(mask, logits, NEG_INF), axis=-1)


def alibi_slopes(n_heads):
    return jnp.asarray(2.0 ** (-8.0 * np.arange(1, n_heads + 1) / n_heads), dtype=jnp.float32)


def sweep_query_blocks(attend, n_q, p0, n_keys, chunk_causal):
    outs = []
    for qs in range(0, n_q, Q_BLOCK):
        qe = min(qs + Q_BLOCK, n_q)
        last = p0 + qe
        reach = -(-last // CHUNK) * CHUNK if chunk_causal else last
        outs.append(attend(qs, qe, min(reach, n_keys)))
    return jnp.concatenate(outs, axis=1)


def diff_attention(h, k_past, v_past, w_in, w_out, q_gain, k_gain, lq1, lk1, lq2, lk2, subln_gain, lambda_init):
    b, s, _ = h.shape
    q, k, v = jnp.split(h @ w_in, 3, axis=-1)
    q = rms_norm(q.reshape(b, s, DIFF_HEADS, 2, DIFF_HD), q_gain)
    k = rms_norm(k.reshape(b, s, DIFF_HEADS, 2, DIFF_HD), k_gain)
    k_rows = k.reshape(b, s, DIFF_HEADS, 2 * DIFF_HD)
    v_rows = v.reshape(b, s, DIFF_HEADS, 2 * DIFF_HD)
    if k_past is None:
        p0 = 0
        keys, vals = k_rows, v_rows
    else:
        p0 = k_past.shape[1]
        keys = jnp.concatenate([k_past.astype(k_rows.dtype), k_rows], axis=1)
        vals = jnp.concatenate([v_past.astype(v_rows.dtype), v_rows], axis=1)
    n_keys = keys.shape[1]
    keys = keys.reshape(b, n_keys, DIFF_HEADS, 2, DIFF_HD)
    lam = (jnp.exp(jnp.sum(lq1.astype(jnp.float32) * lk1.astype(jnp.float32)))
           - jnp.exp(jnp.sum(lq2.astype(jnp.float32) * lk2.astype(jnp.float32))) + lambda_init)
    slopes = alibi_slopes(DIFF_HEADS)
    q_pos = p0 + jnp.arange(s, dtype=jnp.int32)
    k_pos = jnp.arange(n_keys, dtype=jnp.int32)
    scale = DIFF_HD ** -0.5

    def attend(qs, qe, ke):
        qp, kp = q_pos[qs:qe], k_pos[:ke]
        dist = jnp.abs(qp[:, None] - kp[None, :]).astype(jnp.float32)
        bias = -slopes[:, None, None] * dist[None]
        mask = (kp[None, :] // CHUNK) <= (qp[:, None] // CHUNK)

        def probs(c):
            logits = jnp.einsum('bqhd,bkhd->bhqk', q[:, qs:qe, :, c], keys[:, :ke, :, c]).astype(jnp.float32)
            return masked_softmax(logits * scale + bias, mask)

        diff_map = probs(0) - lam * probs(1)
        return jnp.einsum('bhqk,bkhe->bqhe', diff_map.astype(vals.dtype), vals[:, :ke])

    o = sweep_query_blocks(attend, s, p0, n_keys, chunk_causal=True)
    o = rms_norm(o, subln_gain) * (1.0 - lambda_init)
    return o.reshape(b, s, D_MODEL) @ w_out, k_rows, v_rows


def fox_attention(h, k_past, v_past, logf_past, w_in, b_f, w_out, q_gain, k_gain):
    b, s, _ = h.shape
    d = D_MODEL
    q, k, v, g, f_logit = jnp.split(h @ w_in, [d, 2 * d, 3 * d, 4 * d], axis=-1)
    q = rms_norm(q.reshape(b, s, FOX_HEADS, FOX_HD), q_gain)
    k = rms_norm(k.reshape(b, s, FOX_HEADS, FOX_HD), k_gain)
    v = v.reshape(b, s, FOX_HEADS, FOX_HD)
    logf = jax.nn.log_sigmoid((f_logit + b_f).astype(jnp.float32))
    q_off = lax.cumsum(logf, axis=1)
    if k_past is None:
        p0 = 0
        keys, vals, k_off = k, v, q_off
    else:
        p0 = k_past.shape[1]
        lp = logf_past.astype(jnp.float32)
        k_off_past = lp - lax.cumsum(lp, axis=1, reverse=True)
        keys = jnp.concatenate([k_past.astype(k.dtype), k], axis=1)
        vals = jnp.concatenate([v_past.astype(v.dtype), v], axis=1)
        k_off = jnp.concatenate([k_off_past, q_off], axis=1)
    n_keys = keys.shape[1]
    q_pos = p0 + jnp.arange(s, dtype=jnp.int32)
    k_pos = jnp.arange(n_keys, dtype=jnp.int32)
    q_off_t = jnp.transpose(q_off, (0, 2, 1))
    k_off_t = jnp.transpose(k_off, (0, 2, 1))
    scale = FOX_HD ** -0.5

    def attend(qs, qe, ke):
        qp, kp = q_pos[qs:qe], k_pos[:ke]
        mask = kp[None, :] <= qp[:, None]
        decay = q_off_t[:, :, qs:qe, None] - k_off_t[:, :, None, :ke]
        logits = jnp.einsum('bqhd,bkhd->bhqk', q[:, qs:qe], keys[:, :ke]).astype(jnp.float32) * scale + decay
        p = masked_softmax(logits, mask)
        return jnp.einsum('bhqk,bkhd->bqhd', p.astype(vals.dtype), vals[:, :ke])

    o = sweep_query_blocks(attend, s, p0, n_keys, chunk_causal=False)
    o = o.reshape(b, s, D_MODEL) * jax.nn.sigmoid(g)
    return o @ w_out, k, v, logf


def hier_moe(t, w_group, b_group, w_expert, b_expert, w_gate, w_up, w_down):
    n = t.shape[0]
    group_prob = jax.nn.softmax((t @ w_group + b_group).astype(jnp.float32), axis=-1)
    p_top, g_idx = lax.top_k(group_prob, 1)
    expert_logits = (t @ w_expert + b_expert).astype(jnp.float32).reshape(n, N_GROUPS, EXPERTS_PER_GROUP)
    in_group = jnp.einsum('ng,nge->ne', jax.nn.one_hot(g_idx[:, 0], N_GROUPS, dtype=jnp.float32), expert_logits)
    top_vals, top_idx = lax.top_k(in_group, TOP_K_IN_GROUP)
    w_sel = jax.nn.softmax(top_vals, axis=-1) * p_top
    expert_id = g_idx * EXPERTS_PER_GROUP + top_idx
    gates = jnp.sum(w_sel[..., None] * jax.nn.one_hot(expert_id, N_EXPERTS, dtype=jnp.float32), axis=1)
    gates = gates.astype(t.dtype)
    y = jnp.zeros_like(t)
    for e in range(N_EXPERTS):
        hid = jax.nn.silu(t @ w_gate[e]) * (t @ w_up[e])
        y = y + gates[:, e:e + 1] * (hid @ w_down[e])
    return y


def setup_inputs(seed: int = 0) -> dict:
    key = jax.random.key(seed)
    ks = jax.random.split(key, 32)
    f32 = jnp.float32
    D = D_MODEL

    def nrm(k, shape, scale):
        return jax.random.normal(k, shape, f32) * scale

    return {
        'x_prompt': nrm(ks[0], (BATCH, SEQ, D), 1.0),
        'x_sample': nrm(ks[1], (DEC_BATCH, DEC_SEQ, D), 1.0),
        'cache_diff_k': nrm(ks[2], (N_DIFF_LAYERS, DEC_BATCH, PAST_LEN, DIFF_HEADS, 2 * DIFF_HD), 1.0),
        'cache_diff_v': nrm(ks[3], (N_DIFF_LAYERS, DEC_BATCH, PAST_LEN, DIFF_HEADS, 2 * DIFF_HD), 1.0),
        'cache_fox_k': nrm(ks[4], (N_FOX_LAYERS, DEC_BATCH, PAST_LEN, FOX_HEADS, FOX_HD), 1.0),
        'cache_fox_v': nrm(ks[5], (N_FOX_LAYERS, DEC_BATCH, PAST_LEN, FOX_HEADS, FOX_HD), 1.0),
        'cache_fox_logf': jax.nn.log_sigmoid(2.0 + nrm(ks[6], (N_FOX_LAYERS, DEC_BATCH, PAST_LEN, FOX_HEADS), 1.0)),
        'norm_mix': 1.0 + nrm(ks[7], (DEPTH, D), 0.02),
        'norm_ffn': 1.0 + nrm(ks[8], (DEPTH, D), 0.02),
        'diff_w_in': nrm(ks[9], (N_DIFF_LAYERS, D, 3 * D), D ** -0.5),
        'diff_w_out': nrm(ks[10], (N_DIFF_LAYERS, D, D), D ** -0.5),
        'diff_q_norm': 1.0 + nrm(ks[11], (N_DIFF_LAYERS, DIFF_HD), 0.02),
        'diff_k_norm': 1.0 + nrm(ks[12], (N_DIFF_LAYERS, DIFF_HD), 0.02),
        'diff_lambda_q1': nrm(ks[13], (N_DIFF_LAYERS, DIFF_HD), 0.1),
        'diff_lambda_k1': nrm(ks[14], (N_DIFF_LAYERS, DIFF_HD), 0.1),
        'diff_lambda_q2': nrm(ks[15], (N_DIFF_LAYERS, DIFF_HD), 0.1),
        'diff_lambda_k2': nrm(ks[16], (N_DIFF_LAYERS, DIFF_HD), 0.1),
        'diff_subln': 1.0 + nrm(ks[17], (N_DIFF_LAYERS, 2 * DIFF_HD), 0.02),
        'fox_w_in': nrm(ks[18], (N_FOX_LAYERS, D, 4 * D + FOX_HEADS), D ** -0.5),
        'fox_b_f': 1.0 + 2.0 * jax.random.uniform(ks[19], (N_FOX_LAYERS, FOX_HEADS), f32),
        'fox_w_out': nrm(ks[20], (N_FOX_LAYERS, D, D), D ** -0.5),
        'fox_q_norm': 1.0 + nrm(ks[21], (N_FOX_LAYERS, FOX_HD), 0.02),
        'fox_k_norm': 1.0 + nrm(ks[22], (N_FOX_LAYERS, FOX_HD), 0.02),
        'moe_w_group': nrm(ks[23], (DEPTH, D, N_GROUPS), D ** -0.5),
        'moe_b_group': nrm(ks[24], (DEPTH, N_GROUPS), 0.01),
        'moe_w_expert': nrm(ks[25], (DEPTH, D, N_EXPERTS), D ** -0.5),
        'moe_b_expert': nrm(ks[26], (DEPTH, N_EXPERTS), 0.01),
        'moe_w_gate': nrm(ks[27], (DEPTH, N_EXPERTS, D, D_EXPERT), D ** -0.5),
        'moe_w_up': nrm(ks[28], (DEPTH, N_EXPERTS, D, D_EXPERT), D ** -0.5),
        'moe_w_down': nrm(ks[29], (DEPTH, N_EXPERTS, D_EXPERT, D), D_EXPERT ** -0.5),
    }


def reference(x_prompt, x_sample, cache_diff_k, cache_diff_v, cache_fox_k, cache_fox_v, cache_fox_logf,
              norm_mix, norm_ffn, diff_w_in, diff_w_out, diff_q_norm, diff_k_norm, diff_lambda_q1,
              diff_lambda_k1, diff_lambda_q2, diff_lambda_k2, diff_subln, fox_w_in, fox_b_f, fox_w_out,
              fox_q_norm, fox_k_norm, moe_w_group, moe_b_group, moe_w_expert, moe_b_expert, moe_w_gate,
              moe_w_up, moe_w_down):
    n_p = x_prompt.shape[0] * x_prompt.shape[1]
    xp, xs = x_prompt, x_sample
    dkp, dvp, dks, dvs = [], [], [], []
    fkp, fvp, flp, fks, fvs, fls = [], [], [], [], [], []
    for i in range(DEPTH):
        j = i // N_MIXERS
        hp = rms_norm(xp, norm_mix[i])
        hs = rms_norm(xs, norm_mix[i])
        if i % N_MIXERS == 0:
            lambda_init = 0.8 - 0.6 * math.exp(-0.3 * i)
            w = (diff_w_in[j], diff_w_out[j], diff_q_norm[j], diff_k_norm[j], diff_lambda_q1[j],
                 diff_lambda_k1[j], diff_lambda_q2[j], diff_lambda_k2[j], diff_subln[j])
            yp, k_p, v_p = diff_attention(hp, None, None, *w, lambda_init)
            ys, k_s, v_s = diff_attention(hs, cache_diff_k[j], cache_diff_v[j], *w, lambda_init)
            dkp.append(k_p); dvp.append(v_p); dks.append(k_s); dvs.append(v_s)
        else:
            w = (fox_w_in[j], fox_b_f[j], fox_w_out[j], fox_q_norm[j], fox_k_norm[j])
            yp, k_p, v_p, l_p = fox_attention(hp, None, None, None, *w)
            ys, k_s, v_s, l_s = fox_attention(hs, cache_fox_k[j], cache_fox_v[j], cache_fox_logf[j], *w)
            fkp.append(k_p); fvp.append(v_p); flp.append(l_p)
            fks.append(k_s); fvs.append(v_s); fls.append(l_s)
        xp = xp + yp
        xs = xs + ys
        tok = jnp.concatenate([xp.reshape(n_p, D_MODEL), xs.reshape(-1, D_MODEL)], axis=0)
        m = hier_moe(rms_norm(tok, norm_ffn[i]), moe_w_group[i], moe_b_group[i], moe_w_expert[i],
                     moe_b_expert[i], moe_w_gate[i], moe_w_up[i], moe_w_down[i])
        xp = xp + m[:n_p].reshape(xp.shape)
        xs = xs + m[n_p:].reshape(xs.shape)
    new_diff_k_prompt = jnp.stack(dkp)
    new_diff_v_prompt = jnp.stack(dvp)
    new_fox_k_prompt = jnp.stack(fkp)
    new_fox_v_prompt = jnp.stack(fvp)
    new_fox_logf_prompt = jnp.stack(flp)
    new_diff_k_sample = jnp.stack(dks)
    new_diff_v_sample = jnp.stack(dvs)
    new_fox_k_sample = jnp.stack(fks)
    new_fox_v_sample = jnp.stack(fvs)
    new_fox_logf_sample = jnp.stack(fls)
    return (xp, xs, new_diff_k_prompt, new_diff_v_prompt, new_fox_k_prompt, new_fox_v_prompt,
            new_fox_logf_prompt, new_diff_k_sample, new_diff_v_sample, new_fox_k_sample,
            new_fox_v_sample, new_fox_logf_sample)
```

```python
import functools
import math

import numpy as np
import jax
import jax.numpy as jnp
from jax import lax
from jax.experimental import pallas as pl
from jax.experimental.pallas import tpu as pltpu

F32 = jnp.float32
BF16 = jnp.bfloat16

CHUNK_SHIFT = 6
DIFF_HEADS = 8
FOX_HEADS = 16
HEAD_GROUP = 64
N_GROUPS = 4
EXPERTS_PER_GROUP = 4
N_EXPERTS = N_GROUPS * EXPERTS_PER_GROUP
N_PAIRS = 6
N_BUCKETS = N_GROUPS * N_PAIRS
EPS = 1e-6
NEG = -1e30
LOG2E = math.log2(math.e)
QSCALE = (HEAD_GROUP ** -0.5) * LOG2E

LANES = 128
TOK_TILE = 512
ATT_TQ = 512
ATT_TK = 256
MOE_TILE = 256
VMEM_LIMIT = 56 * 1024 * 1024

PAIR_A = np.array([0, 0, 0, 1, 1, 2], np.int32)
PAIR_B = np.array([1, 2, 3, 2, 3, 3], np.int32)


def _cparams(n_axes):
    return pltpu.CompilerParams(dimension_semantics=("arbitrary",) * n_axes,
                                vmem_limit_bytes=VMEM_LIMIT)


def _rms_rows(x, gain):
    ms = jnp.mean(x * x, axis=-1, keepdims=True)
    return x * lax.rsqrt(ms + EPS) * gain


def _group_norm(z, gain, sum_ref, exp_ref):
    ms = jnp.dot((z * z).astype(BF16), sum_ref[...], preferred_element_type=F32)
    r = lax.rsqrt(ms + EPS)
    r_hi = r.astype(BF16)
    r_lo = (r - r_hi.astype(F32)).astype(BF16)
    rb = jnp.dot(jnp.concatenate([r_hi, r_lo], axis=-1), exp_ref[...], preferred_element_type=F32)
    return z * rb * gain


def _group_mats(d):
    n_g = d // HEAD_GROUP
    col = np.arange(d) // HEAD_GROUP
    summ = np.zeros((d, LANES), np.float32)
    summ[np.arange(d), col] = 1.0 / HEAD_GROUP
    expand = np.zeros((2 * LANES, d), np.float32)
    expand[col, np.arange(d)] = 1.0
    expand[LANES + col, np.arange(d)] = 1.0
    assert n_g <= LANES
    return jnp.asarray(summ, BF16), jnp.asarray(expand, BF16)


def _inproj_diff_kernel(x_ref, gm_ref, w_ref, qg_ref, kg_ref, sum_ref, exp_ref,
                        qb_ref, kf_ref, kb_ref, vf_ref, vb_ref):
    d = x_ref.shape[1]
    h = _rms_rows(x_ref[...], gm_ref[...]).astype(BF16)
    q = jnp.dot(h, w_ref[:, 0:d], preferred_element_type=F32)
    qb_ref[...] = (_group_norm(q, qg_ref[...], sum_ref, exp_ref) * QSCALE).astype(BF16)
    k = jnp.dot(h, w_ref[:, d:2 * d], preferred_element_type=F32)
    kn = _group_norm(k, kg_ref[...], sum_ref, exp_ref)
    kf_ref[...] = kn
    kb_ref[...] = kn.astype(BF16)
    v = jnp.dot(h, w_ref[:, 2 * d:3 * d], preferred_element_type=F32)
    vf_ref[...] = v
    vb_ref[...] = v.astype(BF16)


def _inproj_diff(x, gm, w_all, layer, qg, kg, summ, expand):
    n, d = x.shape
    tm = TOK_TILE
    row = lambda i: (i, 0)
    fixed = lambda i: (0, 0)
    big = pl.BlockSpec((tm, d), row)
    return pl.pallas_call(
        _inproj_diff_kernel,
        grid=(n // tm,),
        in_specs=[big,
                  pl.BlockSpec((1, d), fixed),
                  pl.BlockSpec((None, d, 3 * d), lambda i: (layer, 0, 0)),
                  pl.BlockSpec((1, d), fixed), pl.BlockSpec((1, d), fixed),
                  pl.BlockSpec(summ.shape, fixed), pl.BlockSpec(expand.shape, fixed)],
        out_specs=[big, big, big, big, big],
        out_shape=[jax.ShapeDtypeStruct((n, d), BF16), jax.ShapeDtypeStruct((n, d), F32),
                   jax.ShapeDtypeStruct((n, d), BF16), jax.ShapeDtypeStruct((n, d), F32),
                   jax.ShapeDtypeStruct((n, d), BF16)],
        compiler_params=_cparams(1),
        name="inproj_diff",
    )(x, gm, w_all, qg, kg, summ, expand)


def _inproj_fox_kernel(x_ref, gm_ref, w_ref, wf_ref, bf_ref, qg_ref, kg_ref, sum_ref, exp_ref,
                       qb_ref, kf_ref, kb_ref, vf_ref, vb_ref, sg_ref, lf_ref):
    d = x_ref.shape[1]
    h = _rms_rows(x_ref[...], gm_ref[...]).astype(BF16)
    q = jnp.dot(h, w_ref[:, 0:d], preferred_element_type=F32)
    qb_ref[...] = (_group_norm(q, qg_ref[...], sum_ref, exp_ref) * QSCALE).astype(BF16)
    k = jnp.dot(h, w_ref[:, d:2 * d], preferred_element_type=F32)
    kn = _group_norm(k, kg_ref[...], sum_ref, exp_ref)
    kf_ref[...] = kn
    kb_ref[...] = kn.astype(BF16)
    v = jnp.dot(h, w_ref[:, 2 * d:3 * d], preferred_element_type=F32)
    vf_ref[...] = v
    vb_ref[...] = v.astype(BF16)
    g = jnp.dot(h, w_ref[:, 3 * d:4 * d], preferred_element_type=F32)
    sg_ref[...] = (1.0 / (1.0 + jnp.exp(-g))).astype(BF16)
    f = jnp.dot(h, wf_ref[...], preferred_element_type=F32) + bf_ref[...]
    lf_ref[...] = jnp.minimum(f, 0.0) - jnp.log(1.0 + jnp.exp(-jnp.abs(f)))


def _inproj_fox(x, gm, w_all, wf_all, layer, bf, qg, kg, summ, expand):
    n, d = x.shape
    tm = TOK_TILE
    row = lambda i: (i, 0)
    fixed = lambda i: (0, 0)
    big = pl.BlockSpec((tm, d), row)
    return pl.pallas_call(
        _inproj_fox_kernel,
        grid=(n // tm,),
        in_specs=[big,
                  pl.BlockSpec((1, d), fixed),
                  pl.BlockSpec((None, d, 4 * d), lambda i: (layer, 0, 0)),
                  pl.BlockSpec((None, d, LANES), lambda i: (layer, 0, 0)),
                  pl.BlockSpec((1, LANES), fixed),
                  pl.BlockSpec((1, d), fixed), pl.BlockSpec((1, d), fixed),
                  pl.BlockSpec(summ.shape, fixed), pl.BlockSpec(expand.shape, fixed)],
        out_specs=[big, big, big, big, big, big, pl.BlockSpec((tm, LANES), row)],
        out_shape=[jax.ShapeDtypeStruct((n, d), BF16), jax.ShapeDtypeStruct((n, d), F32),
                   jax.ShapeDtypeStruct((n, d), BF16), jax.ShapeDtypeStruct((n, d), F32),
                   jax.ShapeDtypeStruct((n, d), BF16), jax.ShapeDtypeStruct((n, d), BF16),
                   jax.ShapeDtypeStruct((n, LANES), F32)],
        compiler_params=_cparams(1),
        name="inproj_fox",
    )(x, gm, w_all, wf_all, bf, qg, kg, summ, expand)


def _half_masks(rows):
    lane = lax.broadcasted_iota(jnp.int32, (rows, LANES), 1)
    return lane < HEAD_GROUP, lane >= HEAD_GROUP


def _stream_update(c, s, v_blk, m_ref, l_ref, acc_ref):
    m_prev = m_ref[c]
    m_new = jnp.maximum(m_prev, jnp.max(s, axis=-1, keepdims=True))
    alpha = jnp.exp2(m_prev - m_new)
    p = jnp.exp2(s - m_new)
    l_ref[c] = alpha * l_ref[c] + jnp.sum(p, axis=-1, keepdims=True)
    acc_ref[c] = alpha * acc_ref[c] + jnp.dot(p.astype(BF16), v_blk, preferred_element_type=F32)
    m_ref[c] = m_new


def _attend_block(q, k_blk, v_blk, bias0, bias1, m_ref, l_ref, acc_ref):
    lo, hi = _half_masks(k_blk.shape[0])
    zero = jnp.zeros_like(k_blk)
    dims = (((1,), (1,)), ((), ()))
    s0 = lax.dot_general(q, jnp.where(lo, k_blk, zero), dims, preferred_element_type=F32) + bias0
    _stream_update(0, s0, v_blk, m_ref, l_ref, acc_ref)
    s1 = lax.dot_general(q, jnp.where(hi, k_blk, zero), dims, preferred_element_type=F32) + bias1
    _stream_update(1, s1, v_blk, m_ref, l_ref, acc_ref)


def _init_state(m_ref, l_ref, acc_ref):
    m_ref[...] = jnp.full(m_ref.shape, NEG, F32)
    l_ref[...] = jnp.zeros(l_ref.shape, F32)
    acc_ref[...] = jnp.zeros(acc_ref.shape, F32)


def _diff_lambda(lq1, lk1, lq2, lk2, lambda_init):
    a = jnp.exp(jnp.sum(lq1[...] * lk1[...], axis=-1, keepdims=True))
    b = jnp.exp(jnp.sum(lq2[...] * lk2[...], axis=-1, keepdims=True))
    return a - b + lambda_init


def _diff_finish(lam, g_ref, o_ref, l_ref, acc_ref, lambda_init):
    o = acc_ref[0] / l_ref[0] - lam * (acc_ref[1] / l_ref[1])
    ms = jnp.mean(o * o, axis=-1, keepdims=True)
    o_ref[...] = ((o * lax.rsqrt(ms + EPS) * g_ref[...]) * (1.0 - lambda_init)).astype(o_ref.dtype)


def _alibi_table(slope2, rows, cols, q0, k0):
    i = lax.broadcasted_iota(jnp.int32, (rows, cols), 0)
    j = lax.broadcasted_iota(jnp.int32, (rows, cols), 1)
    qpos = i + q0
    kpos = j + k0
    allowed = jnp.right_shift(kpos, CHUNK_SHIFT) <= jnp.right_shift(qpos, CHUNK_SHIFT)
    bias = slope2 * (i - jnp.abs(qpos - kpos)).astype(F32)
    return jnp.where(allowed, bias, NEG)


def _diff_prompt_kernel(lambda_init, sl_ref, lq1, lk1, lq2, lk2, q_ref, k_ref, v_ref, g_ref,
                        o_ref, m_ref, l_ref, acc_ref, tbl_ref):
    tq = q_ref.shape[0]
    tk = ATT_TK if tq % ATT_TK == 0 else tq
    h = pl.program_id(1)
    qi = pl.program_id(2)
    slope2 = sl_ref[h]

    @pl.when(qi == 0)
    def _():
        tbl_ref[...] = _alibi_table(slope2, tq, tq, 0, 0)

    _init_state(m_ref, l_ref, acc_ref)
    q = q_ref[...]
    q0 = qi * tq
    col = lax.broadcasted_iota(jnp.int32, (1, tk), 1)

    def body(ki, carry):
        k0 = pl.multiple_of(ki * tk, tk)
        bias = slope2 * (col + (k0 - q0)).astype(F32)
        _attend_block(q, k_ref[pl.ds(k0, tk), :], v_ref[pl.ds(k0, tk), :], bias, bias,
                      m_ref, l_ref, acc_ref)
        return carry

    lax.fori_loop(0, qi * (tq // tk), body, 0)
    for dblk in range(tq // tk):
        k0 = pl.multiple_of(q0 + dblk * tk, tk)
        bias = tbl_ref[:, dblk * tk:(dblk + 1) * tk]
        _attend_block(q, k_ref[pl.ds(k0, tk), :], v_ref[pl.ds(k0, tk), :], bias, bias,
                      m_ref, l_ref, acc_ref)
    lam = _diff_lambda(lq1, lk1, lq2, lk2, lambda_init)
    _diff_finish(lam, g_ref, o_ref, l_ref, acc_ref, lambda_init)


def _diff_prompt(qb, kb, vb, slopes2, lam_params, subln, n_batch, seq, lambda_init):
    d = qb.shape[1]
    tq = min(ATT_TQ, seq)
    nq = seq // tq
    hd2 = 2 * HEAD_GROUP
    small = pl.BlockSpec((1, HEAD_GROUP), lambda b, h, i: (0, 0))
    return pl.pallas_call(
        functools.partial(_diff_prompt_kernel, lambda_init),
        grid=(n_batch, DIFF_HEADS, nq),
        in_specs=[pl.BlockSpec(memory_space=pltpu.SMEM), small, small, small, small,
                  pl.BlockSpec((tq, hd2), lambda b, h, i: (b * nq + i, h)),
                  pl.BlockSpec((seq, hd2), lambda b, h, i: (b, h)),
                  pl.BlockSpec((seq, hd2), lambda b, h, i: (b, h)),
                  pl.BlockSpec((1, hd2), lambda b, h, i: (0, 0))],
        out_specs=pl.BlockSpec((tq, hd2), lambda b, h, i: (b * nq + i, h)),
        out_shape=jax.ShapeDtypeStruct((n_batch * seq, d), BF16),
        scratch_shapes=[pltpu.VMEM((2, tq, 1), F32), pltpu.VMEM((2, tq, 1), F32),
                        pltpu.VMEM((2, tq, hd2), F32), pltpu.VMEM((tq, tq), F32)],
        compiler_params=_cparams(3),
        name="diff_attn_prompt",
    )(slopes2, *lam_params, qb, kb, vb, subln)


def _diff_sample_kernel(lambda_init, past, sl_ref, lq1, lk1, lq2, lk2, q_ref, kp_ref, vp_ref,
                        kn_ref, vn_ref, g_ref, o_ref, m_ref, l_ref, acc_ref):
    tq = q_ref.shape[0]
    tk = ATT_TK if past % ATT_TK == 0 else past
    slope2 = sl_ref[pl.program_id(1)]
    _init_state(m_ref, l_ref, acc_ref)
    q = q_ref[...]
    col = lax.broadcasted_iota(jnp.int32, (1, tk), 1)

    def body(ki, carry):
        k0 = pl.multiple_of(ki * tk, tk)
        bias = slope2 * (col + (k0 - past)).astype(F32)
        _attend_block(q, kp_ref[pl.ds(k0, tk), :].astype(BF16), vp_ref[pl.ds(k0, tk), :].astype(BF16),
                      bias, bias, m_ref, l_ref, acc_ref)
        return carry

    lax.fori_loop(0, past // tk, body, 0)
    bias = _alibi_table(slope2, tq, tq, past, past)
    _attend_block(q, kn_ref[...], vn_ref[...], bias, bias, m_ref, l_ref, acc_ref)
    lam = _diff_lambda(lq1, lk1, lq2, lk2, lambda_init)
    _diff_finish(lam, g_ref, o_ref, l_ref, acc_ref, lambda_init)


def _diff_sample(qb, kb, vb, cache_k, cache_v, layer, slopes2, lam_params, subln, row0, n_batch, seq,
                 lambda_init):
    d = qb.shape[1]
    past = cache_k.shape[2]
    hd2 = 2 * HEAD_GROUP
    blk0 = row0 // seq
    small = pl.BlockSpec((1, HEAD_GROUP), lambda b, h: (0, 0))
    new = pl.BlockSpec((seq, hd2), lambda b, h: (blk0 + b, h))
    old = pl.BlockSpec((None, None, past, hd2), lambda b, h: (layer, b, 0, h))
    return pl.pallas_call(
        functools.partial(_diff_sample_kernel, lambda_init, past),
        grid=(n_batch, DIFF_HEADS),
        in_specs=[pl.BlockSpec(memory_space=pltpu.SMEM), small, small, small, small,
                  new, old, old, new, new, pl.BlockSpec((1, hd2), lambda b, h: (0, 0))],
        out_specs=pl.BlockSpec((seq, hd2), lambda b, h: (b, h)),
        out_shape=jax.ShapeDtypeStruct((n_batch * seq, d), BF16),
        scratch_shapes=[pltpu.VMEM((2, seq, 1), F32), pltpu.VMEM((2, seq, 1), F32),
                        pltpu.VMEM((2, seq, hd2), F32)],
        compiler_params=_cparams(2),
        name="diff_attn_sample",
    )(slopes2, *lam_params, qb, cache_k, cache_v, kb, vb, subln)


def _causal_table(rows, cols):
    i = lax.broadcasted_iota(jnp.int32, (rows, cols), 0)
    j = lax.broadcasted_iota(jnp.int32, (rows, cols), 1)
    return jnp.where(j <= i, 0.0, NEG).astype(F32)


def _fox_finish(sg_ref, o_ref, l_ref, acc_ref):
    lo, _ = _half_masks(o_ref.shape[0])
    o = jnp.where(lo, acc_ref[0] / l_ref[0], acc_ref[1] / l_ref[1])
    o_ref[...] = (o * sg_ref[...].astype(F32)).astype(o_ref.dtype)


def _fox_prompt_kernel(q_ref, k_ref, v_ref, c_ref, sg_ref, o_ref, m_ref, l_ref, acc_ref, tbl_ref):
    tq = q_ref.shape[0]
    tk = c_ref.shape[-1]
    qi = pl.program_id(2)

    @pl.when(qi == 0)
    def _():
        tbl_ref[...] = _causal_table(tq, tq)

    _init_state(m_ref, l_ref, acc_ref)
    q = q_ref[...]
    q0 = qi * tq
    per_q = tq // tk

    def body(ki, carry):
        k0 = pl.multiple_of(ki * tk, tk)
        c = c_ref[ki]
        _attend_block(q, k_ref[pl.ds(k0, tk), :], v_ref[pl.ds(k0, tk), :], c[0:1, :], c[1:2, :],
                      m_ref, l_ref, acc_ref)
        return carry

    lax.fori_loop(0, qi * per_q, body, 0)
    for dblk in range(per_q):
        k0 = pl.multiple_of(q0 + dblk * tk, tk)
        c = c_ref[qi * per_q + dblk]
        tbl = tbl_ref[:, dblk * tk:(dblk + 1) * tk]
        _attend_block(q, k_ref[pl.ds(k0, tk), :], v_ref[pl.ds(k0, tk), :],
                      tbl + c[0:1, :], tbl + c[1:2, :], m_ref, l_ref, acc_ref)
    _fox_finish(sg_ref, o_ref, l_ref, acc_ref)


def _fox_prompt(qb, kb, vb, coff, sg, n_batch, seq):
    d = qb.shape[1]
    tq = min(ATT_TQ, seq)
    nq = seq // tq
    nblk, tk = coff.shape[2], coff.shape[4]
    w = LANES
    qspec = pl.BlockSpec((tq, w), lambda b, p, i: (b * nq + i, p))
    kspec = pl.BlockSpec((seq, w), lambda b, p, i: (b, p))
    return pl.pallas_call(
        _fox_prompt_kernel,
        grid=(n_batch, d // w, nq),
        in_specs=[qspec, kspec, kspec,
                  pl.BlockSpec((None, None, nblk, 2, tk), lambda b, p, i: (b, p, 0, 0, 0)),
                  qspec],
        out_specs=qspec,
        out_shape=jax.ShapeDtypeStruct((n_batch * seq, d), BF16),
        scratch_shapes=[pltpu.VMEM((2, tq, 1), F32), pltpu.VMEM((2, tq, 1), F32),
                        pltpu.VMEM((2, tq, w), F32), pltpu.VMEM((tq, tq), F32)],
        compiler_params=_cparams(3),
        name="fox_attn_prompt",
    )(qb, kb, vb, coff, sg)


def _fox_sample_kernel(past, q_ref, kp_ref, vp_ref, kn_ref, vn_ref, cp_ref, cn_ref, sg_ref,
                       o_ref, m_ref, l_ref, acc_ref):
    tq = q_ref.shape[0]
    tk = cp_ref.shape[-1]
    _init_state(m_ref, l_ref, acc_ref)
    q = q_ref[...]

    def body(ki, carry):
        k0 = pl.multiple_of(ki * tk, tk)
        c = cp_ref[ki]
        _attend_block(q, kp_ref[pl.ds(k0, tk), :].astype(BF16), vp_ref[pl.ds(k0, tk), :].astype(BF16),
                      c[0:1, :], c[1:2, :], m_ref, l_ref, acc_ref)
        return carry

    lax.fori_loop(0, past // tk, body, 0)
    tbl = _causal_table(tq, tq)
    c = cn_ref[...]
    _attend_block(q, kn_ref[...], vn_ref[...], tbl + c[0:1, :], tbl + c[1:2, :], m_ref, l_ref, acc_ref)
    _fox_finish(sg_ref, o_ref, l_ref, acc_ref)


def _fox_sample(qb, kb, vb, cache_k, cache_v, layer, c_past, c_new, sg, row0, n_batch, seq):
    d = qb.shape[1]
    past = cache_k.shape[2]
    nblk, tk = c_past.shape[2], c_past.shape[4]
    w = LANES
    blk0 = row0 // seq
    new = pl.BlockSpec((seq, w), lambda b, p: (blk0 + b, p))
    old = pl.BlockSpec((None, None, past, w), lambda b, p: (layer, b, 0, p))
    return pl.pallas_call(
        functools.partial(_fox_sample_kernel, past),
        grid=(n_batch, d // w),
        in_specs=[new, old, old, new, new,
                  pl.BlockSpec((None, None, nblk, 2, tk), lambda b, p: (b, p, 0, 0, 0)),
                  pl.BlockSpec((None, None, 2, seq), lambda b, p: (b, p, 0, 0)),
                  new],
        out_specs=pl.BlockSpec((seq, w), lambda b, p: (b, p)),
        out_shape=jax.ShapeDtypeStruct((n_batch * seq, d), BF16),
        scratch_shapes=[pltpu.VMEM((2, seq, 1), F32), pltpu.VMEM((2, seq, 1), F32),
                        pltpu.VMEM((2, seq, w), F32)],
        compiler_params=_cparams(2),
        name="fox_attn_sample",
    )(qb, cache_k, cache_v, kb, vb, c_past, c_new, sg)


def _split_dot(x, mat):
    hi = x.astype(BF16)
    lo = (x - hi.astype(F32)).astype(BF16)
    return (jnp.dot(hi, mat, preferred_element_type=F32) + jnp.dot(lo, mat, preferred_element_type=F32))


def _coff_kernel(suffix, x_ref, o_ref):
    nblk, _, tk = o_ref.shape
    r = lax.broadcasted_iota(jnp.int32, (tk, tk), 0)
    c = lax.broadcasted_iota(jnp.int32, (tk, tk), 1)
    mat = jnp.where((r > c) if suffix else (r <= c), 1.0, 0.0).astype(BF16)
    carry = jnp.zeros((x_ref.shape[0], 1), F32)
    order = range(nblk - 1, -1, -1) if suffix else range(nblk)
    for blk in order:
        x = x_ref[:, blk * tk:(blk + 1) * tk]
        part = _split_dot(x, mat) + carry
        o_ref[blk] = (part if suffix else -part) * LOG2E
        carry = carry + jnp.sum(x, axis=-1, keepdims=True)


def _coff(xt, tk, suffix):
    n_batch, n_head, length = xt.shape
    nblk = length // tk
    return pl.pallas_call(
        functools.partial(_coff_kernel, suffix),
        grid=(n_batch,),
        in_specs=[pl.BlockSpec((None, n_head, length), lambda b: (b, 0, 0))],
        out_specs=pl.BlockSpec((None, nblk, n_head, tk), lambda b: (b, 0, 0, 0)),
        out_shape=jax.ShapeDtypeStruct((n_batch, nblk, n_head, tk), F32),
        compiler_params=_cparams(1),
        name="fox_coff_suffix" if suffix else "fox_coff_prefix",
    )(xt)


def _pair_heads(c):
    n_batch, nblk, n_head, tk = c.shape
    return c.reshape(n_batch, nblk, n_head // 2, 2, tk).transpose(0, 2, 1, 3, 4)


def _outproj_kernel(o_ref, x_ref, w_ref, gf_ref, wr_hi_ref, wr_lo_ref, br_ref, xo_ref, meta_ref):
    xn = x_ref[...] + jnp.dot(o_ref[...], w_ref[...], preferred_element_type=F32)
    xo_ref[...] = xn
    t = _rms_rows(xn, gf_ref[...])
    t_hi = t.astype(BF16)
    t_lo = (t - t_hi.astype(F32)).astype(BF16)
    logits = (jnp.dot(t_hi, wr_hi_ref[...], preferred_element_type=F32)
              + jnp.dot(t_lo, wr_hi_ref[...], preferred_element_type=F32)
              + jnp.dot(t_hi, wr_lo_ref[...], preferred_element_type=F32)) + br_ref[...]
    rows = logits.shape[0]
    lane = lax.broadcasted_iota(jnp.int32, (rows, LANES), 1).astype(F32)
    big = float(LANES)

    def first_argmax(vals, vmax):
        return jnp.min(jnp.where(vals == vmax, lane, big), axis=-1, keepdims=True)

    gl = jnp.where(lane < N_GROUPS, logits, NEG)
    gmax = jnp.max(gl, axis=-1, keepdims=True)
    gsum = jnp.sum(jnp.where(lane < N_GROUPS, jnp.exp(logits - gmax), 0.0), axis=-1, keepdims=True)
    p_top = 1.0 / gsum
    g_idx = first_argmax(gl, gmax)
    base = N_GROUPS + EXPERTS_PER_GROUP * g_idx
    el = jnp.where((lane >= base) & (lane < base + EXPERTS_PER_GROUP), logits, NEG)
    v1 = jnp.max(el, axis=-1, keepdims=True)
    i1 = first_argmax(el, v1)
    el2 = jnp.where(lane == i1, NEG, el)
    v2 = jnp.max(el2, axis=-1, keepdims=True)
    i2 = first_argmax(el2, v2)
    e2 = jnp.exp(v2 - v1)
    w1 = p_top / (1.0 + e2)
    w2 = p_top * e2 / (1.0 + e2)
    a1 = i1 - base
    a2 = i2 - base
    lo = jnp.minimum(a1, a2)
    hi = jnp.maximum(a1, a2)
    wa = jnp.where(a1 < a2, w1, w2)
    wb = jnp.where(a1 < a2, w2, w1)
    pair = jnp.where(lo == 0.0, hi - 1.0, jnp.where(lo == 1.0, hi + 1.0, 5.0))
    bucket = g_idx * N_PAIRS + pair
    meta_ref[...] = jnp.where(lane == 0.0, bucket, jnp.where(lane == 1.0, wa, jnp.where(lane == 2.0, wb, 0.0)))


def _outproj(o, x, w_all, layer, gf, wr_hi, wr_lo, br):
    n, d = x.shape
    tm = TOK_TILE
    row = lambda i: (i, 0)
    fixed = lambda i: (0, 0)
    big = pl.BlockSpec((tm, d), row)
    return pl.pallas_call(
        _outproj_kernel,
        grid=(n // tm,),
        in_specs=[big, big,
                  pl.BlockSpec((None, d, d), lambda i: (layer, 0, 0)),
                  pl.BlockSpec((1, d), fixed),
                  pl.BlockSpec((d, LANES), fixed), pl.BlockSpec((d, LANES), fixed),
                  pl.BlockSpec((1, LANES), fixed)],
        out_specs=[big, pl.BlockSpec((tm, LANES), row)],
        out_shape=[jax.ShapeDtypeStruct((n, d), F32), jax.ShapeDtypeStruct((n, LANES), F32)],
        compiler_params=_cparams(1),
        name="outproj_router",
    )(o, x, w_all, gf, wr_hi, wr_lo, br)


def _moe_kernel(tile_ref, ea_ref, eb_ref, lo_ref, hi_ref, first_ref,
                xs_ref, ms_ref, gf_ref, wgu_a, wd_a, wgu_b, wd_b, out_ref):
    s = pl.program_id(0)
    lo = lo_ref[s]
    hi = hi_ref[s]

    @pl.when(first_ref[s] == 1)
    def _():
        out_ref[...] = xs_ref[...]

    @pl.when(hi > lo)
    def _():
        t = _rms_rows(xs_ref[...], gf_ref[...]).astype(BF16)
        rows = t.shape[0]
        r = lax.broadcasted_iota(jnp.int32, (rows, 1), 0)
        inside = (r >= lo) & (r < hi)
        ms = ms_ref[...]
        wa = jnp.where(inside, ms[:, 1:2], 0.0)
        wb = jnp.where(inside, ms[:, 2:3], 0.0)

        def expert(wgu, wd):
            de = wd.shape[0]
            gu = jnp.dot(t, wgu[...], preferred_element_type=F32)
            g = gu[:, :de]
            hid = (g / (1.0 + jnp.exp(-g)) * gu[:, de:]).astype(BF16)
            return jnp.dot(hid, wd[...], preferred_element_type=F32)

        out_ref[...] += wa * expert(wgu_a, wd_a) + wb * expert(wgu_b, wd_b)


def _moe(sched, xs, ms, gf, wgu_all, wd_all, layer):
    n, d = xs.shape
    de = wd_all.shape[2]
    n_steps = sched[0].shape[0]
    tile_map = lambda s, tile, ea, eb, lo, hi, first: (tile[s], 0)
    fixed = lambda s, *_: (0, 0)
    wa_map = lambda s, tile, ea, eb, lo, hi, first: (layer, ea[s], 0, 0)
    wb_map = lambda s, tile, ea, eb, lo, hi, first: (layer, eb[s], 0, 0)
    grid_spec = pltpu.PrefetchScalarGridSpec(
        num_scalar_prefetch=6,
        grid=(n_steps,),
        in_specs=[pl.BlockSpec((MOE_TILE, d), tile_map),
                  pl.BlockSpec((MOE_TILE, ms.shape[1]), tile_map),
                  pl.BlockSpec((1, d), fixed),
                  pl.BlockSpec((None, None, d, 2 * de), wa_map),
                  pl.BlockSpec((None, None, de, d), wa_map),
                  pl.BlockSpec((None, None, d, 2 * de), wb_map),
                  pl.BlockSpec((None, None, de, d), wb_map)],
        out_specs=pl.BlockSpec((MOE_TILE, d), tile_map),
    )
    return pl.pallas_call(
        _moe_kernel,
        grid_spec=grid_spec,
        out_shape=jax.ShapeDtypeStruct((n, d), F32),
        compiler_params=_cparams(1),
        name="moe_experts",
    )(*sched, xs, ms, gf, wgu_all, wd_all, wgu_all, wd_all)


def _moe_schedule(bucket, n):
    n_tiles = n // MOE_TILE
    n_steps = n_tiles + N_BUCKETS - 1
    perm = jnp.argsort(bucket, stable=True).astype(jnp.int32)
    sorted_b = bucket[perm]
    counts = jnp.zeros((N_BUCKETS,), jnp.int32).at[bucket].add(1)
    offs = jnp.concatenate([jnp.zeros((1,), jnp.int32), jnp.cumsum(counts)])
    fb = sorted_b[::MOE_TILE]
    lb = sorted_b[MOE_TILE - 1::MOE_TILE]
    per_tile = lb - fb + 1
    starts = jnp.cumsum(per_tile) - per_tile
    total = jnp.sum(per_tile)
    s = jnp.arange(n_steps, dtype=jnp.int32)
    tile = jnp.clip(jnp.searchsorted(starts, s, side="right").astype(jnp.int32) - 1, 0, n_tiles - 1)
    valid = s < total
    bkt = jnp.where(valid, fb[tile] + (s - starts[tile]), lb[n_tiles - 1])
    lo = jnp.clip(offs[bkt] - tile * MOE_TILE, 0, MOE_TILE)
    hi = jnp.clip(offs[bkt + 1] - tile * MOE_TILE, 0, MOE_TILE)
    lo = jnp.where(valid, lo, 0)
    hi = jnp.where(valid, hi, 0)
    first = (valid & (s == starts[tile])).astype(jnp.int32)
    grp = bkt // N_PAIRS
    ea = grp * EXPERTS_PER_GROUP + jnp.asarray(PAIR_A)[bkt % N_PAIRS]
    eb = grp * EXPERTS_PER_GROUP + jnp.asarray(PAIR_B)[bkt % N_PAIRS]
    return perm, (tile, ea.astype(jnp.int32), eb.astype(jnp.int32), lo.astype(jnp.int32),
                  hi.astype(jnp.int32), first)


def _tile_gain(g, d):
    return jnp.tile(g.astype(F32), d // g.shape[0]).reshape(1, d)


def kernel(x_prompt, x_sample, cache_diff_k, cache_diff_v, cache_fox_k, cache_fox_v, cache_fox_logf, norm_mix, norm_ffn, diff_w_in, diff_w_out, diff_q_norm, diff_k_norm, diff_lambda_q1, diff_lambda_k1, diff_lambda_q2, diff_lambda_k2, diff_subln, fox_w_in, fox_b_f, fox_w_out, fox_q_norm, fox_k_norm, moe_w_group, moe_b_group, moe_w_expert, moe_b_expert, moe_w_gate, moe_w_up, moe_w_down):
    bp, sp, d = x_prompt.shape
    bs, ss, _ = x_sample.shape
    n_p = bp * sp
    n_s = bs * ss
    n = n_p + n_s
    depth = norm_mix.shape[0]
    past = cache_diff_k.shape[2]
    assert n % TOK_TILE == 0 and n % MOE_TILE == 0 and n_p % TOK_TILE == 0
    assert d % LANES == 0 and sp % min(ATT_TQ, sp) == 0 and n_p % ss == 0
    assert DIFF_HEADS * 2 * HEAD_GROUP == d and FOX_HEADS * HEAD_GROUP == d

    summ, expand = _group_mats(d)
    x = jnp.concatenate([x_prompt.reshape(n_p, d), x_sample.reshape(n_s, d)], axis=0)

    diff_w_in_b = diff_w_in.astype(BF16)
    diff_w_out_b = diff_w_out.astype(BF16)
    fox_w_main_b = fox_w_in[:, :, :4 * d].astype(BF16)
    fox_w_f_b = jnp.pad(fox_w_in[:, :, 4 * d:], ((0, 0), (0, 0), (0, LANES - FOX_HEADS))).astype(BF16)
    fox_w_out_b = fox_w_out.astype(BF16)
    wgu_b = jnp.concatenate([moe_w_gate, moe_w_up], axis=-1).astype(BF16)
    wd_b = moe_w_down.astype(BF16)
    w_router = jnp.pad(jnp.concatenate([moe_w_group, moe_w_expert], axis=-1),
                       ((0, 0), (0, 0), (0, LANES - N_GROUPS - N_EXPERTS)))
    wr_hi = w_router.astype(BF16)
    wr_lo = (w_router - wr_hi.astype(F32)).astype(BF16)
    b_router = jnp.pad(jnp.concatenate([moe_b_group, moe_b_expert], axis=-1),
                       ((0, 0), (0, LANES - N_GROUPS - N_EXPERTS)))
    slopes2 = jnp.asarray(2.0 ** (-8.0 * np.arange(1, DIFF_HEADS + 1) / DIFF_HEADS) * LOG2E, F32)
    cache_dk = cache_diff_k.reshape(cache_diff_k.shape[:3] + (d,))
    cache_dv = cache_diff_v.reshape(cache_diff_v.shape[:3] + (d,))
    cache_fk = cache_fox_k.reshape(cache_fox_k.shape[:3] + (d,))
    cache_fv = cache_fox_v.reshape(cache_fox_v.shape[:3] + (d,))

    outs = {name: [] for name in ("dkp", "dvp", "dks", "dvs", "fkp", "fvp", "flp", "fks", "fvs", "fls")}
    for i in range(depth):
        j = i // 2
        gm = norm_mix[i].reshape(1, d)
        if i % 2 == 0:
            lambda_init = 0.8 - 0.6 * math.exp(-0.3 * i)
            qb, kf, kb, vf, vb = _inproj_diff(x, gm, diff_w_in_b, j, _tile_gain(diff_q_norm[j], d),
                                              _tile_gain(diff_k_norm[j], d), summ, expand)
            lam_params = [p[j].reshape(1, HEAD_GROUP).astype(F32) for p in
                          (diff_lambda_q1, diff_lambda_k1, diff_lambda_q2, diff_lambda_k2)]
            subln = diff_subln[j].reshape(1, 2 * HEAD_GROUP)
            o_p = _diff_prompt(qb, kb, vb, slopes2, lam_params, subln, bp, sp, lambda_init)
            o_s = _diff_sample(qb, kb, vb, cache_dk, cache_dv, j, slopes2, lam_params, subln, n_p, bs, ss,
                               lambda_init)
            w_out_b = diff_w_out_b
            outs["dkp"].append(kf[:n_p].reshape(bp, sp, DIFF_HEADS, 2 * HEAD_GROUP))
            outs["dvp"].append(vf[:n_p].reshape(bp, sp, DIFF_HEADS, 2 * HEAD_GROUP))
            outs["dks"].append(kf[n_p:].reshape(bs, ss, DIFF_HEADS, 2 * HEAD_GROUP))
            outs["dvs"].append(vf[n_p:].reshape(bs, ss, DIFF_HEADS, 2 * HEAD_GROUP))
        else:
            bf = jnp.pad(fox_b_f[j], (0, LANES - FOX_HEADS)).reshape(1, LANES)
            qb, kf, kb, vf, vb, sg, lf = _inproj_fox(x, gm, fox_w_main_b, fox_w_f_b, j, bf,
                                                     _tile_gain(fox_q_norm[j], d),
                                                     _tile_gain(fox_k_norm[j], d), summ, expand)
            logf = lf[:, :FOX_HEADS]
            logf_p = logf[:n_p].reshape(bp, sp, FOX_HEADS)
            logf_s = logf[n_p:].reshape(bs, ss, FOX_HEADS)
            tk_p = ATT_TK if sp % ATT_TK == 0 else sp
            tk_c = ATT_TK if past % ATT_TK == 0 else past
            c_p = _pair_heads(_coff(logf_p.transpose(0, 2, 1), tk_p, False))
            c_new = _coff(logf_s.transpose(0, 2, 1), ss, False)
            c_new = c_new.reshape(bs, FOX_HEADS // 2, 2, ss)
            c_past = _pair_heads(_coff(cache_fox_logf[j].astype(F32).transpose(0, 2, 1), tk_c, True))
            o_p = _fox_prompt(qb, kb, vb, c_p, sg, bp, sp)
            o_s = _fox_sample(qb, kb, vb, cache_fk, cache_fv, j, c_past, c_new, sg, n_p, bs, ss)
            w_out_b = fox_w_out_b
            outs["fkp"].append(kf[:n_p].reshape(bp, sp, FOX_HEADS, HEAD_GROUP))
            outs["fvp"].append(vf[:n_p].reshape(bp, sp, FOX_HEADS, HEAD_GROUP))
            outs["flp"].append(logf_p)
            outs["fks"].append(kf[n_p:].reshape(bs, ss, FOX_HEADS, HEAD_GROUP))
            outs["fvs"].append(vf[n_p:].reshape(bs, ss, FOX_HEADS, HEAD_GROUP))
            outs["fls"].append(logf_s)
        o = jnp.concatenate([o_p, o_s], axis=0)
        gf = norm_ffn[i].reshape(1, d)
        x, meta = _outproj(o, x, w_out_b, j, gf, wr_hi[i], wr_lo[i], b_router[i].reshape(1, LANES))
        bucket = meta[:, 0].astype(jnp.int32)
        perm, sched = _moe_schedule(bucket, n)
        xs = x[perm]
        ms = meta[:, :8][perm]
        ys = _moe(sched, xs, ms, gf, wgu_b, wd_b, i)
        inv = jnp.zeros((n,), jnp.int32).at[perm].set(jnp.arange(n, dtype=jnp.int32))
        x = ys[inv]

    y_prompt = x[:n_p].reshape(bp, sp, d)
    y_sample = x[n_p:].reshape(bs, ss, d)
    stk = lambda name: jnp.stack(outs[name])
    return (y_prompt, y_sample, stk("dkp"), stk("dvp"), stk("fkp"), stk("fvp"), stk("flp"),
            stk("dks"), stk("dvs"), stk("fks"), stk("fvs"), stk("fls"))
```

```python
import functools
import math

import numpy as np
import jax
import jax.numpy as jnp
from jax import lax
from jax.experimental import pallas as pl
from jax.experimental.pallas import tpu as pltpu

F32 = jnp.float32
BF16 = jnp.bfloat16

CHUNK_SHIFT = 6
DIFF_HEADS = 8
FOX_HEADS = 16
HEAD_GROUP = 64
N_GROUPS = 4
EXPERTS_PER_GROUP = 4
N_EXPERTS = N_GROUPS * EXPERTS_PER_GROUP
N_PAIRS = 6
N_BUCKETS = N_GROUPS * N_PAIRS
EPS = 1e-6
NEG = -1e30
LOG2E = math.log2(math.e)
QSCALE = (HEAD_GROUP ** -0.5) * LOG2E

LANES = 128
TOK_TILE = 512
ATT_TQ = 512
ATT_TK = 512
SAMPLE_TK = 1024
MOE_TILE = 256
VMEM_LIMIT = 56 * 1024 * 1024

PAIR_A = np.array([0, 0, 0, 1, 1, 2], np.int32)
PAIR_B = np.array([1, 2, 3, 2, 3, 3], np.int32)


def _cparams(n_axes):
    return pltpu.CompilerParams(dimension_semantics=("arbitrary",) * n_axes,
                                vmem_limit_bytes=VMEM_LIMIT)


def _rms_rows(x, gain):
    ms = jnp.mean(x * x, axis=-1, keepdims=True)
    return x * lax.rsqrt(ms + EPS) * gain


def _group_norm(z, gain, sum_ref, exp_ref):
    ms = jnp.dot((z * z).astype(BF16), sum_ref[...], preferred_element_type=F32)
    r = lax.rsqrt(ms + EPS)
    r_hi = r.astype(BF16)
    r_lo = (r - r_hi.astype(F32)).astype(BF16)
    rb = jnp.dot(jnp.concatenate([r_hi, r_lo], axis=-1), exp_ref[...], preferred_element_type=F32)
    return z * rb * gain


def _group_mats(d):
    n_g = d // HEAD_GROUP
    col = np.arange(d) // HEAD_GROUP
    summ = np.zeros((d, LANES), np.float32)
    summ[np.arange(d), col] = 1.0 / HEAD_GROUP
    expand = np.zeros((2 * LANES, d), np.float32)
    expand[col, np.arange(d)] = 1.0
    expand[LANES + col, np.arange(d)] = 1.0
    assert n_g <= LANES
    return jnp.asarray(summ, BF16), jnp.asarray(expand, BF16)


def _inproj_diff_kernel(x_ref, gm_ref, w_ref, qg_ref, kg_ref, sum_ref, exp_ref,
                        qb_ref, kf_ref, kb_ref, vf_ref, vb_ref):
    d = x_ref.shape[1]
    h = _rms_rows(x_ref[...], gm_ref[...]).astype(BF16)
    q = jnp.dot(h, w_ref[:, 0:d], preferred_element_type=F32)
    qb_ref[...] = (_group_norm(q, qg_ref[...], sum_ref, exp_ref) * QSCALE).astype(BF16)
    k = jnp.dot(h, w_ref[:, d:2 * d], preferred_element_type=F32)
    kn = _group_norm(k, kg_ref[...], sum_ref, exp_ref)
    kf_ref[...] = kn
    kb_ref[...] = kn.astype(BF16)
    v = jnp.dot(h, w_ref[:, 2 * d:3 * d], preferred_element_type=F32)
    vf_ref[...] = v
    vb_ref[...] = v.astype(BF16)


def _inproj_diff(x, gm, w_all, layer, qg, kg, summ, expand):
    n, d = x.shape
    tm = TOK_TILE
    row = lambda i: (i, 0)
    fixed = lambda i: (0, 0)
    big = pl.BlockSpec((tm, d), row)
    return pl.pallas_call(
        _inproj_diff_kernel,
        grid=(n // tm,),
        in_specs=[big,
                  pl.BlockSpec((1, d), fixed),
                  pl.BlockSpec((None, d, 3 * d), lambda i: (layer, 0, 0)),
                  pl.BlockSpec((1, d), fixed), pl.BlockSpec((1, d), fixed),
                  pl.BlockSpec(summ.shape, fixed), pl.BlockSpec(expand.shape, fixed)],
        out_specs=[big, big, big, big, big],
        out_shape=[jax.ShapeDtypeStruct((n, d), BF16), jax.ShapeDtypeStruct((n, d), F32),
                   jax.ShapeDtypeStruct((n, d), BF16), jax.ShapeDtypeStruct((n, d), F32),
                   jax.ShapeDtypeStruct((n, d), BF16)],
        compiler_params=_cparams(1),
        name="inproj_diff",
    )(x, gm, w_all, qg, kg, summ, expand)


def _inproj_fox_kernel(x_ref, gm_ref, w_ref, wf_ref, bf_ref, qg_ref, kg_ref, sum_ref, exp_ref,
                       qb_ref, kf_ref, kb_ref, vf_ref, vb_ref, sg_ref, lf_ref):
    d = x_ref.shape[1]
    h = _rms_rows(x_ref[...], gm_ref[...]).astype(BF16)
    q = jnp.dot(h, w_ref[:, 0:d], preferred_element_type=F32)
    qb_ref[...] = (_group_norm(q, qg_ref[...], sum_ref, exp_ref) * QSCALE).astype(BF16)
    k = jnp.dot(h, w_ref[:, d:2 * d], preferred_element_type=F32)
    kn = _group_norm(k, kg_ref[...], sum_ref, exp_ref)
    kf_ref[...] = kn
    kb_ref[...] = kn.astype(BF16)
    v = jnp.dot(h, w_ref[:, 2 * d:3 * d], preferred_element_type=F32)
    vf_ref[...] = v
    vb_ref[...] = v.astype(BF16)
    g = jnp.dot(h, w_ref[:, 3 * d:4 * d], preferred_element_type=F32)
    sg_ref[...] = (1.0 / (1.0 + jnp.exp(-g))).astype(BF16)
    f = jnp.dot(h, wf_ref[...], preferred_element_type=F32) + bf_ref[...]
    lf_ref[...] = jnp.minimum(f, 0.0) - jnp.log(1.0 + jnp.exp(-jnp.abs(f)))


def _inproj_fox(x, gm, w_all, wf_all, layer, bf, qg, kg, summ, expand):
    n, d = x.shape
    tm = TOK_TILE
    row = lambda i: (i, 0)
    fixed = lambda i: (0, 0)
    big = pl.BlockSpec((tm, d), row)
    return pl.pallas_call(
        _inproj_fox_kernel,
        grid=(n // tm,),
        in_specs=[big,
                  pl.BlockSpec((1, d), fixed),
                  pl.BlockSpec((None, d, 4 * d), lambda i: (layer, 0, 0)),
                  pl.BlockSpec((None, d, LANES), lambda i: (layer, 0, 0)),
                  pl.BlockSpec((1, LANES), fixed),
                  pl.BlockSpec((1, d), fixed), pl.BlockSpec((1, d), fixed),
                  pl.BlockSpec(summ.shape, fixed), pl.BlockSpec(expand.shape, fixed)],
        out_specs=[big, big, big, big, big, big, pl.BlockSpec((tm, LANES), row)],
        out_shape=[jax.ShapeDtypeStruct((n, d), BF16), jax.ShapeDtypeStruct((n, d), F32),
                   jax.ShapeDtypeStruct((n, d), BF16), jax.ShapeDtypeStruct((n, d), F32),
                   jax.ShapeDtypeStruct((n, d), BF16), jax.ShapeDtypeStruct((n, d), BF16),
                   jax.ShapeDtypeStruct((n, LANES), F32)],
        compiler_params=_cparams(1),
        name="inproj_fox",
    )(x, gm, w_all, wf_all, bf, qg, kg, summ, expand)


def _half_masks(rows):
    lane = lax.broadcasted_iota(jnp.int32, (rows, LANES), 1)
    return lane < HEAD_GROUP, lane >= HEAD_GROUP


def _stream_update(c, s, v_blk, m_ref, l_ref, acc_ref):
    m_prev = m_ref[c]
    m_new = jnp.maximum(m_prev, jnp.max(s, axis=-1, keepdims=True))
    alpha = jnp.exp2(m_prev - m_new)
    p = jnp.exp2(s - m_new)
    l_ref[c] = alpha * l_ref[c] + jnp.sum(p, axis=-1, keepdims=True)
    acc_ref[c] = alpha * acc_ref[c] + jnp.dot(p.astype(BF16), v_blk, preferred_element_type=F32)
    m_ref[c] = m_new


def _attend_block(q, k_blk, v_blk, bias0, bias1, m_ref, l_ref, acc_ref):
    lo, hi = _half_masks(k_blk.shape[0])
    zero = jnp.zeros_like(k_blk)
    dims = (((1,), (1,)), ((), ()))
    s0 = lax.dot_general(q, jnp.where(lo, k_blk, zero), dims, preferred_element_type=F32) + bias0
    _stream_update(0, s0, v_blk, m_ref, l_ref, acc_ref)
    s1 = lax.dot_general(q, jnp.where(hi, k_blk, zero), dims, preferred_element_type=F32) + bias1
    _stream_update(1, s1, v_blk, m_ref, l_ref, acc_ref)


def _init_state(m_ref, l_ref, acc_ref):
    m_ref[...] = jnp.full(m_ref.shape, NEG, F32)
    l_ref[...] = jnp.zeros(l_ref.shape, F32)
    acc_ref[...] = jnp.zeros(acc_ref.shape, F32)


def _diff_lambda(lq1, lk1, lq2, lk2, lambda_init):
    a = jnp.exp(jnp.sum(lq1[...] * lk1[...], axis=-1, keepdims=True))
    b = jnp.exp(jnp.sum(lq2[...] * lk2[...], axis=-1, keepdims=True))
    return a - b + lambda_init


def _diff_finish(lam, g_ref, o_ref, l_ref, acc_ref, lambda_init):
    o = acc_ref[0] / l_ref[0] - lam * (acc_ref[1] / l_ref[1])
    ms = jnp.mean(o * o, axis=-1, keepdims=True)
    o_ref[...] = ((o * lax.rsqrt(ms + EPS) * g_ref[...]) * (1.0 - lambda_init)).astype(o_ref.dtype)


def _alibi_table(slope2, rows, cols, q0, k0):
    i = lax.broadcasted_iota(jnp.int32, (rows, cols), 0)
    j = lax.broadcasted_iota(jnp.int32, (rows, cols), 1)
    qpos = i + q0
    kpos = j + k0
    allowed = jnp.right_shift(kpos, CHUNK_SHIFT) <= jnp.right_shift(qpos, CHUNK_SHIFT)
    bias = slope2 * (i - jnp.abs(qpos - kpos)).astype(F32)
    return jnp.where(allowed, bias, NEG)


BIAS_TERMS = 3


def _bias_lanes(c):
    return HEAD_GROUP * (1 - c)


def _own_lanes(lane, c):
    return (lane >= HEAD_GROUP * c) & (lane < HEAD_GROUP * (c + 1))


def _augment_keys(k_blk, bias, c):
    lane = lax.broadcasted_iota(jnp.int32, k_blk.shape, 1)
    b0 = _bias_lanes(c)
    out = jnp.where(_own_lanes(lane, c), k_blk, jnp.zeros_like(k_blk))
    rest = bias
    for t in range(BIAS_TERMS):
        term = rest.astype(BF16)
        out = jnp.where(lane == b0 + t, term, out)
        rest = rest - term.astype(F32)
    return out


def _augment_queries(q, c):
    lane = lax.broadcasted_iota(jnp.int32, q.shape, 1)
    b0 = _bias_lanes(c)
    ones = jnp.where((lane >= b0) & (lane < b0 + BIAS_TERMS), 1.0, 0.0).astype(q.dtype)
    return jnp.where(_own_lanes(lane, c), q, ones)


def _transpose_values(v_ref, vt_ref):
    nblk, _, tk = vt_ref.shape
    for blk in range(nblk):
        vt_ref[blk] = v_ref[blk * tk:(blk + 1) * tk, :].astype(F32).T.astype(vt_ref.dtype)


def _block_t(c, ka, qa, vt, tbl, q_lo, m_ref, l_ref, acc_ref):
    dims = (((1,), (1,)), ((), ()))
    st = lax.dot_general(ka, qa[q_lo:, :], dims, preferred_element_type=F32)
    if tbl is not None:
        st = st + tbl
    m_prev = m_ref[c, :, q_lo:]
    m_new = jnp.maximum(m_prev, jnp.max(st, axis=0, keepdims=True))
    alpha = jnp.exp2(m_prev - m_new)
    pt = jnp.exp2(st - m_new)
    l_ref[c, :, q_lo:] = alpha * l_ref[c, :, q_lo:] + jnp.sum(pt, axis=0, keepdims=True)
    acc_ref[c, :, q_lo:] = alpha * acc_ref[c, :, q_lo:] + jnp.dot(vt, pt.astype(BF16),
                                                                  preferred_element_type=F32)
    m_ref[c, :, q_lo:] = m_new


def _sweep_t(qi, qa, kaug_ref, vt_ref, tbl_ref, m_ref, l_ref, acc_ref, v_rows):
    tk = vt_ref.shape[2]
    tq = qa[0].shape[0]
    per_q = tq // tk

    def body(ki, carry):
        k0 = pl.multiple_of(ki * tk, tk)
        vt = vt_ref[ki]
        for c in range(2):
            _block_t(c, kaug_ref[c, pl.ds(k0, tk), :], qa[c], vt[v_rows[c], :], None, 0,
                     m_ref, l_ref, acc_ref)
        return carry

    lax.fori_loop(0, qi * per_q, body, 0)
    for dblk in range(per_q):
        ki = qi * per_q + dblk
        k0 = pl.multiple_of(ki * tk, tk)
        vt = vt_ref[ki]
        q_lo = dblk * tk
        for c in range(2):
            _block_t(c, kaug_ref[c, pl.ds(k0, tk), :], qa[c], vt[v_rows[c], :],
                     tbl_ref[dblk, :, q_lo:], q_lo, m_ref, l_ref, acc_ref)


def _diff_prompt_kernel(lambda_init, sl_ref, lq1, lk1, lq2, lk2, q_ref, k_ref, v_ref, g_ref,
                        o_ref, kaug_ref, vt_ref, m_ref, l_ref, acc_ref, tbl_ref):
    tq = q_ref.shape[0]
    seq = k_ref.shape[0]
    nblk, _, tk = vt_ref.shape
    qi = pl.program_id(2)
    slope2 = sl_ref[pl.program_id(1)]

    @pl.when(qi == 0)
    def _():
        for blk in range(nblk):
            rows = slice(blk * tk, (blk + 1) * tk)
            kpos = lax.broadcasted_iota(jnp.int32, (tk, LANES), 0) + blk * tk
            bias = slope2 * kpos.astype(F32)
            for c in range(2):
                kaug_ref[c, rows, :] = _augment_keys(k_ref[rows, :], bias, c)
        _transpose_values(v_ref, vt_ref)
        for dblk in range(tq // tk):
            dj = lax.broadcasted_iota(jnp.int32, (tk, tq), 0) + dblk * tk
            i = lax.broadcasted_iota(jnp.int32, (tk, tq), 1)
            allowed = jnp.right_shift(dj, CHUNK_SHIFT) <= jnp.right_shift(i, CHUNK_SHIFT)
            ahead = jnp.maximum(dj - i, 0).astype(F32)
            tbl_ref[dblk] = jnp.where(allowed, (-2.0 * slope2) * ahead, NEG)

    _init_state(m_ref, l_ref, acc_ref)
    q = q_ref[...]
    qa = [_augment_queries(q, c) for c in range(2)]
    whole = slice(None)
    _sweep_t(qi, qa, kaug_ref, vt_ref, tbl_ref, m_ref, l_ref, acc_ref, (whole, whole))
    lam = _diff_lambda(lq1, lk1, lq2, lk2, lambda_init)
    ot = acc_ref[0] / l_ref[0] - lam * (acc_ref[1] / l_ref[1])
    ms = jnp.mean(ot * ot, axis=0, keepdims=True)
    o = (ot * lax.rsqrt(ms + EPS)).T
    o_ref[...] = ((o * g_ref[...]) * (1.0 - lambda_init)).astype(o_ref.dtype)


def _diff_prompt(qb, kb, vb, slopes2, lam_params, subln, n_batch, seq, lambda_init):
    d = qb.shape[1]
    tq = min(ATT_TQ, seq)
    tk = min(ATT_TK, tq)
    nq = seq // tq
    hd2 = 2 * HEAD_GROUP
    small = pl.BlockSpec((1, HEAD_GROUP), lambda b, h, i: (0, 0))
    return pl.pallas_call(
        functools.partial(_diff_prompt_kernel, lambda_init),
        grid=(n_batch, DIFF_HEADS, nq),
        in_specs=[pl.BlockSpec(memory_space=pltpu.SMEM), small, small, small, small,
                  pl.BlockSpec((tq, hd2), lambda b, h, i: (b * nq + i, h)),
                  pl.BlockSpec((seq, hd2), lambda b, h, i: (b, h)),
                  pl.BlockSpec((seq, hd2), lambda b, h, i: (b, h)),
                  pl.BlockSpec((1, hd2), lambda b, h, i: (0, 0))],
        out_specs=pl.BlockSpec((tq, hd2), lambda b, h, i: (b * nq + i, h)),
        out_shape=jax.ShapeDtypeStruct((n_batch * seq, d), BF16),
        scratch_shapes=[pltpu.VMEM((2, seq, hd2), BF16), pltpu.VMEM((seq // tk, hd2, tk), BF16),
                        pltpu.VMEM((2, 1, tq), F32), pltpu.VMEM((2, 1, tq), F32),
                        pltpu.VMEM((2, hd2, tq), F32), pltpu.VMEM((tq // tk, tk, tq), F32)],
        compiler_params=_cparams(3),
        name="diff_attn_prompt",
    )(slopes2, *lam_params, qb, kb, vb, subln)


def _diff_sample_kernel(lambda_init, past, sl_ref, lq1, lk1, lq2, lk2, q_ref, kp_ref, vp_ref,
                        kn_ref, vn_ref, g_ref, o_ref, m_ref, l_ref, acc_ref):
    tq = q_ref.shape[0]
    tk = SAMPLE_TK if past % SAMPLE_TK == 0 else past
    slope2 = sl_ref[pl.program_id(1)]
    _init_state(m_ref, l_ref, acc_ref)
    q = q_ref[...]
    col = lax.broadcasted_iota(jnp.int32, (1, tk), 1)

    def body(ki, carry):
        k0 = pl.multiple_of(ki * tk, tk)
        bias = slope2 * (col + (k0 - past)).astype(F32)
        _attend_block(q, kp_ref[pl.ds(k0, tk), :].astype(BF16), vp_ref[pl.ds(k0, tk), :].astype(BF16),
                      bias, bias, m_ref, l_ref, acc_ref)
        return carry

    lax.fori_loop(0, past // tk, body, 0)
    bias = _alibi_table(slope2, tq, tq, past, past)
    _attend_block(q, kn_ref[...], vn_ref[...], bias, bias, m_ref, l_ref, acc_ref)
    lam = _diff_lambda(lq1, lk1, lq2, lk2, lambda_init)
    _diff_finish(lam, g_ref, o_ref, l_ref, acc_ref, lambda_init)


def _diff_sample(qb, kb, vb, cache_k, cache_v, layer, slopes2, lam_params, subln, row0, n_batch, seq,
                 lambda_init):
    d = qb.shape[1]
    past = cache_k.shape[2]
    hd2 = 2 * HEAD_GROUP
    blk0 = row0 // seq
    small = pl.BlockSpec((1, HEAD_GROUP), lambda b, h: (0, 0))
    new = pl.BlockSpec((seq, hd2), lambda b, h: (blk0 + b, h))
    old = pl.BlockSpec((None, None, past, hd2), lambda b, h: (layer, b, 0, h))
    return pl.pallas_call(
        functools.partial(_diff_sample_kernel, lambda_init, past),
        grid=(n_batch, DIFF_HEADS),
        in_specs=[pl.BlockSpec(memory_space=pltpu.SMEM), small, small, small, small,
                  new, old, old, new, new, pl.BlockSpec((1, hd2), lambda b, h: (0, 0))],
        out_specs=pl.BlockSpec((seq, hd2), lambda b, h: (b, h)),
        out_shape=jax.ShapeDtypeStruct((n_batch * seq, d), BF16),
        scratch_shapes=[pltpu.VMEM((2, seq, 1), F32), pltpu.VMEM((2, seq, 1), F32),
                        pltpu.VMEM((2, seq, hd2), F32)],
        compiler_params=_cparams(2),
        name="diff_attn_sample",
    )(slopes2, *lam_params, qb, cache_k, cache_v, kb, vb, subln)


def _causal_table(rows, cols):
    i = lax.broadcasted_iota(jnp.int32, (rows, cols), 0)
    j = lax.broadcasted_iota(jnp.int32, (rows, cols), 1)
    return jnp.where(j <= i, 0.0, NEG).astype(F32)


def _fox_finish(sg_ref, o_ref, l_ref, acc_ref):
    lo, _ = _half_masks(o_ref.shape[0])
    o = jnp.where(lo, acc_ref[0] / l_ref[0], acc_ref[1] / l_ref[1])
    o_ref[...] = (o * sg_ref[...].astype(F32)).astype(o_ref.dtype)


def _fox_prompt_kernel(q_ref, k_ref, v_ref, lf_ref, sg_ref, o_ref,
                       cum_ref, kaug_ref, vt_ref, m_ref, l_ref, acc_ref, tbl_ref):
    tq = q_ref.shape[0]
    nblk, _, tk = vt_ref.shape
    pair = pl.program_id(1)
    qi = pl.program_id(2)

    @pl.when((pair == 0) & (qi == 0))
    def _():
        r = lax.broadcasted_iota(jnp.int32, (tk, tk), 0)
        c = lax.broadcasted_iota(jnp.int32, (tk, tk), 1)
        tril = jnp.where(c <= r, 1.0, 0.0).astype(BF16)
        carry = jnp.zeros((1, LANES), F32)
        for blk in range(nblk):
            x = lf_ref[blk * tk:(blk + 1) * tk, :]
            hi = x.astype(BF16)
            lo = (x - hi.astype(F32)).astype(BF16)
            part = (jnp.dot(tril, hi, preferred_element_type=F32)
                    + jnp.dot(tril, lo, preferred_element_type=F32)) + carry
            cum_ref[blk * tk:(blk + 1) * tk, :] = part * (-LOG2E)
            carry = carry + jnp.sum(x, axis=0, keepdims=True)

    @pl.when(qi == 0)
    def _():
        lane = lax.broadcasted_iota(jnp.int32, (tk, LANES), 1)
        for blk in range(nblk):
            rows = slice(blk * tk, (blk + 1) * tk)
            cum = cum_ref[rows, :]
            for c in range(2):
                col = jnp.sum(jnp.where(lane == 2 * pair + c, cum, 0.0), axis=-1, keepdims=True)
                bias = jnp.broadcast_to(col, (tk, LANES))
                kaug_ref[c, rows, :] = _augment_keys(k_ref[rows, :], bias, c)
        _transpose_values(v_ref, vt_ref)
        for dblk in range(tq // tk):
            dj = lax.broadcasted_iota(jnp.int32, (tk, tq), 0) + dblk * tk
            i = lax.broadcasted_iota(jnp.int32, (tk, tq), 1)
            tbl_ref[dblk] = jnp.where(dj <= i, 0.0, NEG).astype(F32)

    _init_state(m_ref, l_ref, acc_ref)
    q = q_ref[...]
    qa = [_augment_queries(q, c) for c in range(2)]
    halves = (slice(0, HEAD_GROUP), slice(HEAD_GROUP, 2 * HEAD_GROUP))
    _sweep_t(qi, qa, kaug_ref, vt_ref, tbl_ref, m_ref, l_ref, acc_ref, halves)
    ot = jnp.concatenate([acc_ref[0] / l_ref[0], acc_ref[1] / l_ref[1]], axis=0)
    o_ref[...] = (ot.T * sg_ref[...].astype(F32)).astype(o_ref.dtype)


def _fox_prompt(qb, kb, vb, lf, sg, n_batch, seq):
    d = qb.shape[1]
    tq = min(ATT_TQ, seq)
    tk = min(ATT_TK, tq)
    nq = seq // tq
    w = LANES
    qspec = pl.BlockSpec((tq, w), lambda b, p, i: (b * nq + i, p))
    kspec = pl.BlockSpec((seq, w), lambda b, p, i: (b, p))
    return pl.pallas_call(
        _fox_prompt_kernel,
        grid=(n_batch, d // w, nq),
        in_specs=[qspec, kspec, kspec,
                  pl.BlockSpec((seq, LANES), lambda b, p, i: (b, 0)),
                  qspec],
        out_specs=qspec,
        out_shape=jax.ShapeDtypeStruct((n_batch * seq, d), BF16),
        scratch_shapes=[pltpu.VMEM((seq, LANES), F32),
                        pltpu.VMEM((2, seq, w), BF16), pltpu.VMEM((seq // tk, w, tk), BF16),
                        pltpu.VMEM((2, 1, tq), F32), pltpu.VMEM((2, 1, tq), F32),
                        pltpu.VMEM((2, HEAD_GROUP, tq), F32), pltpu.VMEM((tq // tk, tk, tq), F32)],
        compiler_params=_cparams(3),
        name="fox_attn_prompt",
    )(qb, kb, vb, lf, sg)


def _fox_sample_kernel(past, q_ref, kp_ref, vp_ref, kn_ref, vn_ref, cp_ref, cn_ref, sg_ref,
                       o_ref, m_ref, l_ref, acc_ref):
    tq = q_ref.shape[0]
    tk = cp_ref.shape[-1]
    _init_state(m_ref, l_ref, acc_ref)
    q = q_ref[...]

    def body(ki, carry):
        k0 = pl.multiple_of(ki * tk, tk)
        c = cp_ref[ki]
        _attend_block(q, kp_ref[pl.ds(k0, tk), :].astype(BF16), vp_ref[pl.ds(k0, tk), :].astype(BF16),
                      c[0:1, :], c[1:2, :], m_ref, l_ref, acc_ref)
        return carry

    lax.fori_loop(0, past // tk, body, 0)
    tbl = _causal_table(tq, tq)
    c = cn_ref[...]
    _attend_block(q, kn_ref[...], vn_ref[...], tbl + c[0:1, :], tbl + c[1:2, :], m_ref, l_ref, acc_ref)
    _fox_finish(sg_ref, o_ref, l_ref, acc_ref)


def _fox_sample(qb, kb, vb, cache_k, cache_v, layer, c_past, c_new, sg, row0, n_batch, seq):
    d = qb.shape[1]
    past = cache_k.shape[2]
    nblk, tk = c_past.shape[2], c_past.shape[4]
    w = LANES
    blk0 = row0 // seq
    new = pl.BlockSpec((seq, w), lambda b, p: (blk0 + b, p))
    old = pl.BlockSpec((None, None, past, w), lambda b, p: (layer, b, 0, p))
    return pl.pallas_call(
        functools.partial(_fox_sample_kernel, past),
        grid=(n_batch, d // w),
        in_specs=[new, old, old, new, new,
                  pl.BlockSpec((None, None, nblk, 2, tk), lambda b, p: (b, p, 0, 0, 0)),
                  pl.BlockSpec((None, None, 2, seq), lambda b, p: (b, p, 0, 0)),
                  new],
        out_specs=pl.BlockSpec((seq, w), lambda b, p: (b, p)),
        out_shape=jax.ShapeDtypeStruct((n_batch * seq, d), BF16),
        scratch_shapes=[pltpu.VMEM((2, seq, 1), F32), pltpu.VMEM((2, seq, 1), F32),
                        pltpu.VMEM((2, seq, w), F32)],
        compiler_params=_cparams(2),
        name="fox_attn_sample",
    )(qb, cache_k, cache_v, kb, vb, c_past, c_new, sg)


def _split_dot(x, mat):
    hi = x.astype(BF16)
    lo = (x - hi.astype(F32)).astype(BF16)
    return (jnp.dot(hi, mat, preferred_element_type=F32) + jnp.dot(lo, mat, preferred_element_type=F32))


def _coff_kernel(suffix, x_ref, o_ref):
    nblk, _, tk = o_ref.shape
    r = lax.broadcasted_iota(jnp.int32, (tk, tk), 0)
    c = lax.broadcasted_iota(jnp.int32, (tk, tk), 1)
    mat = jnp.where((r > c) if suffix else (r <= c), 1.0, 0.0).astype(BF16)
    carry = jnp.zeros((x_ref.shape[0], 1), F32)
    order = range(nblk - 1, -1, -1) if suffix else range(nblk)
    for blk in order:
        x = x_ref[:, blk * tk:(blk + 1) * tk]
        part = _split_dot(x, mat) + carry
        o_ref[blk] = (part if suffix else -part) * LOG2E
        carry = carry + jnp.sum(x, axis=-1, keepdims=True)


def _coff(xt, tk, suffix):
    n_batch, n_head, length = xt.shape
    nblk = length // tk
    return pl.pallas_call(
        functools.partial(_coff_kernel, suffix),
        grid=(n_batch,),
        in_specs=[pl.BlockSpec((None, n_head, length), lambda b: (b, 0, 0))],
        out_specs=pl.BlockSpec((None, nblk, n_head, tk), lambda b: (b, 0, 0, 0)),
        out_shape=jax.ShapeDtypeStruct((n_batch, nblk, n_head, tk), F32),
        compiler_params=_cparams(1),
        name="fox_coff_suffix" if suffix else "fox_coff_prefix",
    )(xt)


def _pair_heads(c):
    n_batch, nblk, n_head, tk = c.shape
    return c.reshape(n_batch, nblk, n_head // 2, 2, tk).transpose(0, 2, 1, 3, 4)


def _outproj_kernel(o_ref, x_ref, w_ref, gf_ref, wr_hi_ref, wr_lo_ref, br_ref, xo_ref, meta_ref):
    xn = x_ref[...] + jnp.dot(o_ref[...], w_ref[...], preferred_element_type=F32)
    xo_ref[...] = xn
    t = _rms_rows(xn, gf_ref[...])
    t_hi = t.astype(BF16)
    t_lo = (t - t_hi.astype(F32)).astype(BF16)
    logits = (jnp.dot(t_hi, wr_hi_ref[...], preferred_element_type=F32)
              + jnp.dot(t_lo, wr_hi_ref[...], preferred_element_type=F32)
              + jnp.dot(t_hi, wr_lo_ref[...], preferred_element_type=F32)) + br_ref[...]
    rows = logits.shape[0]
    lane = lax.broadcasted_iota(jnp.int32, (rows, LANES), 1).astype(F32)
    big = float(LANES)

    def first_argmax(vals, vmax):
        return jnp.min(jnp.where(vals == vmax, lane, big), axis=-1, keepdims=True)

    gl = jnp.where(lane < N_GROUPS, logits, NEG)
    gmax = jnp.max(gl, axis=-1, keepdims=True)
    gsum = jnp.sum(jnp.where(lane < N_GROUPS, jnp.exp(logits - gmax), 0.0), axis=-1, keepdims=True)
    p_top = 1.0 / gsum
    g_idx = first_argmax(gl, gmax)
    base = N_GROUPS + EXPERTS_PER_GROUP * g_idx
    el = jnp.where((lane >= base) & (lane < base + EXPERTS_PER_GROUP), logits, NEG)
    v1 = jnp.max(el, axis=-1, keepdims=True)
    i1 = first_argmax(el, v1)
    el2 = jnp.where(lane == i1, NEG, el)
    v2 = jnp.max(el2, axis=-1, keepdims=True)
    i2 = first_argmax(el2, v2)
    e2 = jnp.exp(v2 - v1)
    w1 = p_top / (1.0 + e2)
    w2 = p_top * e2 / (1.0 + e2)
    a1 = i1 - base
    a2 = i2 - base
    lo = jnp.minimum(a1, a2)
    hi = jnp.maximum(a1, a2)
    wa = jnp.where(a1 < a2, w1, w2)
    wb = jnp.where(a1 < a2, w2, w1)
    pair = jnp.where(lo == 0.0, hi - 1.0, jnp.where(lo == 1.0, hi + 1.0, 5.0))
    bucket = g_idx * N_PAIRS + pair
    meta_ref[...] = jnp.where(lane == 0.0, bucket, jnp.where(lane == 1.0, wa, jnp.where(lane == 2.0, wb, 0.0)))


def _outproj(o, x, w_all, layer, gf, wr_hi, wr_lo, br):
    n, d = x.shape
    tm = TOK_TILE
    row = lambda i: (i, 0)
    fixed = lambda i: (0, 0)
    big = pl.BlockSpec((tm, d), row)
    return pl.pallas_call(
        _outproj_kernel,
        grid=(n // tm,),
        in_specs=[big, big,
                  pl.BlockSpec((None, d, d), lambda i: (layer, 0, 0)),
                  pl.BlockSpec((1, d), fixed),
                  pl.BlockSpec((d, LANES), fixed), pl.BlockSpec((d, LANES), fixed),
                  pl.BlockSpec((1, LANES), fixed)],
        out_specs=[big, pl.BlockSpec((tm, LANES), row)],
        out_shape=[jax.ShapeDtypeStruct((n, d), F32), jax.ShapeDtypeStruct((n, LANES), F32)],
        compiler_params=_cparams(1),
        name="outproj_router",
    )(o, x, w_all, gf, wr_hi, wr_lo, br)


def _moe_kernel(tile_ref, ea_ref, eb_ref, lo_ref, hi_ref, first_ref,
                xs_ref, ms_ref, gf_ref, wgu_a, wd_a, wgu_b, wd_b, out_ref):
    s = pl.program_id(0)
    lo = lo_ref[s]
    hi = hi_ref[s]

    @pl.when(first_ref[s] == 1)
    def _():
        out_ref[...] = xs_ref[...]

    @pl.when(hi > lo)
    def _():
        t = _rms_rows(xs_ref[...], gf_ref[...]).astype(BF16)
        rows = t.shape[0]
        r = lax.broadcasted_iota(jnp.int32, (rows, 1), 0)
        inside = (r >= lo) & (r < hi)
        ms = ms_ref[...]
        wa = jnp.where(inside, ms[:, 1:2], 0.0)
        wb = jnp.where(inside, ms[:, 2:3], 0.0)

        def expert(wgu, wd):
            de = wd.shape[0]
            gu = jnp.dot(t, wgu[...], preferred_element_type=F32)
            g = gu[:, :de]
            hid = (g / (1.0 + jnp.exp(-g)) * gu[:, de:]).astype(BF16)
            return jnp.dot(hid, wd[...], preferred_element_type=F32)

        out_ref[...] += wa * expert(wgu_a, wd_a) + wb * expert(wgu_b, wd_b)


def _moe(sched, xs, ms, gf, wgu_all, wd_all, layer):
    n, d = xs.shape
    de = wd_all.shape[2]
    n_steps = sched[0].shape[0]
    tile_map = lambda s, tile, ea, eb, lo, hi, first: (tile[s], 0)
    fixed = lambda s, *_: (0, 0)
    wa_map = lambda s, tile, ea, eb, lo, hi, first: (layer, ea[s], 0, 0)
    wb_map = lambda s, tile, ea, eb, lo, hi, first: (layer, eb[s], 0, 0)
    grid_spec = pltpu.PrefetchScalarGridSpec(
        num_scalar_prefetch=6,
        grid=(n_steps,),
        in_specs=[pl.BlockSpec((MOE_TILE, d), tile_map),
                  pl.BlockSpec((MOE_TILE, ms.shape[1]), tile_map),
                  pl.BlockSpec((1, d), fixed),
                  pl.BlockSpec((None, None, d, 2 * de), wa_map),
                  pl.BlockSpec((None, None, de, d), wa_map),
                  pl.BlockSpec((None, None, d, 2 * de), wb_map),
                  pl.BlockSpec((None, None, de, d), wb_map)],
        out_specs=pl.BlockSpec((MOE_TILE, d), tile_map),
    )
    return pl.pallas_call(
        _moe_kernel,
        grid_spec=grid_spec,
        out_shape=jax.ShapeDtypeStruct((n, d), F32),
        compiler_params=_cparams(1),
        name="moe_experts",
    )(*sched, xs, ms, gf, wgu_all, wd_all, wgu_all, wd_all)


def _moe_schedule(bucket, n):
    n_tiles = n // MOE_TILE
    n_steps = n_tiles + N_BUCKETS - 1
    perm = jnp.argsort(bucket, stable=True).astype(jnp.int32)
    sorted_b = bucket[perm]
    counts = jnp.zeros((N_BUCKETS,), jnp.int32).at[bucket].add(1)
    offs = jnp.concatenate([jnp.zeros((1,), jnp.int32), jnp.cumsum(counts)])
    fb = sorted_b[::MOE_TILE]
    lb = sorted_b[MOE_TILE - 1::MOE_TILE]
    per_tile = lb - fb + 1
    starts = jnp.cumsum(per_tile) - per_tile
    total = jnp.sum(per_tile)
    s = jnp.arange(n_steps, dtype=jnp.int32)
    tile = jnp.clip(jnp.searchsorted(starts, s, side="right").astype(jnp.int32) - 1, 0, n_tiles - 1)
    valid = s < total
    bkt = jnp.where(valid, fb[tile] + (s - starts[tile]), lb[n_tiles - 1])
    lo = jnp.clip(offs[bkt] - tile * MOE_TILE, 0, MOE_TILE)
    hi = jnp.clip(offs[bkt + 1] - tile * MOE_TILE, 0, MOE_TILE)
    lo = jnp.where(valid, lo, 0)
    hi = jnp.where(valid, hi, 0)
    first = (valid & (s == starts[tile])).astype(jnp.int32)
    grp = bkt // N_PAIRS
    ea = grp * EXPERTS_PER_GROUP + jnp.asarray(PAIR_A)[bkt % N_PAIRS]
    eb = grp * EXPERTS_PER_GROUP + jnp.asarray(PAIR_B)[bkt % N_PAIRS]
    return perm, (tile, ea.astype(jnp.int32), eb.astype(jnp.int32), lo.astype(jnp.int32),
                  hi.astype(jnp.int32), first)


def _tile_gain(g, d):
    return jnp.tile(g.astype(F32), d // g.shape[0]).reshape(1, d)


def kernel(x_prompt, x_sample, cache_diff_k, cache_diff_v, cache_fox_k, cache_fox_v, cache_fox_logf, norm_mix, norm_ffn, diff_w_in, diff_w_out, diff_q_norm, diff_k_norm, diff_lambda_q1, diff_lambda_k1, diff_lambda_q2, diff_lambda_k2, diff_subln, fox_w_in, fox_b_f, fox_w_out, fox_q_norm, fox_k_norm, moe_w_group, moe_b_group, moe_w_expert, moe_b_expert, moe_w_gate, moe_w_up, moe_w_down):
    bp, sp, d = x_prompt.shape
    bs, ss, _ = x_sample.shape
    n_p = bp * sp
    n_s = bs * ss
    n = n_p + n_s
    depth = norm_mix.shape[0]
    past = cache_diff_k.shape[2]
    assert n % TOK_TILE == 0 and n % MOE_TILE == 0 and n_p % TOK_TILE == 0
    assert d % LANES == 0 and sp % min(ATT_TQ, sp) == 0 and n_p % ss == 0
    assert DIFF_HEADS * 2 * HEAD_GROUP == d and FOX_HEADS * HEAD_GROUP == d

    summ, expand = _group_mats(d)
    x = jnp.concatenate([x_prompt.reshape(n_p, d), x_sample.reshape(n_s, d)], axis=0)

    diff_w_in_b = diff_w_in.astype(BF16)
    diff_w_out_b = diff_w_out.astype(BF16)
    fox_w_main_b = fox_w_in[:, :, :4 * d].astype(BF16)
    fox_w_f_b = jnp.pad(fox_w_in[:, :, 4 * d:], ((0, 0), (0, 0), (0, LANES - FOX_HEADS))).astype(BF16)
    fox_w_out_b = fox_w_out.astype(BF16)
    wgu_b = jnp.concatenate([moe_w_gate, moe_w_up], axis=-1).astype(BF16)
    wd_b = moe_w_down.astype(BF16)
    w_router = jnp.pad(jnp.concatenate([moe_w_group, moe_w_expert], axis=-1),
                       ((0, 0), (0, 0), (0, LANES - N_GROUPS - N_EXPERTS)))
    wr_hi = w_router.astype(BF16)
    wr_lo = (w_router - wr_hi.astype(F32)).astype(BF16)
    b_router = jnp.pad(jnp.concatenate([moe_b_group, moe_b_expert], axis=-1),
                       ((0, 0), (0, LANES - N_GROUPS - N_EXPERTS)))
    slopes2 = jnp.asarray(2.0 ** (-8.0 * np.arange(1, DIFF_HEADS + 1) / DIFF_HEADS) * LOG2E, F32)
    cache_dk = cache_diff_k.reshape(cache_diff_k.shape[:3] + (d,))
    cache_dv = cache_diff_v.reshape(cache_diff_v.shape[:3] + (d,))
    cache_fk = cache_fox_k.reshape(cache_fox_k.shape[:3] + (d,))
    cache_fv = cache_fox_v.reshape(cache_fox_v.shape[:3] + (d,))

    outs = {name: [] for name in ("dkp", "dvp", "dks", "dvs", "fkp", "fvp", "flp", "fks", "fvs", "fls")}
    for i in range(depth):
        j = i // 2
        gm = norm_mix[i].reshape(1, d)
        if i % 2 == 0:
            lambda_init = 0.8 - 0.6 * math.exp(-0.3 * i)
            qb, kf, kb, vf, vb = _inproj_diff(x, gm, diff_w_in_b, j, _tile_gain(diff_q_norm[j], d),
                                              _tile_gain(diff_k_norm[j], d), summ, expand)
            lam_params = [p[j].reshape(1, HEAD_GROUP).astype(F32) for p in
                          (diff_lambda_q1, diff_lambda_k1, diff_lambda_q2, diff_lambda_k2)]
            subln = diff_subln[j].reshape(1, 2 * HEAD_GROUP)
            o_p = _diff_prompt(qb, kb, vb, slopes2, lam_params, subln, bp, sp, lambda_init)
            o_s = _diff_sample(qb, kb, vb, cache_dk, cache_dv, j, slopes2, lam_params, subln, n_p, bs, ss,
                               lambda_init)
            w_out_b = diff_w_out_b
            outs["dkp"].append(kf[:n_p].reshape(bp, sp, DIFF_HEADS, 2 * HEAD_GROUP))
            outs["dvp"].append(vf[:n_p].reshape(bp, sp, DIFF_HEADS, 2 * HEAD_GROUP))
            outs["dks"].append(kf[n_p:].reshape(bs, ss, DIFF_HEADS, 2 * HEAD_GROUP))
            outs["dvs"].append(vf[n_p:].reshape(bs, ss, DIFF_HEADS, 2 * HEAD_GROUP))
        else:
            bf = jnp.pad(fox_b_f[j], (0, LANES - FOX_HEADS)).reshape(1, LANES)
            qb, kf, kb, vf, vb, sg, lf = _inproj_fox(x, gm, fox_w_main_b, fox_w_f_b, j, bf,
                                                     _tile_gain(fox_q_norm[j], d),
                                                     _tile_gain(fox_k_norm[j], d), summ, expand)
            logf = lf[:, :FOX_HEADS]
            logf_p = logf[:n_p].reshape(bp, sp, FOX_HEADS)
            logf_s = logf[n_p:].reshape(bs, ss, FOX_HEADS)
            tk_c = SAMPLE_TK if past % SAMPLE_TK == 0 else past
            c_new = _coff(logf_s.transpose(0, 2, 1), ss, False)
            c_new = c_new.reshape(bs, FOX_HEADS // 2, 2, ss)
            c_past = _pair_heads(_coff(cache_fox_logf[j].astype(F32).transpose(0, 2, 1), tk_c, True))
            o_p = _fox_prompt(qb, kb, vb, lf, sg, bp, sp)
            o_s = _fox_sample(qb, kb, vb, cache_fk, cache_fv, j, c_past, c_new, sg, n_p, bs, ss)
            w_out_b = fox_w_out_b
            outs["fkp"].append(kf[:n_p].reshape(bp, sp, FOX_HEADS, HEAD_GROUP))
            outs["fvp"].append(vf[:n_p].reshape(bp, sp, FOX_HEADS, HEAD_GROUP))
            outs["flp"].append(logf_p)
            outs["fks"].append(kf[n_p:].reshape(bs, ss, FOX_HEADS, HEAD_GROUP))
            outs["fvs"].append(vf[n_p:].reshape(bs, ss, FOX_HEADS, HEAD_GROUP))
            outs["fls"].append(logf_s)
        o = jnp.concatenate([o_p, o_s], axis=0)
        gf = norm_ffn[i].reshape(1, d)
        x, meta = _outproj(o, x, w_out_b, j, gf, wr_hi[i], wr_lo[i], b_router[i].reshape(1, LANES))
        bucket = meta[:, 0].astype(jnp.int32)
        perm, sched = _moe_schedule(bucket, n)
        xs = x[perm]
        ms = meta[:, :8][perm]
        ys = _moe(sched, xs, ms, gf, wgu_b, wd_b, i)
        inv = jnp.zeros((n,), jnp.int32).at[perm].set(jnp.arange(n, dtype=jnp.int32))
        x = ys[inv]

    y_prompt = x[:n_p].reshape(bp, sp, d)
    y_sample = x[n_p:].reshape(bs, ss, d)
    stk = lambda name: jnp.stack(outs[name])
    return (y_prompt, y_sample, stk("dkp"), stk("dvp"), stk("fkp"), stk("fvp"), stk("flp"),
            stk("dks"), stk("dvs"), stk("fks"), stk("fvs"), stk("fls"))
```

```python
import functools
import math

import numpy as np
import jax
import jax.numpy as jnp
from jax import lax
from jax.experimental import pallas as pl
from jax.experimental.pallas import tpu as pltpu

F32 = jnp.float32
BF16 = jnp.bfloat16

CHUNK_SHIFT = 6
DIFF_HEADS = 8
FOX_HEADS = 16
HEAD_GROUP = 64
N_GROUPS = 4
EXPERTS_PER_GROUP = 4
N_EXPERTS = N_GROUPS * EXPERTS_PER_GROUP
N_PAIRS = 6
N_BUCKETS = N_GROUPS * N_PAIRS
EPS = 1e-6
NEG = -1e30
LOG2E = math.log2(math.e)
QSCALE = (HEAD_GROUP ** -0.5) * LOG2E

LANES = 128
TOK_TILE = 512
ATT_TQ = 512
ATT_TK = 512
SAMPLE_TK = 1024
MOE_TILE = 256
VMEM_LIMIT = 56 * 1024 * 1024

PAIR_A = np.array([0, 0, 0, 1, 1, 2], np.int32)
PAIR_B = np.array([1, 2, 3, 2, 3, 3], np.int32)


def _cparams(n_axes):
    return pltpu.CompilerParams(dimension_semantics=("arbitrary",) * n_axes,
                                vmem_limit_bytes=VMEM_LIMIT)


def _rms_rows(x, gain):
    ms = jnp.mean(x * x, axis=-1, keepdims=True)
    return x * lax.rsqrt(ms + EPS) * gain


def _group_norm(z, gain, sum_ref, exp_ref):
    ms = jnp.dot((z * z).astype(BF16), sum_ref[...], preferred_element_type=F32)
    r = lax.rsqrt(ms + EPS)
    r_hi = r.astype(BF16)
    r_lo = (r - r_hi.astype(F32)).astype(BF16)
    rb = jnp.dot(jnp.concatenate([r_hi, r_lo], axis=-1), exp_ref[...], preferred_element_type=F32)
    return z * rb * gain


def _group_mats(d):
    n_g = d // HEAD_GROUP
    col = np.arange(d) // HEAD_GROUP
    summ = np.zeros((d, LANES), np.float32)
    summ[np.arange(d), col] = 1.0 / HEAD_GROUP
    expand = np.zeros((2 * LANES, d), np.float32)
    expand[col, np.arange(d)] = 1.0
    expand[LANES + col, np.arange(d)] = 1.0
    assert n_g <= LANES
    return jnp.asarray(summ, BF16), jnp.asarray(expand, BF16)


def _put_heads(ref, val):
    rows = val.shape[0]
    n_heads = ref.shape[0] // rows
    for h in range(n_heads):
        ref[pl.ds(h, rows, stride=n_heads), :] = val[:, h * LANES:(h + 1) * LANES]


def _inproj_diff_kernel(tiles_p, x_ref, gm_ref, w_ref, qg_ref, kg_ref, sum_ref, exp_ref,
                        qb_ref, kb_ref, vb_ref, kp_ref, vp_ref, ks_ref, vs_ref):
    d = x_ref.shape[1]
    i = pl.program_id(0)
    h = _rms_rows(x_ref[...], gm_ref[...]).astype(BF16)
    q = jnp.dot(h, w_ref[:, 0:d], preferred_element_type=F32)
    qb_ref[...] = (_group_norm(q, qg_ref[...], sum_ref, exp_ref) * QSCALE).astype(BF16)
    k = jnp.dot(h, w_ref[:, d:2 * d], preferred_element_type=F32)
    kn = _group_norm(k, kg_ref[...], sum_ref, exp_ref)
    kb_ref[...] = kn.astype(BF16)
    v = jnp.dot(h, w_ref[:, 2 * d:3 * d], preferred_element_type=F32)
    vb_ref[...] = v.astype(BF16)

    @pl.when(i < tiles_p)
    def _():
        _put_heads(kp_ref, kn)
        _put_heads(vp_ref, v)

    @pl.when(i >= tiles_p)
    def _():
        _put_heads(ks_ref, kn)
        _put_heads(vs_ref, v)


def _inproj_diff(x, gm, w_all, layer, qg, kg, summ, expand, n_p):
    n, d = x.shape
    tm = TOK_TILE
    tiles_p = n_p // tm
    row = lambda i: (i, 0)
    fixed = lambda i: (0, 0)
    big = pl.BlockSpec((tm, d), row)
    heads = pl.BlockSpec((tm * DIFF_HEADS, LANES), lambda i: (jnp.minimum(i, tiles_p - 1), 0))
    heads_s = pl.BlockSpec((tm * DIFF_HEADS, LANES), lambda i: (jnp.maximum(i - tiles_p, 0), 0))
    cache_p = jax.ShapeDtypeStruct((n_p * DIFF_HEADS, LANES), F32)
    cache_s = jax.ShapeDtypeStruct(((n - n_p) * DIFF_HEADS, LANES), F32)
    return pl.pallas_call(
        functools.partial(_inproj_diff_kernel, tiles_p),
        grid=(n // tm,),
        in_specs=[big,
                  pl.BlockSpec((1, d), fixed),
                  pl.BlockSpec((None, d, 3 * d), lambda i: (layer, 0, 0)),
                  pl.BlockSpec((1, d), fixed), pl.BlockSpec((1, d), fixed),
                  pl.BlockSpec(summ.shape, fixed), pl.BlockSpec(expand.shape, fixed)],
        out_specs=[big, big, big, heads, heads, heads_s, heads_s],
        out_shape=[jax.ShapeDtypeStruct((n, d), BF16), jax.ShapeDtypeStruct((n, d), BF16),
                   jax.ShapeDtypeStruct((n, d), BF16), cache_p, cache_p, cache_s, cache_s],
        compiler_params=_cparams(1),
        name="inproj_diff",
    )(x, gm, w_all, qg, kg, summ, expand)


def _inproj_fox_kernel(tiles_p, x_ref, gm_ref, w_ref, wf_ref, bf_ref, qg_ref, kg_ref, sum_ref, exp_ref,
                       qb_ref, kb_ref, vb_ref, sg_ref, lf_ref,
                       kp_ref, vp_ref, lp_ref, ks_ref, vs_ref, ls_ref):
    d = x_ref.shape[1]
    i = pl.program_id(0)
    h = _rms_rows(x_ref[...], gm_ref[...]).astype(BF16)
    q = jnp.dot(h, w_ref[:, 0:d], preferred_element_type=F32)
    qb_ref[...] = (_group_norm(q, qg_ref[...], sum_ref, exp_ref) * QSCALE).astype(BF16)
    k = jnp.dot(h, w_ref[:, d:2 * d], preferred_element_type=F32)
    kn = _group_norm(k, kg_ref[...], sum_ref, exp_ref)
    kb_ref[...] = kn.astype(BF16)
    v = jnp.dot(h, w_ref[:, 2 * d:3 * d], preferred_element_type=F32)
    vb_ref[...] = v.astype(BF16)
    g = jnp.dot(h, w_ref[:, 3 * d:4 * d], preferred_element_type=F32)
    sg_ref[...] = (1.0 / (1.0 + jnp.exp(-g))).astype(BF16)
    f = jnp.dot(h, wf_ref[...], preferred_element_type=F32) + bf_ref[...]
    lf = jnp.minimum(f, 0.0) - jnp.log(1.0 + jnp.exp(-jnp.abs(f)))
    lf_ref[...] = lf
    n_heads = lp_ref.shape[0]

    @pl.when(i < tiles_p)
    def _():
        kp_ref[...] = kn.T
        vp_ref[...] = v.T
        lp_ref[...] = lf.T[:n_heads, :]

    @pl.when(i >= tiles_p)
    def _():
        seq = ks_ref.shape[2]
        for bb in range(ks_ref.shape[0]):
            rows = slice(bb * seq, (bb + 1) * seq)
            ks_ref[bb] = kn[rows, :].T
            vs_ref[bb] = v[rows, :].T
            ls_ref[bb] = lf[rows, :].T[:n_heads, :]


def _inproj_fox(x, gm, w_all, wf_all, layer, bf, qg, kg, summ, expand, n_p, seq_p, seq_s):
    n, d = x.shape
    tm = TOK_TILE
    tiles_p = n_p // tm
    per_b = seq_p // tm
    grp = tm // seq_s
    row = lambda i: (i, 0)
    fixed = lambda i: (0, 0)
    big = pl.BlockSpec((tm, d), row)

    def p_map(i):
        ip = jnp.minimum(i, tiles_p - 1)
        return (ip // per_b, 0, ip % per_b)

    s_map = lambda i: (jnp.maximum(i - tiles_p, 0), 0, 0)
    f32 = lambda *shape: jax.ShapeDtypeStruct(shape, F32)
    bf16 = jax.ShapeDtypeStruct((n, d), BF16)
    return pl.pallas_call(
        functools.partial(_inproj_fox_kernel, tiles_p),
        grid=(n // tm,),
        in_specs=[big,
                  pl.BlockSpec((1, d), fixed),
                  pl.BlockSpec((None, d, 4 * d), lambda i: (layer, 0, 0)),
                  pl.BlockSpec((None, d, LANES), lambda i: (layer, 0, 0)),
                  pl.BlockSpec((1, LANES), fixed),
                  pl.BlockSpec((1, d), fixed), pl.BlockSpec((1, d), fixed),
                  pl.BlockSpec(summ.shape, fixed), pl.BlockSpec(expand.shape, fixed)],
        out_specs=[big, big, big, big, pl.BlockSpec((tm, LANES), row),
                   pl.BlockSpec((None, d, tm), p_map), pl.BlockSpec((None, d, tm), p_map),
                   pl.BlockSpec((None, FOX_HEADS, tm), p_map),
                   pl.BlockSpec((grp, d, seq_s), s_map), pl.BlockSpec((grp, d, seq_s), s_map),
                   pl.BlockSpec((grp, FOX_HEADS, seq_s), s_map)],
        out_shape=[bf16, bf16, bf16, bf16, f32(n, LANES),
                   f32(n_p // seq_p, d, seq_p), f32(n_p // seq_p, d, seq_p),
                   f32(n_p // seq_p, FOX_HEADS, seq_p),
                   f32((n - n_p) // seq_s, d, seq_s), f32((n - n_p) // seq_s, d, seq_s),
                   f32((n - n_p) // seq_s, FOX_HEADS, seq_s)],
        compiler_params=_cparams(1),
        name="inproj_fox",
    )(x, gm, w_all, wf_all, bf, qg, kg, summ, expand)


def _half_masks(rows):
    lane = lax.broadcasted_iota(jnp.int32, (rows, LANES), 1)
    return lane < HEAD_GROUP, lane >= HEAD_GROUP


NT_DIMS = (((1,), (1,)), ((), ()))


def _row_update(idx, s, pv, m_ref, l_ref, acc_ref):
    m_prev = m_ref[idx]
    m_new = jnp.maximum(m_prev, jnp.max(s, axis=-1, keepdims=True))
    alpha = jnp.exp2(m_prev - m_new)
    p = jnp.exp2(s - m_new)
    l_ref[idx] = alpha * l_ref[idx] + jnp.sum(p, axis=-1, keepdims=True)
    acc_ref[idx] = alpha * acc_ref[idx] + pv(p.astype(BF16))
    m_ref[idx] = m_new


def _init_state(m_ref, l_ref, acc_ref):
    m_ref[...] = jnp.full(m_ref.shape, NEG, F32)
    l_ref[...] = jnp.zeros(l_ref.shape, F32)
    acc_ref[...] = jnp.zeros(acc_ref.shape, F32)


def _diff_lambda(lq1, lk1, lq2, lk2, lambda_init):
    a = jnp.exp(jnp.sum(lq1[...] * lk1[...], axis=-1, keepdims=True))
    b = jnp.exp(jnp.sum(lq2[...] * lk2[...], axis=-1, keepdims=True))
    return a - b + lambda_init


def _alibi_table(slope2, rows, cols, q0, k0):
    i = lax.broadcasted_iota(jnp.int32, (rows, cols), 0)
    j = lax.broadcasted_iota(jnp.int32, (rows, cols), 1)
    qpos = i + q0
    kpos = j + k0
    allowed = jnp.right_shift(kpos, CHUNK_SHIFT) <= jnp.right_shift(qpos, CHUNK_SHIFT)
    bias = slope2 * (i - jnp.abs(qpos - kpos)).astype(F32)
    return jnp.where(allowed, bias, NEG)


BIAS_TERMS = 3


def _bias_lanes(c):
    return HEAD_GROUP * (1 - c)


def _own_lanes(lane, c):
    return (lane >= HEAD_GROUP * c) & (lane < HEAD_GROUP * (c + 1))


def _augment_keys(k_blk, bias, c):
    lane = lax.broadcasted_iota(jnp.int32, k_blk.shape, 1)
    b0 = _bias_lanes(c)
    out = jnp.where(_own_lanes(lane, c), k_blk, jnp.zeros_like(k_blk))
    rest = bias
    for t in range(BIAS_TERMS):
        term = rest.astype(BF16)
        out = jnp.where(lane == b0 + t, term, out)
        rest = rest - term.astype(F32)
    return out


def _augment_queries(q, c):
    lane = lax.broadcasted_iota(jnp.int32, q.shape, 1)
    b0 = _bias_lanes(c)
    ones = jnp.where((lane >= b0) & (lane < b0 + BIAS_TERMS), 1.0, 0.0).astype(q.dtype)
    return jnp.where(_own_lanes(lane, c), q, ones)


def _transpose_values(v_ref, vt_ref):
    nblk, _, tk = vt_ref.shape
    for blk in range(nblk):
        vt_ref[blk] = v_ref[blk * tk:(blk + 1) * tk, :].astype(F32).T.astype(vt_ref.dtype)


def _block_t(c, ka, qa, vt, tbl, q_lo, m_ref, l_ref, acc_ref):
    dims = (((1,), (1,)), ((), ()))
    st = lax.dot_general(ka, qa[q_lo:, :], dims, preferred_element_type=F32)
    if tbl is not None:
        st = st + tbl
    m_prev = m_ref[c, :, q_lo:]
    m_new = jnp.maximum(m_prev, jnp.max(st, axis=0, keepdims=True))
    alpha = jnp.exp2(m_prev - m_new)
    pt = jnp.exp2(st - m_new)
    l_ref[c, :, q_lo:] = alpha * l_ref[c, :, q_lo:] + jnp.sum(pt, axis=0, keepdims=True)
    acc_ref[c, :, q_lo:] = alpha * acc_ref[c, :, q_lo:] + jnp.dot(vt, pt.astype(BF16),
                                                                  preferred_element_type=F32)
    m_ref[c, :, q_lo:] = m_new


def _sweep_t(qi, qa, kaug_ref, vt_ref, tbl_ref, m_ref, l_ref, acc_ref, v_rows):
    tk = vt_ref.shape[2]
    tq = qa[0].shape[0]
    per_q = tq // tk

    def body(ki, carry):
        k0 = pl.multiple_of(ki * tk, tk)
        vt = vt_ref[ki]
        for c in range(2):
            _block_t(c, kaug_ref[c, pl.ds(k0, tk), :], qa[c], vt[v_rows[c], :], None, 0,
                     m_ref, l_ref, acc_ref)
        return carry

    lax.fori_loop(0, qi * per_q, body, 0)
    for dblk in range(per_q):
        ki = qi * per_q + dblk
        k0 = pl.multiple_of(ki * tk, tk)
        vt = vt_ref[ki]
        q_lo = dblk * tk
        for c in range(2):
            _block_t(c, kaug_ref[c, pl.ds(k0, tk), :], qa[c], vt[v_rows[c], :],
                     tbl_ref[dblk, :, q_lo:], q_lo, m_ref, l_ref, acc_ref)


def _diff_prompt_kernel(lambda_init, sl_ref, lq1, lk1, lq2, lk2, q_ref, k_ref, v_ref, g_ref,
                        o_ref, kaug_ref, vt_ref, m_ref, l_ref, acc_ref, tbl_ref):
    tq = q_ref.shape[0]
    seq = k_ref.shape[0]
    nblk, _, tk = vt_ref.shape
    qi = pl.program_id(2)
    slope2 = sl_ref[pl.program_id(1)]

    @pl.when(qi == 0)
    def _():
        for blk in range(nblk):
            rows = slice(blk * tk, (blk + 1) * tk)
            kpos = lax.broadcasted_iota(jnp.int32, (tk, LANES), 0) + blk * tk
            bias = slope2 * kpos.astype(F32)
            for c in range(2):
                kaug_ref[c, rows, :] = _augment_keys(k_ref[rows, :], bias, c)
        _transpose_values(v_ref, vt_ref)
        for dblk in range(tq // tk):
            dj = lax.broadcasted_iota(jnp.int32, (tk, tq), 0) + dblk * tk
            i = lax.broadcasted_iota(jnp.int32, (tk, tq), 1)
            allowed = jnp.right_shift(dj, CHUNK_SHIFT) <= jnp.right_shift(i, CHUNK_SHIFT)
            ahead = jnp.maximum(dj - i, 0).astype(F32)
            tbl_ref[dblk] = jnp.where(allowed, (-2.0 * slope2) * ahead, NEG)

    _init_state(m_ref, l_ref, acc_ref)
    q = q_ref[...]
    qa = [_augment_queries(q, c) for c in range(2)]
    whole = slice(None)
    _sweep_t(qi, qa, kaug_ref, vt_ref, tbl_ref, m_ref, l_ref, acc_ref, (whole, whole))
    lam = _diff_lambda(lq1, lk1, lq2, lk2, lambda_init)
    ot = acc_ref[0] / l_ref[0] - lam * (acc_ref[1] / l_ref[1])
    ms = jnp.mean(ot * ot, axis=0, keepdims=True)
    o = (ot * lax.rsqrt(ms + EPS)).T
    o_ref[...] = ((o * g_ref[...]) * (1.0 - lambda_init)).astype(o_ref.dtype)


def _diff_prompt(qb, kb, vb, slopes2, lam_params, subln, n_batch, seq, lambda_init):
    d = qb.shape[1]
    tq = min(ATT_TQ, seq)
    tk = min(ATT_TK, tq)
    nq = seq // tq
    hd2 = 2 * HEAD_GROUP
    small = pl.BlockSpec((1, HEAD_GROUP), lambda b, h, i: (0, 0))
    return pl.pallas_call(
        functools.partial(_diff_prompt_kernel, lambda_init),
        grid=(n_batch, DIFF_HEADS, nq),
        in_specs=[pl.BlockSpec(memory_space=pltpu.SMEM), small, small, small, small,
                  pl.BlockSpec((tq, hd2), lambda b, h, i: (b * nq + i, h)),
                  pl.BlockSpec((seq, hd2), lambda b, h, i: (b, h)),
                  pl.BlockSpec((seq, hd2), lambda b, h, i: (b, h)),
                  pl.BlockSpec((1, hd2), lambda b, h, i: (0, 0))],
        out_specs=pl.BlockSpec((tq, hd2), lambda b, h, i: (b * nq + i, h)),
        out_shape=jax.ShapeDtypeStruct((n_batch * seq, d), BF16),
        scratch_shapes=[pltpu.VMEM((2, seq, hd2), BF16), pltpu.VMEM((seq // tk, hd2, tk), BF16),
                        pltpu.VMEM((2, 1, tq), F32), pltpu.VMEM((2, 1, tq), F32),
                        pltpu.VMEM((2, hd2, tq), F32), pltpu.VMEM((tq // tk, tk, tq), F32)],
        compiler_params=_cparams(3),
        name="diff_attn_prompt",
    )(slopes2, *lam_params, qb, kb, vb, subln)


def _diff_sample_kernel(lambda_init, past, sl_ref, lq1, lk1, lq2, lk2, q_ref, kp_ref, vp_ref,
                        kn_ref, vn_ref, g_ref, o_ref, m_ref, l_ref, acc_ref):
    tq = q_ref.shape[0]
    n_heads = m_ref.shape[0]
    hd2 = kp_ref.shape[1]
    tk = kp_ref.shape[0] // n_heads
    step = pl.program_id(1)

    @pl.when(step == 0)
    def _():
        _init_state(m_ref, l_ref, acc_ref)

    lo, hi = _half_masks(tq)
    col = lax.broadcasted_iota(jnp.int32, (1, tk), 1) + (step * tk - past)

    def q_streams(h):
        q = q_ref[:, h * hd2:(h + 1) * hd2]
        zero = jnp.zeros_like(q)
        return jnp.where(lo, q, zero), jnp.where(hi, q, zero)

    for h in range(n_heads):
        k_h = kp_ref[pl.ds(h, tk, stride=n_heads), :].astype(BF16)
        v_h = vp_ref[pl.ds(h, tk, stride=n_heads), :].astype(BF16)
        bias = sl_ref[h] * col.astype(F32)
        for c, q_c in enumerate(q_streams(h)):
            s = lax.dot_general(q_c, k_h, NT_DIMS, preferred_element_type=F32) + bias
            _row_update((h, c), s, lambda p: jnp.dot(p, v_h, preferred_element_type=F32),
                        m_ref, l_ref, acc_ref)

    @pl.when(step == pl.num_programs(1) - 1)
    def _():
        lam = _diff_lambda(lq1, lk1, lq2, lk2, lambda_init)
        for h in range(n_heads):
            cols = slice(h * hd2, (h + 1) * hd2)
            k_h = kn_ref[:, cols]
            v_h = vn_ref[:, cols]
            bias = _alibi_table(sl_ref[h], tq, tq, past, past)
            for c, q_c in enumerate(q_streams(h)):
                s = lax.dot_general(q_c, k_h, NT_DIMS, preferred_element_type=F32) + bias
                _row_update((h, c), s, lambda p: jnp.dot(p, v_h, preferred_element_type=F32),
                            m_ref, l_ref, acc_ref)
            o = acc_ref[h, 0] / l_ref[h, 0] - lam * (acc_ref[h, 1] / l_ref[h, 1])
            ms = jnp.mean(o * o, axis=-1, keepdims=True)
            o_ref[:, cols] = ((o * lax.rsqrt(ms + EPS) * g_ref[...]) * (1.0 - lambda_init)).astype(o_ref.dtype)


def _diff_sample(qb, kb, vb, cache_k, cache_v, layer, slopes2, lam_params, subln, row0, n_batch, seq,
                 lambda_init):
    d = qb.shape[1]
    past, n_heads, hd2 = cache_k.shape[2:]
    tk = SAMPLE_TK if past % SAMPLE_TK == 0 else past
    blk0 = row0 // seq
    rows = lambda a: a.reshape(a.shape[:2] + (past * n_heads, hd2))
    cache_k, cache_v = rows(cache_k), rows(cache_v)
    small = pl.BlockSpec((1, HEAD_GROUP), lambda b, c: (0, 0))
    new = pl.BlockSpec((seq, d), lambda b, c: (blk0 + b, 0))
    old = pl.BlockSpec((None, None, tk * n_heads, hd2), lambda b, c: (layer, b, c, 0))
    return pl.pallas_call(
        functools.partial(_diff_sample_kernel, lambda_init, past),
        grid=(n_batch, past // tk),
        in_specs=[pl.BlockSpec(memory_space=pltpu.SMEM), small, small, small, small,
                  new, old, old, new, new, pl.BlockSpec((1, hd2), lambda b, c: (0, 0))],
        out_specs=pl.BlockSpec((seq, d), lambda b, c: (b, 0)),
        out_shape=jax.ShapeDtypeStruct((n_batch * seq, d), BF16),
        scratch_shapes=[pltpu.VMEM((n_heads, 2, seq, 1), F32), pltpu.VMEM((n_heads, 2, seq, 1), F32),
                        pltpu.VMEM((n_heads, 2, seq, hd2), F32)],
        compiler_params=_cparams(2),
        name="diff_attn_sample",
    )(slopes2, *lam_params, qb, cache_k, cache_v, kb, vb, subln)


def _causal_table(rows, cols):
    i = lax.broadcasted_iota(jnp.int32, (rows, cols), 0)
    j = lax.broadcasted_iota(jnp.int32, (rows, cols), 1)
    return jnp.where(j <= i, 0.0, NEG).astype(F32)


def _fox_prompt_kernel(q_ref, k_ref, v_ref, lf_ref, sg_ref, o_ref,
                       cum_ref, kaug_ref, vt_ref, m_ref, l_ref, acc_ref, tbl_ref):
    tq = q_ref.shape[0]
    nblk, _, tk = vt_ref.shape
    pair = pl.program_id(1)
    qi = pl.program_id(2)

    @pl.when((pair == 0) & (qi == 0))
    def _():
        r = lax.broadcasted_iota(jnp.int32, (tk, tk), 0)
        c = lax.broadcasted_iota(jnp.int32, (tk, tk), 1)
        tril = jnp.where(c <= r, 1.0, 0.0).astype(BF16)
        carry = jnp.zeros((1, LANES), F32)
        for blk in range(nblk):
            x = lf_ref[blk * tk:(blk + 1) * tk, :]
            hi = x.astype(BF16)
            lo = (x - hi.astype(F32)).astype(BF16)
            part = (jnp.dot(tril, hi, preferred_element_type=F32)
                    + jnp.dot(tril, lo, preferred_element_type=F32)) + carry
            cum_ref[blk * tk:(blk + 1) * tk, :] = part * (-LOG2E)
            carry = carry + jnp.sum(x, axis=0, keepdims=True)

    @pl.when(qi == 0)
    def _():
        lane = lax.broadcasted_iota(jnp.int32, (tk, LANES), 1)
        for blk in range(nblk):
            rows = slice(blk * tk, (blk + 1) * tk)
            cum = cum_ref[rows, :]
            for c in range(2):
                col = jnp.sum(jnp.where(lane == 2 * pair + c, cum, 0.0), axis=-1, keepdims=True)
                bias = jnp.broadcast_to(col, (tk, LANES))
                kaug_ref[c, rows, :] = _augment_keys(k_ref[rows, :], bias, c)
        _transpose_values(v_ref, vt_ref)
        for dblk in range(tq // tk):
            dj = lax.broadcasted_iota(jnp.int32, (tk, tq), 0) + dblk * tk
            i = lax.broadcasted_iota(jnp.int32, (tk, tq), 1)
            tbl_ref[dblk] = jnp.where(dj <= i, 0.0, NEG).astype(F32)

    _init_state(m_ref, l_ref, acc_ref)
    q = q_ref[...]
    qa = [_augment_queries(q, c) for c in range(2)]
    halves = (slice(0, HEAD_GROUP), slice(HEAD_GROUP, 2 * HEAD_GROUP))
    _sweep_t(qi, qa, kaug_ref, vt_ref, tbl_ref, m_ref, l_ref, acc_ref, halves)
    ot = jnp.concatenate([acc_ref[0] / l_ref[0], acc_ref[1] / l_ref[1]], axis=0)
    o_ref[...] = (ot.T * sg_ref[...].astype(F32)).astype(o_ref.dtype)


def _fox_prompt(qb, kb, vb, lf, sg, n_batch, seq):
    d = qb.shape[1]
    tq = min(ATT_TQ, seq)
    tk = min(ATT_TK, tq)
    nq = seq // tq
    w = LANES
    qspec = pl.BlockSpec((tq, w), lambda b, p, i: (b * nq + i, p))
    kspec = pl.BlockSpec((seq, w), lambda b, p, i: (b, p))
    return pl.pallas_call(
        _fox_prompt_kernel,
        grid=(n_batch, d // w, nq),
        in_specs=[qspec, kspec, kspec,
                  pl.BlockSpec((seq, LANES), lambda b, p, i: (b, 0)),
                  qspec],
        out_specs=qspec,
        out_shape=jax.ShapeDtypeStruct((n_batch * seq, d), BF16),
        scratch_shapes=[pltpu.VMEM((seq, LANES), F32),
                        pltpu.VMEM((2, seq, w), BF16), pltpu.VMEM((seq // tk, w, tk), BF16),
                        pltpu.VMEM((2, 1, tq), F32), pltpu.VMEM((2, 1, tq), F32),
                        pltpu.VMEM((2, HEAD_GROUP, tq), F32), pltpu.VMEM((tq // tk, tk, tq), F32)],
        compiler_params=_cparams(3),
        name="fox_attn_prompt",
    )(qb, kb, vb, lf, sg)


def _fox_sample_kernel(q_ref, kp_ref, vp_ref, kn_ref, vn_ref, cp_ref, cn_ref, sg_ref,
                       o_ref, m_ref, l_ref, acc_ref):
    tq = q_ref.shape[0]
    n_heads, hd, tk = kp_ref.shape
    step = pl.program_id(1)

    @pl.when(step == 0)
    def _():
        _init_state(m_ref, l_ref, acc_ref)

    for h in range(n_heads):
        q_h = q_ref[:, h * hd:(h + 1) * hd]
        kt_h = kp_ref[h].astype(BF16)
        vt_h = vp_ref[h].astype(BF16)
        s = jnp.dot(q_h, kt_h, preferred_element_type=F32) + cp_ref[h:h + 1, :]
        _row_update(h, s, lambda p: lax.dot_general(p, vt_h, NT_DIMS, preferred_element_type=F32),
                    m_ref, l_ref, acc_ref)

    @pl.when(step == pl.num_programs(1) - 1)
    def _():
        tbl = _causal_table(tq, tq)
        for h in range(n_heads):
            cols = slice(h * hd, (h + 1) * hd)
            q_h = q_ref[:, cols]
            k_h = kn_ref[:, cols]
            v_h = vn_ref[:, cols]
            s = lax.dot_general(q_h, k_h, NT_DIMS, preferred_element_type=F32) + (tbl + cn_ref[h:h + 1, :])
            _row_update(h, s, lambda p: jnp.dot(p, v_h, preferred_element_type=F32),
                        m_ref, l_ref, acc_ref)
            o = acc_ref[h] / l_ref[h]
            o_ref[:, cols] = (o * sg_ref[:, cols].astype(F32)).astype(o_ref.dtype)


def _fox_sample(qb, kb, vb, cache_kt, cache_vt, layer, c_past, c_new, sg, row0, n_batch, seq):
    d = qb.shape[1]
    n_heads, hd, past = cache_kt.shape[2:]
    tk = SAMPLE_TK if past % SAMPLE_TK == 0 else past
    blk0 = row0 // seq
    new = pl.BlockSpec((seq, d), lambda b, c: (blk0 + b, 0))
    old = pl.BlockSpec((None, None, n_heads, hd, tk), lambda b, c: (layer, b, 0, 0, c))
    return pl.pallas_call(
        _fox_sample_kernel,
        grid=(n_batch, past // tk),
        in_specs=[new, old, old, new, new,
                  pl.BlockSpec((None, n_heads, tk), lambda b, c: (b, 0, c)),
                  pl.BlockSpec((None, n_heads, seq), lambda b, c: (b, 0, 0)),
                  new],
        out_specs=pl.BlockSpec((seq, d), lambda b, c: (b, 0)),
        out_shape=jax.ShapeDtypeStruct((n_batch * seq, d), BF16),
        scratch_shapes=[pltpu.VMEM((n_heads, seq, 1), F32), pltpu.VMEM((n_heads, seq, 1), F32),
                        pltpu.VMEM((n_heads, seq, hd), F32)],
        compiler_params=_cparams(2),
        name="fox_attn_sample",
    )(qb, cache_kt, cache_vt, kb, vb, c_past, c_new, sg)


def _split_dot(x, mat):
    hi = x.astype(BF16)
    lo = (x - hi.astype(F32)).astype(BF16)
    return (jnp.dot(hi, mat, preferred_element_type=F32) + jnp.dot(lo, mat, preferred_element_type=F32))


def _coff_kernel(suffix, tk, x_ref, o_ref):
    nblk = x_ref.shape[1] // tk
    r = lax.broadcasted_iota(jnp.int32, (tk, tk), 0)
    c = lax.broadcasted_iota(jnp.int32, (tk, tk), 1)
    mat = jnp.where((r > c) if suffix else (r <= c), 1.0, 0.0).astype(BF16)
    carry = jnp.zeros((x_ref.shape[0], 1), F32)
    order = range(nblk - 1, -1, -1) if suffix else range(nblk)
    for blk in order:
        cols = slice(blk * tk, (blk + 1) * tk)
        x = x_ref[:, cols]
        part = _split_dot(x, mat) + carry
        o_ref[:, cols] = (part if suffix else -part) * LOG2E
        carry = carry + jnp.sum(x, axis=-1, keepdims=True)


def _coff(xt, tk, suffix):
    n_batch, n_head, length = xt.shape
    spec = pl.BlockSpec((None, n_head, length), lambda b: (b, 0, 0))
    return pl.pallas_call(
        functools.partial(_coff_kernel, suffix, tk),
        grid=(n_batch,),
        in_specs=[spec],
        out_specs=spec,
        out_shape=jax.ShapeDtypeStruct((n_batch, n_head, length), F32),
        compiler_params=_cparams(1),
        name="fox_coff_suffix" if suffix else "fox_coff_prefix",
    )(xt)


def _outproj_kernel(tiles_p, op_ref, os_ref, x_ref, w_ref, gf_ref, wr_hi_ref, wr_lo_ref, br_ref,
                    xo_ref, meta_ref):
    o = jnp.where(pl.program_id(0) < tiles_p, op_ref[...], os_ref[...])
    xn = x_ref[...] + jnp.dot(o, w_ref[...], preferred_element_type=F32)
    xo_ref[...] = xn
    t = _rms_rows(xn, gf_ref[...])
    t_hi = t.astype(BF16)
    t_lo = (t - t_hi.astype(F32)).astype(BF16)
    logits = (jnp.dot(t_hi, wr_hi_ref[...], preferred_element_type=F32)
              + jnp.dot(t_lo, wr_hi_ref[...], preferred_element_type=F32)
              + jnp.dot(t_hi, wr_lo_ref[...], preferred_element_type=F32)) + br_ref[...]
    rows = logits.shape[0]
    lane = lax.broadcasted_iota(jnp.int32, (rows, LANES), 1).astype(F32)
    big = float(LANES)

    def first_argmax(vals, vmax):
        return jnp.min(jnp.where(vals == vmax, lane, big), axis=-1, keepdims=True)

    gl = jnp.where(lane < N_GROUPS, logits, NEG)
    gmax = jnp.max(gl, axis=-1, keepdims=True)
    gsum = jnp.sum(jnp.where(lane < N_GROUPS, jnp.exp(logits - gmax), 0.0), axis=-1, keepdims=True)
    p_top = 1.0 / gsum
    g_idx = first_argmax(gl, gmax)
    base = N_GROUPS + EXPERTS_PER_GROUP * g_idx
    el = jnp.where((lane >= base) & (lane < base + EXPERTS_PER_GROUP), logits, NEG)
    v1 = jnp.max(el, axis=-1, keepdims=True)
    i1 = first_argmax(el, v1)
    el2 = jnp.where(lane == i1, NEG, el)
    v2 = jnp.max(el2, axis=-1, keepdims=True)
    i2 = first_argmax(el2, v2)
    e2 = jnp.exp(v2 - v1)
    w1 = p_top / (1.0 + e2)
    w2 = p_top * e2 / (1.0 + e2)
    a1 = i1 - base
    a2 = i2 - base
    lo = jnp.minimum(a1, a2)
    hi = jnp.maximum(a1, a2)
    wa = jnp.where(a1 < a2, w1, w2)
    wb = jnp.where(a1 < a2, w2, w1)
    pair = jnp.where(lo == 0.0, hi - 1.0, jnp.where(lo == 1.0, hi + 1.0, 5.0))
    bucket = g_idx * N_PAIRS + pair
    meta_ref[...] = jnp.where(lane == 0.0, bucket, jnp.where(lane == 1.0, wa, jnp.where(lane == 2.0, wb, 0.0)))


def _outproj(o_p, o_s, x, w_all, layer, gf, wr_hi, wr_lo, br):
    n, d = x.shape
    tm = TOK_TILE
    tiles_p = o_p.shape[0] // tm
    row = lambda i: (i, 0)
    fixed = lambda i: (0, 0)
    big = pl.BlockSpec((tm, d), row)
    return pl.pallas_call(
        functools.partial(_outproj_kernel, tiles_p),
        grid=(n // tm,),
        in_specs=[pl.BlockSpec((tm, d), lambda i: (jnp.minimum(i, tiles_p - 1), 0)),
                  pl.BlockSpec((tm, d), lambda i: (jnp.maximum(i - tiles_p, 0), 0)),
                  big,
                  pl.BlockSpec((None, d, d), lambda i: (layer, 0, 0)),
                  pl.BlockSpec((1, d), fixed),
                  pl.BlockSpec((d, LANES), fixed), pl.BlockSpec((d, LANES), fixed),
                  pl.BlockSpec((1, LANES), fixed)],
        out_specs=[big, pl.BlockSpec((tm, LANES), row)],
        out_shape=[jax.ShapeDtypeStruct((n, d), F32), jax.ShapeDtypeStruct((n, LANES), F32)],
        compiler_params=_cparams(1),
        name="outproj_router",
    )(o_p, o_s, x, w_all, gf, wr_hi, wr_lo, br)


def _moe_kernel(tile_ref, ea_ref, eb_ref, lo_ref, hi_ref, first_ref,
                xs_ref, ms_ref, gf_ref, wgu_a, wd_a, wgu_b, wd_b, out_ref):
    s = pl.program_id(0)
    lo = lo_ref[s]
    hi = hi_ref[s]

    @pl.when(first_ref[s] == 1)
    def _():
        out_ref[...] = xs_ref[...]

    @pl.when(hi > lo)
    def _():
        t = _rms_rows(xs_ref[...], gf_ref[...]).astype(BF16)
        rows = t.shape[0]
        r = lax.broadcasted_iota(jnp.int32, (rows, 1), 0)
        inside = (r >= lo) & (r < hi)
        ms = ms_ref[...]
        wa = jnp.where(inside, ms[:, 1:2], 0.0)
        wb = jnp.where(inside, ms[:, 2:3], 0.0)

        def expert(wgu, wd):
            de = wd.shape[0]
            gu = jnp.dot(t, wgu[...], preferred_element_type=F32)
            g = gu[:, :de]
            hid = (g / (1.0 + jnp.exp(-g)) * gu[:, de:]).astype(BF16)
            return jnp.dot(hid, wd[...], preferred_element_type=F32)

        out_ref[...] += wa * expert(wgu_a, wd_a) + wb * expert(wgu_b, wd_b)


def _moe(sched, xs, ms, gf, wgu_all, wd_all, layer):
    n, d = xs.shape
    de = wd_all.shape[2]
    n_steps = sched[0].shape[0]
    tile_map = lambda s, tile, ea, eb, lo, hi, first: (tile[s], 0)
    fixed = lambda s, *_: (0, 0)
    wa_map = lambda s, tile, ea, eb, lo, hi, first: (layer, ea[s], 0, 0)
    wb_map = lambda s, tile, ea, eb, lo, hi, first: (layer, eb[s], 0, 0)
    grid_spec = pltpu.PrefetchScalarGridSpec(
        num_scalar_prefetch=6,
        grid=(n_steps,),
        in_specs=[pl.BlockSpec((MOE_TILE, d), tile_map),
                  pl.BlockSpec((MOE_TILE, ms.shape[1]), tile_map),
                  pl.BlockSpec((1, d), fixed),
                  pl.BlockSpec((None, None, d, 2 * de), wa_map),
                  pl.BlockSpec((None, None, de, d), wa_map),
                  pl.BlockSpec((None, None, d, 2 * de), wb_map),
                  pl.BlockSpec((None, None, de, d), wb_map)],
        out_specs=pl.BlockSpec((MOE_TILE, d), tile_map),
    )
    return pl.pallas_call(
        _moe_kernel,
        grid_spec=grid_spec,
        out_shape=jax.ShapeDtypeStruct((n, d), F32),
        compiler_params=_cparams(1),
        name="moe_experts",
    )(*sched, xs, ms, gf, wgu_all, wd_all, wgu_all, wd_all)


def _moe_schedule(bucket, n):
    n_tiles = n // MOE_TILE
    n_steps = n_tiles + N_BUCKETS - 1
    perm = jnp.argsort(bucket, stable=True).astype(jnp.int32)
    sorted_b = bucket[perm]
    member = (bucket[None, :] == jnp.arange(N_BUCKETS, dtype=jnp.int32)[:, None]).astype(jnp.int32)
    running = jnp.cumsum(member, axis=1)
    counts = running[:, -1]
    offs = jnp.concatenate([jnp.zeros((1,), jnp.int32), jnp.cumsum(counts)])
    inv = offs[bucket] + jnp.sum(running * member, axis=0) - 1
    fb = sorted_b[::MOE_TILE]
    lb = sorted_b[MOE_TILE - 1::MOE_TILE]
    per_tile = lb - fb + 1
    starts = jnp.cumsum(per_tile) - per_tile
    total = jnp.sum(per_tile)
    s = jnp.arange(n_steps, dtype=jnp.int32)
    tile = jnp.clip(jnp.searchsorted(starts, s, side="right").astype(jnp.int32) - 1, 0, n_tiles - 1)
    valid = s < total
    bkt = jnp.where(valid, fb[tile] + (s - starts[tile]), lb[n_tiles - 1])
    lo = jnp.clip(offs[bkt] - tile * MOE_TILE, 0, MOE_TILE)
    hi = jnp.clip(offs[bkt + 1] - tile * MOE_TILE, 0, MOE_TILE)
    lo = jnp.where(valid, lo, 0)
    hi = jnp.where(valid, hi, 0)
    first = (valid & (s == starts[tile])).astype(jnp.int32)
    grp = bkt // N_PAIRS
    ea = grp * EXPERTS_PER_GROUP + jnp.asarray(PAIR_A)[bkt % N_PAIRS]
    eb = grp * EXPERTS_PER_GROUP + jnp.asarray(PAIR_B)[bkt % N_PAIRS]
    return perm, inv, (tile, ea.astype(jnp.int32), eb.astype(jnp.int32), lo.astype(jnp.int32),
                       hi.astype(jnp.int32), first)


def _tile_gain(g, d):
    return jnp.tile(g.astype(F32), d // g.shape[0]).reshape(1, d)


def kernel(x_prompt, x_sample, cache_diff_k, cache_diff_v, cache_fox_k, cache_fox_v, cache_fox_logf, norm_mix, norm_ffn, diff_w_in, diff_w_out, diff_q_norm, diff_k_norm, diff_lambda_q1, diff_lambda_k1, diff_lambda_q2, diff_lambda_k2, diff_subln, fox_w_in, fox_b_f, fox_w_out, fox_q_norm, fox_k_norm, moe_w_group, moe_b_group, moe_w_expert, moe_b_expert, moe_w_gate, moe_w_up, moe_w_down):
    bp, sp, d = x_prompt.shape
    bs, ss, _ = x_sample.shape
    n_p = bp * sp
    n_s = bs * ss
    n = n_p + n_s
    depth = norm_mix.shape[0]
    past = cache_diff_k.shape[2]
    assert n % TOK_TILE == 0 and n % MOE_TILE == 0 and n_p % TOK_TILE == 0
    assert d % LANES == 0 and sp % min(ATT_TQ, sp) == 0 and n_p % ss == 0
    assert DIFF_HEADS * 2 * HEAD_GROUP == d and FOX_HEADS * HEAD_GROUP == d

    summ, expand = _group_mats(d)
    x = jnp.concatenate([x_prompt.reshape(n_p, d), x_sample.reshape(n_s, d)], axis=0)

    diff_w_in_b = diff_w_in.astype(BF16)
    diff_w_out_b = diff_w_out.astype(BF16)
    fox_w_main_b = fox_w_in[:, :, :4 * d].astype(BF16)
    fox_w_f_b = jnp.pad(fox_w_in[:, :, 4 * d:], ((0, 0), (0, 0), (0, LANES - FOX_HEADS))).astype(BF16)
    fox_w_out_b = fox_w_out.astype(BF16)
    wgu_b = jnp.concatenate([moe_w_gate, moe_w_up], axis=-1).astype(BF16)
    wd_b = moe_w_down.astype(BF16)
    w_router = jnp.pad(jnp.concatenate([moe_w_group, moe_w_expert], axis=-1),
                       ((0, 0), (0, 0), (0, LANES - N_GROUPS - N_EXPERTS)))
    wr_hi = w_router.astype(BF16)
    wr_lo = (w_router - wr_hi.astype(F32)).astype(BF16)
    b_router = jnp.pad(jnp.concatenate([moe_b_group, moe_b_expert], axis=-1),
                       ((0, 0), (0, LANES - N_GROUPS - N_EXPERTS)))
    slopes2 = jnp.asarray(2.0 ** (-8.0 * np.arange(1, DIFF_HEADS + 1) / DIFF_HEADS) * LOG2E, F32)
    cache_fkt = jnp.transpose(cache_fox_k, (0, 1, 3, 4, 2))
    cache_fvt = jnp.transpose(cache_fox_v, (0, 1, 3, 4, 2))
    cache_flt = jnp.transpose(cache_fox_logf.astype(F32), (0, 1, 3, 2))

    def from_feature_major(a, seq):
        return a.reshape(a.shape[0], FOX_HEADS, HEAD_GROUP, seq).transpose(0, 3, 1, 2)

    outs = {name: [] for name in ("dkp", "dvp", "dks", "dvs", "fkp", "fvp", "flp", "fks", "fvs", "fls")}
    for i in range(depth):
        j = i // 2
        gm = norm_mix[i].reshape(1, d)
        if i % 2 == 0:
            lambda_init = 0.8 - 0.6 * math.exp(-0.3 * i)
            qb, kb, vb, k_p, v_p, k_s, v_s = _inproj_diff(
                x, gm, diff_w_in_b, j, _tile_gain(diff_q_norm[j], d), _tile_gain(diff_k_norm[j], d),
                summ, expand, n_p)
            lam_params = [p[j].reshape(1, HEAD_GROUP).astype(F32) for p in
                          (diff_lambda_q1, diff_lambda_k1, diff_lambda_q2, diff_lambda_k2)]
            subln = diff_subln[j].reshape(1, 2 * HEAD_GROUP)
            o_p = _diff_prompt(qb, kb, vb, slopes2, lam_params, subln, bp, sp, lambda_init)
            o_s = _diff_sample(qb, kb, vb, cache_diff_k, cache_diff_v, j, slopes2, lam_params, subln,
                               n_p, bs, ss, lambda_init)
            w_out_b = diff_w_out_b
            outs["dkp"].append(k_p.reshape(bp, sp, DIFF_HEADS, 2 * HEAD_GROUP))
            outs["dvp"].append(v_p.reshape(bp, sp, DIFF_HEADS, 2 * HEAD_GROUP))
            outs["dks"].append(k_s.reshape(bs, ss, DIFF_HEADS, 2 * HEAD_GROUP))
            outs["dvs"].append(v_s.reshape(bs, ss, DIFF_HEADS, 2 * HEAD_GROUP))
        else:
            bf = jnp.pad(fox_b_f[j], (0, LANES - FOX_HEADS)).reshape(1, LANES)
            qb, kb, vb, sg, lf, kt_p, vt_p, lt_p, kt_s, vt_s, lt_s = _inproj_fox(
                x, gm, fox_w_main_b, fox_w_f_b, j, bf, _tile_gain(fox_q_norm[j], d),
                _tile_gain(fox_k_norm[j], d), summ, expand, n_p, sp, ss)
            tk_c = SAMPLE_TK if past % SAMPLE_TK == 0 else past
            c_new = _coff(lt_s, ss, False)
            c_past = _coff(cache_flt[j], tk_c, True)
            o_p = _fox_prompt(qb, kb, vb, lf, sg, bp, sp)
            o_s = _fox_sample(qb, kb, vb, cache_fkt, cache_fvt, j, c_past, c_new, sg, n_p, bs, ss)
            w_out_b = fox_w_out_b
            outs["fkp"].append(from_feature_major(kt_p, sp))
            outs["fvp"].append(from_feature_major(vt_p, sp))
            outs["flp"].append(lt_p.transpose(0, 2, 1))
            outs["fks"].append(from_feature_major(kt_s, ss))
            outs["fvs"].append(from_feature_major(vt_s, ss))
            outs["fls"].append(lt_s.transpose(0, 2, 1))
        gf = norm_ffn[i].reshape(1, d)
        x, meta = _outproj(o_p, o_s, x, w_out_b, j, gf, wr_hi[i], wr_lo[i], b_router[i].reshape(1, LANES))
        bucket = meta[:, 0].astype(jnp.int32)
        perm, inv, sched = _moe_schedule(bucket, n)
        xs = x[perm]
        ms = meta[:, :8][perm]
        ys = _moe(sched, xs, ms, gf, wgu_b, wd_b, i)
        x = ys[inv]

    y_prompt = x[:n_p].reshape(bp, sp, d)
    y_sample = x[n_p:].reshape(bs, ss, d)
    stk = lambda name: jnp.stack(outs[name])
    return (y_prompt, y_sample, stk("dkp"), stk("dvp"), stk("fkp"), stk("fvp"), stk("flp"),
            stk("dks"), stk("dvs"), stk("fks"), stk("fvs"), stk("fls"))
```

```python
import functools
import math

import numpy as np
import jax
import jax.numpy as jnp
from jax import lax
from jax.experimental import pallas as pl
from jax.experimental.pallas import tpu as pltpu

F32 = jnp.float32
BF16 = jnp.bfloat16

CHUNK_SHIFT = 6
DIFF_HEADS = 8
FOX_HEADS = 16
HEAD_GROUP = 64
N_GROUPS = 4
EXPERTS_PER_GROUP = 4
N_EXPERTS = N_GROUPS * EXPERTS_PER_GROUP
N_PAIRS = 6
N_BUCKETS = N_GROUPS * N_PAIRS
EPS = 1e-6
NEG = -1e30
LOG2E = math.log2(math.e)
QSCALE = (HEAD_GROUP ** -0.5) * LOG2E

LANES = 128
TOK_TILE = 512
ATT_TQ = 512
ATT_TK = 512
SAMPLE_TK = 1024
MOE_TILE = 256
VMEM_LIMIT = 56 * 1024 * 1024

PAIR_A = np.array([0, 0, 0, 1, 1, 2], np.int32)
PAIR_B = np.array([1, 2, 3, 2, 3, 3], np.int32)


def _cparams(n_axes):
    return pltpu.CompilerParams(dimension_semantics=("arbitrary",) * n_axes,
                                vmem_limit_bytes=VMEM_LIMIT)


def _rms_rows(x, gain):
    ms = jnp.mean(x * x, axis=-1, keepdims=True)
    return x * lax.rsqrt(ms + EPS) * gain


def _group_norm(z, gain, sum_ref, exp_ref):
    ms = jnp.dot((z * z).astype(BF16), sum_ref[...], preferred_element_type=F32)
    r = lax.rsqrt(ms + EPS)
    r_hi = r.astype(BF16)
    r_lo = (r - r_hi.astype(F32)).astype(BF16)
    rb = jnp.dot(jnp.concatenate([r_hi, r_lo], axis=-1), exp_ref[...], preferred_element_type=F32)
    return z * rb * gain


def _group_mats(d):
    n_g = d // HEAD_GROUP
    col = np.arange(d) // HEAD_GROUP
    summ = np.zeros((d, LANES), np.float32)
    summ[np.arange(d), col] = 1.0 / HEAD_GROUP
    expand = np.zeros((2 * LANES, d), np.float32)
    expand[col, np.arange(d)] = 1.0
    expand[LANES + col, np.arange(d)] = 1.0
    assert n_g <= LANES
    return jnp.asarray(summ, BF16), jnp.asarray(expand, BF16)


def _put_heads(ref, val):
    rows = val.shape[0]
    n_heads = ref.shape[0] // rows
    for h in range(n_heads):
        ref[pl.ds(h, rows, stride=n_heads), :] = val[:, h * LANES:(h + 1) * LANES]


def _inproj_diff_kernel(tiles_p, x_ref, gm_ref, w_ref, qg_ref, kg_ref, sum_ref, exp_ref,
                        qb_ref, kb_ref, vb_ref, kp_ref, vp_ref, ks_ref, vs_ref):
    d = x_ref.shape[1]
    i = pl.program_id(0)
    h = _rms_rows(x_ref[...], gm_ref[...]).astype(BF16)
    q = jnp.dot(h, w_ref[:, 0:d], preferred_element_type=F32)
    qb_ref[...] = (_group_norm(q, qg_ref[...], sum_ref, exp_ref) * QSCALE).astype(BF16)
    k = jnp.dot(h, w_ref[:, d:2 * d], preferred_element_type=F32)
    kn = _group_norm(k, kg_ref[...], sum_ref, exp_ref)
    kb_ref[...] = kn.astype(BF16)
    v = jnp.dot(h, w_ref[:, 2 * d:3 * d], preferred_element_type=F32)
    vb_ref[...] = v.astype(BF16)

    @pl.when(i < tiles_p)
    def _():
        _put_heads(kp_ref, kn)
        _put_heads(vp_ref, v)

    @pl.when(i >= tiles_p)
    def _():
        _put_heads(ks_ref, kn)
        _put_heads(vs_ref, v)


def _inproj_diff(x, gm, w_all, layer, qg, kg, summ, expand, n_p):
    n, d = x.shape
    tm = TOK_TILE
    tiles_p = n_p // tm
    row = lambda i: (i, 0)
    fixed = lambda i: (0, 0)
    big = pl.BlockSpec((tm, d), row)
    heads = pl.BlockSpec((tm * DIFF_HEADS, LANES), lambda i: (jnp.minimum(i, tiles_p - 1), 0))
    heads_s = pl.BlockSpec((tm * DIFF_HEADS, LANES), lambda i: (jnp.maximum(i - tiles_p, 0), 0))
    cache_p = jax.ShapeDtypeStruct((n_p * DIFF_HEADS, LANES), F32)
    cache_s = jax.ShapeDtypeStruct(((n - n_p) * DIFF_HEADS, LANES), F32)
    return pl.pallas_call(
        functools.partial(_inproj_diff_kernel, tiles_p),
        grid=(n // tm,),
        in_specs=[big,
                  pl.BlockSpec((1, d), fixed),
                  pl.BlockSpec((None, d, 3 * d), lambda i: (layer, 0, 0)),
                  pl.BlockSpec((1, d), fixed), pl.BlockSpec((1, d), fixed),
                  pl.BlockSpec(summ.shape, fixed), pl.BlockSpec(expand.shape, fixed)],
        out_specs=[big, big, big, heads, heads, heads_s, heads_s],
        out_shape=[jax.ShapeDtypeStruct((n, d), BF16), jax.ShapeDtypeStruct((n, d), BF16),
                   jax.ShapeDtypeStruct((n, d), BF16), cache_p, cache_p, cache_s, cache_s],
        compiler_params=_cparams(1),
        name="inproj_diff",
    )(x, gm, w_all, qg, kg, summ, expand)


def _inproj_fox_kernel(tiles_p, x_ref, gm_ref, w_ref, wf_ref, bf_ref, qg_ref, kg_ref, sum_ref, exp_ref,
                       qb_ref, kb_ref, vb_ref, sg_ref, lf_ref,
                       kp_ref, vp_ref, lp_ref, ks_ref, vs_ref, ls_ref):
    d = x_ref.shape[1]
    i = pl.program_id(0)
    h = _rms_rows(x_ref[...], gm_ref[...]).astype(BF16)
    q = jnp.dot(h, w_ref[:, 0:d], preferred_element_type=F32)
    qb_ref[...] = (_group_norm(q, qg_ref[...], sum_ref, exp_ref) * QSCALE).astype(BF16)
    k = jnp.dot(h, w_ref[:, d:2 * d], preferred_element_type=F32)
    kn = _group_norm(k, kg_ref[...], sum_ref, exp_ref)
    kb_ref[...] = kn.astype(BF16)
    v = jnp.dot(h, w_ref[:, 2 * d:3 * d], preferred_element_type=F32)
    vb_ref[...] = v.astype(BF16)
    g = jnp.dot(h, w_ref[:, 3 * d:4 * d], preferred_element_type=F32)
    sg_ref[...] = (1.0 / (1.0 + jnp.exp(-g))).astype(BF16)
    f = jnp.dot(h, wf_ref[...], preferred_element_type=F32) + bf_ref[...]
    lf = jnp.minimum(f, 0.0) - jnp.log(1.0 + jnp.exp(-jnp.abs(f)))
    lf_ref[...] = lf
    n_heads = lp_ref.shape[0]

    @pl.when(i < tiles_p)
    def _():
        kp_ref[...] = kn.T
        vp_ref[...] = v.T
        lp_ref[...] = lf.T[:n_heads, :]

    @pl.when(i >= tiles_p)
    def _():
        seq = ks_ref.shape[2]
        for bb in range(ks_ref.shape[0]):
            rows = slice(bb * seq, (bb + 1) * seq)
            ks_ref[bb] = kn[rows, :].T
            vs_ref[bb] = v[rows, :].T
            ls_ref[bb] = lf[rows, :].T[:n_heads, :]


def _inproj_fox(x, gm, w_all, wf_all, layer, bf, qg, kg, summ, expand, n_p, seq_p, seq_s):
    n, d = x.shape
    tm = TOK_TILE
    tiles_p = n_p // tm
    per_b = seq_p // tm
    grp = tm // seq_s
    row = lambda i: (i, 0)
    fixed = lambda i: (0, 0)
    big = pl.BlockSpec((tm, d), row)

    def p_map(i):
        ip = jnp.minimum(i, tiles_p - 1)
        return (ip // per_b, 0, ip % per_b)

    s_map = lambda i: (jnp.maximum(i - tiles_p, 0), 0, 0)
    f32 = lambda *shape: jax.ShapeDtypeStruct(shape, F32)
    bf16 = jax.ShapeDtypeStruct((n, d), BF16)
    return pl.pallas_call(
        functools.partial(_inproj_fox_kernel, tiles_p),
        grid=(n // tm,),
        in_specs=[big,
                  pl.BlockSpec((1, d), fixed),
                  pl.BlockSpec((None, d, 4 * d), lambda i: (layer, 0, 0)),
                  pl.BlockSpec((None, d, LANES), lambda i: (layer, 0, 0)),
                  pl.BlockSpec((1, LANES), fixed),
                  pl.BlockSpec((1, d), fixed), pl.BlockSpec((1, d), fixed),
                  pl.BlockSpec(summ.shape, fixed), pl.BlockSpec(expand.shape, fixed)],
        out_specs=[big, big, big, big, pl.BlockSpec((tm, LANES), row),
                   pl.BlockSpec((None, d, tm), p_map), pl.BlockSpec((None, d, tm), p_map),
                   pl.BlockSpec((None, FOX_HEADS, tm), p_map),
                   pl.BlockSpec((grp, d, seq_s), s_map), pl.BlockSpec((grp, d, seq_s), s_map),
                   pl.BlockSpec((grp, FOX_HEADS, seq_s), s_map)],
        out_shape=[bf16, bf16, bf16, bf16, f32(n, LANES),
                   f32(n_p // seq_p, d, seq_p), f32(n_p // seq_p, d, seq_p),
                   f32(n_p // seq_p, FOX_HEADS, seq_p),
                   f32((n - n_p) // seq_s, d, seq_s), f32((n - n_p) // seq_s, d, seq_s),
                   f32((n - n_p) // seq_s, FOX_HEADS, seq_s)],
        compiler_params=_cparams(1),
        name="inproj_fox",
    )(x, gm, w_all, wf_all, bf, qg, kg, summ, expand)


def _half_masks(rows):
    lane = lax.broadcasted_iota(jnp.int32, (rows, LANES), 1)
    return lane < HEAD_GROUP, lane >= HEAD_GROUP


NT_DIMS = (((1,), (1,)), ((), ()))


def _row_update(idx, s, pv, m_ref, l_ref, acc_ref):
    m_prev = m_ref[idx]
    m_new = jnp.maximum(m_prev, jnp.max(s, axis=-1, keepdims=True))
    alpha = jnp.exp2(m_prev - m_new)
    p = jnp.exp2(s - m_new)
    l_ref[idx] = alpha * l_ref[idx] + jnp.sum(p, axis=-1, keepdims=True)
    acc_ref[idx] = alpha * acc_ref[idx] + pv(p.astype(BF16))
    m_ref[idx] = m_new


def _init_state(m_ref, l_ref, acc_ref):
    m_ref[...] = jnp.full(m_ref.shape, NEG, F32)
    l_ref[...] = jnp.zeros(l_ref.shape, F32)
    acc_ref[...] = jnp.zeros(acc_ref.shape, F32)


def _diff_lambda(lq1, lk1, lq2, lk2, lambda_init):
    a = jnp.exp(jnp.sum(lq1[...] * lk1[...], axis=-1, keepdims=True))
    b = jnp.exp(jnp.sum(lq2[...] * lk2[...], axis=-1, keepdims=True))
    return a - b + lambda_init


def _alibi_table(slope2, rows, cols, q0, k0):
    i = lax.broadcasted_iota(jnp.int32, (rows, cols), 0)
    j = lax.broadcasted_iota(jnp.int32, (rows, cols), 1)
    qpos = i + q0
    kpos = j + k0
    allowed = jnp.right_shift(kpos, CHUNK_SHIFT) <= jnp.right_shift(qpos, CHUNK_SHIFT)
    bias = slope2 * (i - jnp.abs(qpos - kpos)).astype(F32)
    return jnp.where(allowed, bias, NEG)


BIAS_TERMS = 3


def _bias_lanes(c):
    return HEAD_GROUP * (1 - c)


def _own_lanes(lane, c):
    return (lane >= HEAD_GROUP * c) & (lane < HEAD_GROUP * (c + 1))


def _augment_keys(k_blk, bias, c):
    lane = lax.broadcasted_iota(jnp.int32, k_blk.shape, 1)
    b0 = _bias_lanes(c)
    out = jnp.where(_own_lanes(lane, c), k_blk, jnp.zeros_like(k_blk))
    rest = bias
    for t in range(BIAS_TERMS):
        term = rest.astype(BF16)
        out = jnp.where(lane == b0 + t, term, out)
        rest = rest - term.astype(F32)
    return out


def _augment_queries(q, c):
    lane = lax.broadcasted_iota(jnp.int32, q.shape, 1)
    b0 = _bias_lanes(c)
    ones = jnp.where((lane >= b0) & (lane < b0 + BIAS_TERMS), 1.0, 0.0).astype(q.dtype)
    return jnp.where(_own_lanes(lane, c), q, ones)


def _transpose_values(v_ref, vt_ref):
    nblk, _, tk = vt_ref.shape
    for blk in range(nblk):
        vt_ref[blk] = v_ref[blk * tk:(blk + 1) * tk, :].astype(F32).T.astype(vt_ref.dtype)


def _scores_t(ka, qa, q_lo):
    return lax.dot_general(ka, qa[q_lo:, :], NT_DIMS, preferred_element_type=F32)


def _block_t(idx, st, vt, tbl, q_lo, m_ref, l_ref, acc_ref):
    at = idx + (slice(None), slice(q_lo, None))
    if tbl is not None:
        st = st + tbl
    m_prev = m_ref[at]
    m_new = jnp.maximum(m_prev, jnp.max(st, axis=0, keepdims=True))
    alpha = jnp.exp2(m_prev - m_new)
    pt = jnp.exp2(st - m_new)
    l_ref[at] = alpha * l_ref[at] + jnp.sum(pt, axis=0, keepdims=True)
    acc_ref[at] = alpha * acc_ref[at] + jnp.dot(vt, pt.astype(BF16), preferred_element_type=F32)
    m_ref[at] = m_new


def _sweep_t(qa, kaug_ref, vt_ref, tbl_ref, m_ref, l_ref, acc_ref, v_rows):
    nblk, _, tk = vt_ref.shape
    nq = len(qa)
    tq = qa[0][0].shape[0]
    per_q = tq // tk
    for ki in range(nblk):
        vt = vt_ref[ki]
        ka = [kaug_ref[c, ki * tk:(ki + 1) * tk, :] for c in range(2)]
        for qi in range(ki // per_q, nq):
            diag = qi == ki // per_q
            dblk = ki % per_q
            q_lo = dblk * tk if diag else 0
            tbl = tbl_ref[dblk, :, q_lo:] if diag else None
            st = [_scores_t(ka[c], qa[qi][c], q_lo) for c in range(2)]
            for c in range(2):
                _block_t((qi, c), st[c], vt[v_rows[c], :], tbl, q_lo, m_ref, l_ref, acc_ref)


def _augmented_query_tiles(q_ref, tq):
    return [[_augment_queries(q_ref[qi * tq:(qi + 1) * tq, :], c) for c in range(2)]
            for qi in range(q_ref.shape[0] // tq)]


def _diff_prompt_kernel(lambda_init, sl_ref, lq1, lk1, lq2, lk2, q_ref, k_ref, v_ref, g_ref,
                        o_ref, kaug_ref, vt_ref, m_ref, l_ref, acc_ref, tbl_ref):
    nq, _, _, tq = m_ref.shape
    nblk, _, tk = vt_ref.shape
    slope2 = sl_ref[pl.program_id(1)]
    for blk in range(nblk):
        rows = slice(blk * tk, (blk + 1) * tk)
        kpos = lax.broadcasted_iota(jnp.int32, (tk, LANES), 0) + blk * tk
        bias = slope2 * kpos.astype(F32)
        for c in range(2):
            kaug_ref[c, rows, :] = _augment_keys(k_ref[rows, :], bias, c)
    _transpose_values(v_ref, vt_ref)
    for dblk in range(tq // tk):
        dj = lax.broadcasted_iota(jnp.int32, (tk, tq), 0) + dblk * tk
        i = lax.broadcasted_iota(jnp.int32, (tk, tq), 1)
        allowed = jnp.right_shift(dj, CHUNK_SHIFT) <= jnp.right_shift(i, CHUNK_SHIFT)
        ahead = jnp.maximum(dj - i, 0).astype(F32)
        tbl_ref[dblk] = jnp.where(allowed, (-2.0 * slope2) * ahead, NEG)

    _init_state(m_ref, l_ref, acc_ref)
    whole = slice(None)
    _sweep_t(_augmented_query_tiles(q_ref, tq), kaug_ref, vt_ref, tbl_ref, m_ref, l_ref, acc_ref,
             (whole, whole))
    lam = _diff_lambda(lq1, lk1, lq2, lk2, lambda_init)
    for qi in range(nq):
        ot = acc_ref[qi, 0] / l_ref[qi, 0] - lam * (acc_ref[qi, 1] / l_ref[qi, 1])
        ms = jnp.mean(ot * ot, axis=0, keepdims=True)
        o = (ot * lax.rsqrt(ms + EPS)).T
        o_ref[qi * tq:(qi + 1) * tq, :] = ((o * g_ref[...]) * (1.0 - lambda_init)).astype(o_ref.dtype)


def _diff_prompt(qb, kb, vb, slopes2, lam_params, subln, n_batch, seq, lambda_init):
    d = qb.shape[1]
    tq = min(ATT_TQ, seq)
    tk = min(ATT_TK, tq)
    nq = seq // tq
    hd2 = 2 * HEAD_GROUP
    small = pl.BlockSpec((1, HEAD_GROUP), lambda b, h: (0, 0))
    whole_seq = pl.BlockSpec((seq, hd2), lambda b, h: (b, h))
    return pl.pallas_call(
        functools.partial(_diff_prompt_kernel, lambda_init),
        grid=(n_batch, DIFF_HEADS),
        in_specs=[pl.BlockSpec(memory_space=pltpu.SMEM), small, small, small, small,
                  whole_seq, whole_seq, whole_seq,
                  pl.BlockSpec((1, hd2), lambda b, h: (0, 0))],
        out_specs=whole_seq,
        out_shape=jax.ShapeDtypeStruct((n_batch * seq, d), BF16),
        scratch_shapes=[pltpu.VMEM((2, seq, hd2), BF16), pltpu.VMEM((seq // tk, hd2, tk), BF16),
                        pltpu.VMEM((nq, 2, 1, tq), F32), pltpu.VMEM((nq, 2, 1, tq), F32),
                        pltpu.VMEM((nq, 2, hd2, tq), F32), pltpu.VMEM((tq // tk, tk, tq), F32)],
        compiler_params=_cparams(2),
        name="diff_attn_prompt",
    )(slopes2, *lam_params, qb, kb, vb, subln)


def _diff_sample_kernel(lambda_init, past, sl_ref, lq1, lk1, lq2, lk2, q_ref, kp_ref, vp_ref,
                        kn_ref, vn_ref, g_ref, o_ref, m_ref, l_ref, acc_ref):
    tq = q_ref.shape[0]
    n_heads = m_ref.shape[0]
    hd2 = kp_ref.shape[1]
    tk = kp_ref.shape[0] // n_heads
    step = pl.program_id(1)

    @pl.when(step == 0)
    def _():
        _init_state(m_ref, l_ref, acc_ref)

    lo, hi = _half_masks(tq)
    col = lax.broadcasted_iota(jnp.int32, (1, tk), 1) + (step * tk - past)

    def q_streams(h):
        q = q_ref[:, h * hd2:(h + 1) * hd2]
        zero = jnp.zeros_like(q)
        return jnp.where(lo, q, zero), jnp.where(hi, q, zero)

    for h in range(n_heads):
        k_h = kp_ref[pl.ds(h, tk, stride=n_heads), :].astype(BF16)
        v_h = vp_ref[pl.ds(h, tk, stride=n_heads), :].astype(BF16)
        bias = sl_ref[h] * col.astype(F32)
        for c, q_c in enumerate(q_streams(h)):
            s = lax.dot_general(q_c, k_h, NT_DIMS, preferred_element_type=F32) + bias
            _row_update((h, c), s, lambda p: jnp.dot(p, v_h, preferred_element_type=F32),
                        m_ref, l_ref, acc_ref)

    @pl.when(step == pl.num_programs(1) - 1)
    def _():
        lam = _diff_lambda(lq1, lk1, lq2, lk2, lambda_init)
        for h in range(n_heads):
            cols = slice(h * hd2, (h + 1) * hd2)
            k_h = kn_ref[:, cols]
            v_h = vn_ref[:, cols]
            bias = _alibi_table(sl_ref[h], tq, tq, past, past)
            for c, q_c in enumerate(q_streams(h)):
                s = lax.dot_general(q_c, k_h, NT_DIMS, preferred_element_type=F32) + bias
                _row_update((h, c), s, lambda p: jnp.dot(p, v_h, preferred_element_type=F32),
                            m_ref, l_ref, acc_ref)
            o = acc_ref[h, 0] / l_ref[h, 0] - lam * (acc_ref[h, 1] / l_ref[h, 1])
            ms = jnp.mean(o * o, axis=-1, keepdims=True)
            o_ref[:, cols] = ((o * lax.rsqrt(ms + EPS) * g_ref[...]) * (1.0 - lambda_init)).astype(o_ref.dtype)


def _diff_sample(qb, kb, vb, cache_k, cache_v, layer, slopes2, lam_params, subln, row0, n_batch, seq,
                 lambda_init):
    d = qb.shape[1]
    past, n_heads, hd2 = cache_k.shape[2:]
    tk = SAMPLE_TK if past % SAMPLE_TK == 0 else past
    blk0 = row0 // seq
    rows = lambda a: a.reshape(a.shape[:2] + (past * n_heads, hd2))
    cache_k, cache_v = rows(cache_k), rows(cache_v)
    small = pl.BlockSpec((1, HEAD_GROUP), lambda b, c: (0, 0))
    new = pl.BlockSpec((seq, d), lambda b, c: (blk0 + b, 0))
    old = pl.BlockSpec((None, None, tk * n_heads, hd2), lambda b, c: (layer, b, c, 0))
    return pl.pallas_call(
        functools.partial(_diff_sample_kernel, lambda_init, past),
        grid=(n_batch, past // tk),
        in_specs=[pl.BlockSpec(memory_space=pltpu.SMEM), small, small, small, small,
                  new, old, old, new, new, pl.BlockSpec((1, hd2), lambda b, c: (0, 0))],
        out_specs=pl.BlockSpec((seq, d), lambda b, c: (b, 0)),
        out_shape=jax.ShapeDtypeStruct((n_batch * seq, d), BF16),
        scratch_shapes=[pltpu.VMEM((n_heads, 2, seq, 1), F32), pltpu.VMEM((n_heads, 2, seq, 1), F32),
                        pltpu.VMEM((n_heads, 2, seq, hd2), F32)],
        compiler_params=_cparams(2),
        name="diff_attn_sample",
    )(slopes2, *lam_params, qb, cache_k, cache_v, kb, vb, subln)


def _causal_table(rows, cols):
    i = lax.broadcasted_iota(jnp.int32, (rows, cols), 0)
    j = lax.broadcasted_iota(jnp.int32, (rows, cols), 1)
    return jnp.where(j <= i, 0.0, NEG).astype(F32)


def _fox_prompt_kernel(q_ref, k_ref, v_ref, lf_ref, sg_ref, o_ref,
                       cum_ref, kaug_ref, vt_ref, m_ref, l_ref, acc_ref, tbl_ref):
    nq, _, _, tq = m_ref.shape
    nblk, _, tk = vt_ref.shape
    pair = pl.program_id(1)

    @pl.when(pair == 0)
    def _():
        r = lax.broadcasted_iota(jnp.int32, (tk, tk), 0)
        c = lax.broadcasted_iota(jnp.int32, (tk, tk), 1)
        tril = jnp.where(c <= r, 1.0, 0.0).astype(BF16)
        carry = jnp.zeros((1, LANES), F32)
        for blk in range(nblk):
            x = lf_ref[blk * tk:(blk + 1) * tk, :]
            hi = x.astype(BF16)
            lo = (x - hi.astype(F32)).astype(BF16)
            part = (jnp.dot(tril, hi, preferred_element_type=F32)
                    + jnp.dot(tril, lo, preferred_element_type=F32)) + carry
            cum_ref[blk * tk:(blk + 1) * tk, :] = part * (-LOG2E)
            carry = carry + jnp.sum(x, axis=0, keepdims=True)

    lane = lax.broadcasted_iota(jnp.int32, (tk, LANES), 1)
    for blk in range(nblk):
        rows = slice(blk * tk, (blk + 1) * tk)
        cum = cum_ref[rows, :]
        for c in range(2):
            col = jnp.sum(jnp.where(lane == 2 * pair + c, cum, 0.0), axis=-1, keepdims=True)
            bias = jnp.broadcast_to(col, (tk, LANES))
            kaug_ref[c, rows, :] = _augment_keys(k_ref[rows, :], bias, c)
    _transpose_values(v_ref, vt_ref)
    for dblk in range(tq // tk):
        dj = lax.broadcasted_iota(jnp.int32, (tk, tq), 0) + dblk * tk
        i = lax.broadcasted_iota(jnp.int32, (tk, tq), 1)
        tbl_ref[dblk] = jnp.where(dj <= i, 0.0, NEG).astype(F32)

    _init_state(m_ref, l_ref, acc_ref)
    halves = (slice(0, HEAD_GROUP), slice(HEAD_GROUP, 2 * HEAD_GROUP))
    _sweep_t(_augmented_query_tiles(q_ref, tq), kaug_ref, vt_ref, tbl_ref, m_ref, l_ref, acc_ref, halves)
    for qi in range(nq):
        rows = slice(qi * tq, (qi + 1) * tq)
        ot = jnp.concatenate([acc_ref[qi, 0] / l_ref[qi, 0], acc_ref[qi, 1] / l_ref[qi, 1]], axis=0)
        o_ref[rows, :] = (ot.T * sg_ref[rows, :].astype(F32)).astype(o_ref.dtype)


def _fox_prompt(qb, kb, vb, lf, sg, n_batch, seq):
    d = qb.shape[1]
    tq = min(ATT_TQ, seq)
    tk = min(ATT_TK, tq)
    nq = seq // tq
    w = LANES
    whole_seq = pl.BlockSpec((seq, w), lambda b, p: (b, p))
    return pl.pallas_call(
        _fox_prompt_kernel,
        grid=(n_batch, d // w),
        in_specs=[whole_seq, whole_seq, whole_seq,
                  pl.BlockSpec((seq, LANES), lambda b, p: (b, 0)),
                  whole_seq],
        out_specs=whole_seq,
        out_shape=jax.ShapeDtypeStruct((n_batch * seq, d), BF16),
        scratch_shapes=[pltpu.VMEM((seq, LANES), F32),
                        pltpu.VMEM((2, seq, w), BF16), pltpu.VMEM((seq // tk, w, tk), BF16),
                        pltpu.VMEM((nq, 2, 1, tq), F32), pltpu.VMEM((nq, 2, 1, tq), F32),
                        pltpu.VMEM((nq, 2, HEAD_GROUP, tq), F32), pltpu.VMEM((tq // tk, tk, tq), F32)],
        compiler_params=_cparams(2),
        name="fox_attn_prompt",
    )(qb, kb, vb, lf, sg)


def _fox_sample_kernel(q_ref, kp_ref, vp_ref, kn_ref, vn_ref, cp_ref, cn_ref, sg_ref,
                       o_ref, m_ref, l_ref, acc_ref):
    tq = q_ref.shape[0]
    n_heads, hd, tk = kp_ref.shape
    step = pl.program_id(1)

    @pl.when(step == 0)
    def _():
        _init_state(m_ref, l_ref, acc_ref)

    for h in range(n_heads):
        q_h = q_ref[:, h * hd:(h + 1) * hd]
        kt_h = kp_ref[h].astype(BF16)
        vt_h = vp_ref[h].astype(BF16)
        s = jnp.dot(q_h, kt_h, preferred_element_type=F32) + cp_ref[h:h + 1, :]
        _row_update(h, s, lambda p: lax.dot_general(p, vt_h, NT_DIMS, preferred_element_type=F32),
                    m_ref, l_ref, acc_ref)

    @pl.when(step == pl.num_programs(1) - 1)
    def _():
        tbl = _causal_table(tq, tq)
        for h in range(n_heads):
            cols = slice(h * hd, (h + 1) * hd)
            q_h = q_ref[:, cols]
            k_h = kn_ref[:, cols]
            v_h = vn_ref[:, cols]
            s = lax.dot_general(q_h, k_h, NT_DIMS, preferred_element_type=F32) + (tbl + cn_ref[h:h + 1, :])
            _row_update(h, s, lambda p: jnp.dot(p, v_h, preferred_element_type=F32),
                        m_ref, l_ref, acc_ref)
            o = acc_ref[h] / l_ref[h]
            o_ref[:, cols] = (o * sg_ref[:, cols].astype(F32)).astype(o_ref.dtype)


def _fox_sample(qb, kb, vb, cache_kt, cache_vt, layer, c_past, c_new, sg, row0, n_batch, seq):
    d = qb.shape[1]
    n_heads, hd, past = cache_kt.shape[2:]
    tk = SAMPLE_TK if past % SAMPLE_TK == 0 else past
    blk0 = row0 // seq
    new = pl.BlockSpec((seq, d), lambda b, c: (blk0 + b, 0))
    old = pl.BlockSpec((None, None, n_heads, hd, tk), lambda b, c: (layer, b, 0, 0, c))
    return pl.pallas_call(
        _fox_sample_kernel,
        grid=(n_batch, past // tk),
        in_specs=[new, old, old, new, new,
                  pl.BlockSpec((None, n_heads, tk), lambda b, c: (b, 0, c)),
                  pl.BlockSpec((None, n_heads, seq), lambda b, c: (b, 0, 0)),
                  new],
        out_specs=pl.BlockSpec((seq, d), lambda b, c: (b, 0)),
        out_shape=jax.ShapeDtypeStruct((n_batch * seq, d), BF16),
        scratch_shapes=[pltpu.VMEM((n_heads, seq, 1), F32), pltpu.VMEM((n_heads, seq, 1), F32),
                        pltpu.VMEM((n_heads, seq, hd), F32)],
        compiler_params=_cparams(2),
        name="fox_attn_sample",
    )(qb, cache_kt, cache_vt, kb, vb, c_past, c_new, sg)


def _split_dot(x, mat):
    hi = x.astype(BF16)
    lo = (x - hi.astype(F32)).astype(BF16)
    return (jnp.dot(hi, mat, preferred_element_type=F32) + jnp.dot(lo, mat, preferred_element_type=F32))


def _coff_kernel(suffix, tk, x_ref, o_ref):
    nblk = x_ref.shape[1] // tk
    r = lax.broadcasted_iota(jnp.int32, (tk, tk), 0)
    c = lax.broadcasted_iota(jnp.int32, (tk, tk), 1)
    mat = jnp.where((r > c) if suffix else (r <= c), 1.0, 0.0).astype(BF16)
    carry = jnp.zeros((x_ref.shape[0], 1), F32)
    order = range(nblk - 1, -1, -1) if suffix else range(nblk)
    for blk in order:
        cols = slice(blk * tk, (blk + 1) * tk)
        x = x_ref[:, cols]
        part = _split_dot(x, mat) + carry
        o_ref[:, cols] = (part if suffix else -part) * LOG2E
        carry = carry + jnp.sum(x, axis=-1, keepdims=True)


def _coff(xt, tk, suffix):
    n_batch, n_head, length = xt.shape
    spec = pl.BlockSpec((None, n_head, length), lambda b: (b, 0, 0))
    return pl.pallas_call(
        functools.partial(_coff_kernel, suffix, tk),
        grid=(n_batch,),
        in_specs=[spec],
        out_specs=spec,
        out_shape=jax.ShapeDtypeStruct((n_batch, n_head, length), F32),
        compiler_params=_cparams(1),
        name="fox_coff_suffix" if suffix else "fox_coff_prefix",
    )(xt)


def _outproj_kernel(tiles_p, op_ref, os_ref, x_ref, w_ref, gf_ref, wr_hi_ref, wr_lo_ref, br_ref,
                    xo_ref, meta_ref):
    o = jnp.where(pl.program_id(0) < tiles_p, op_ref[...], os_ref[...])
    xn = x_ref[...] + jnp.dot(o, w_ref[...], preferred_element_type=F32)
    xo_ref[...] = xn
    t = _rms_rows(xn, gf_ref[...])
    t_hi = t.astype(BF16)
    t_lo = (t - t_hi.astype(F32)).astype(BF16)
    logits = (jnp.dot(t_hi, wr_hi_ref[...], preferred_element_type=F32)
              + jnp.dot(t_lo, wr_hi_ref[...], preferred_element_type=F32)
              + jnp.dot(t_hi, wr_lo_ref[...], preferred_element_type=F32)) + br_ref[...]
    rows = logits.shape[0]
    lane = lax.broadcasted_iota(jnp.int32, (rows, LANES), 1).astype(F32)
    big = float(LANES)

    def first_argmax(vals, vmax):
        return jnp.min(jnp.where(vals == vmax, lane, big), axis=-1, keepdims=True)

    gl = jnp.where(lane < N_GROUPS, logits, NEG)
    gmax = jnp.max(gl, axis=-1, keepdims=True)
    gsum = jnp.sum(jnp.where(lane < N_GROUPS, jnp.exp(logits - gmax), 0.0), axis=-1, keepdims=True)
    p_top = 1.0 / gsum
    g_idx = first_argmax(gl, gmax)
    base = N_GROUPS + EXPERTS_PER_GROUP * g_idx
    el = jnp.where((lane >= base) & (lane < base + EXPERTS_PER_GROUP), logits, NEG)
    v1 = jnp.max(el, axis=-1, keepdims=True)
    i1 = first_argmax(el, v1)
    el2 = jnp.where(lane == i1, NEG, el)
    v2 = jnp.max(el2, axis=-1, keepdims=True)
    i2 = first_argmax(el2, v2)
    e2 = jnp.exp(v2 - v1)
    w1 = p_top / (1.0 + e2)
    w2 = p_top * e2 / (1.0 + e2)
    a1 = i1 - base
    a2 = i2 - base
    lo = jnp.minimum(a1, a2)
    hi = jnp.maximum(a1, a2)
    wa = jnp.where(a1 < a2, w1, w2)
    wb = jnp.where(a1 < a2, w2, w1)
    pair = jnp.where(lo == 0.0, hi - 1.0, jnp.where(lo == 1.0, hi + 1.0, 5.0))
    bucket = g_idx * N_PAIRS + pair
    meta_ref[...] = jnp.where(lane == 0.0, bucket, jnp.where(lane == 1.0, wa, jnp.where(lane == 2.0, wb, 0.0)))


def _outproj(o_p, o_s, x, w_all, layer, gf, wr_hi, wr_lo, br):
    n, d = x.shape
    tm = TOK_TILE
    tiles_p = o_p.shape[0] // tm
    row = lambda i: (i, 0)
    fixed = lambda i: (0, 0)
    big = pl.BlockSpec((tm, d), row)
    return pl.pallas_call(
        functools.partial(_outproj_kernel, tiles_p),
        grid=(n // tm,),
        in_specs=[pl.BlockSpec((tm, d), lambda i: (jnp.minimum(i, tiles_p - 1), 0)),
                  pl.BlockSpec((tm, d), lambda i: (jnp.maximum(i - tiles_p, 0), 0)),
                  big,
                  pl.BlockSpec((None, d, d), lambda i: (layer, 0, 0)),
                  pl.BlockSpec((1, d), fixed),
                  pl.BlockSpec((d, LANES), fixed), pl.BlockSpec((d, LANES), fixed),
                  pl.BlockSpec((1, LANES), fixed)],
        out_specs=[big, pl.BlockSpec((tm, LANES), row)],
        out_shape=[jax.ShapeDtypeStruct((n, d), F32), jax.ShapeDtypeStruct((n, LANES), F32)],
        compiler_params=_cparams(1),
        name="outproj_router",
    )(o_p, o_s, x, w_all, gf, wr_hi, wr_lo, br)


def _moe_kernel(tile_ref, ea_ref, eb_ref, lo_ref, hi_ref, first_ref,
                xs_ref, ms_ref, gf_ref, wgu_a, wd_a, wgu_b, wd_b, out_ref):
    s = pl.program_id(0)
    lo = lo_ref[s]
    hi = hi_ref[s]

    @pl.when(first_ref[s] == 1)
    def _():
        out_ref[...] = xs_ref[...]

    @pl.when(hi > lo)
    def _():
        t = _rms_rows(xs_ref[...], gf_ref[...]).astype(BF16)
        rows = t.shape[0]
        r = lax.broadcasted_iota(jnp.int32, (rows, 1), 0)
        inside = (r >= lo) & (r < hi)
        ms = ms_ref[...]
        wa = jnp.where(inside, ms[:, 1:2], 0.0)
        wb = jnp.where(inside, ms[:, 2:3], 0.0)

        def expert(wgu, wd):
            de = wd.shape[0]
            gu = jnp.dot(t, wgu[...], preferred_element_type=F32)
            g = gu[:, :de]
            hid = (g / (1.0 + jnp.exp(-g)) * gu[:, de:]).astype(BF16)
            return jnp.dot(hid, wd[...], preferred_element_type=F32)

        out_ref[...] += wa * expert(wgu_a, wd_a) + wb * expert(wgu_b, wd_b)


def _moe(sched, xs, ms, gf, wgu_all, wd_all, layer):
    n, d = xs.shape
    de = wd_all.shape[2]
    n_steps = sched[0].shape[0]
    tile_map = lambda s, tile, ea, eb, lo, hi, first: (tile[s], 0)
    fixed = lambda s, *_: (0, 0)
    wa_map = lambda s, tile, ea, eb, lo, hi, first: (layer, ea[s], 0, 0)
    wb_map = lambda s, tile, ea, eb, lo, hi, first: (layer, eb[s], 0, 0)
    grid_spec = pltpu.PrefetchScalarGridSpec(
        num_scalar_prefetch=6,
        grid=(n_steps,),
        in_specs=[pl.BlockSpec((MOE_TILE, d), tile_map),
                  pl.BlockSpec((MOE_TILE, ms.shape[1]), tile_map),
                  pl.BlockSpec((1, d), fixed),
                  pl.BlockSpec((None, None, d, 2 * de), wa_map),
                  pl.BlockSpec((None, None, de, d), wa_map),
                  pl.BlockSpec((None, None, d, 2 * de), wb_map),
                  pl.BlockSpec((None, None, de, d), wb_map)],
        out_specs=pl.BlockSpec((MOE_TILE, d), tile_map),
    )
    return pl.pallas_call(
        _moe_kernel,
        grid_spec=grid_spec,
        out_shape=jax.ShapeDtypeStruct((n, d), F32),
        compiler_params=_cparams(1),
        name="moe_experts",
    )(*sched, xs, ms, gf, wgu_all, wd_all, wgu_all, wd_all)


def _moe_schedule(bucket, n):
    n_tiles = n // MOE_TILE
    n_steps = n_tiles + N_BUCKETS - 1
    perm = jnp.argsort(bucket, stable=True).astype(jnp.int32)
    sorted_b = bucket[perm]
    member = (bucket[None, :] == jnp.arange(N_BUCKETS, dtype=jnp.int32)[:, None]).astype(jnp.int32)
    running = jnp.cumsum(member, axis=1)
    counts = running[:, -1]
    offs = jnp.concatenate([jnp.zeros((1,), jnp.int32), jnp.cumsum(counts)])
    inv = offs[bucket] + jnp.sum(running * member, axis=0) - 1
    fb = sorted_b[::MOE_TILE]
    lb = sorted_b[MOE_TILE - 1::MOE_TILE]
    per_tile = lb - fb + 1
    starts = jnp.cumsum(per_tile) - per_tile
    total = jnp.sum(per_tile)
    s = jnp.arange(n_steps, dtype=jnp.int32)
    tile = jnp.clip(jnp.searchsorted(starts, s, side="right").astype(jnp.int32) - 1, 0, n_tiles - 1)
    valid = s < total
    bkt = jnp.where(valid, fb[tile] + (s - starts[tile]), lb[n_tiles - 1])
    lo = jnp.clip(offs[bkt] - tile * MOE_TILE, 0, MOE_TILE)
    hi = jnp.clip(offs[bkt + 1] - tile * MOE_TILE, 0, MOE_TILE)
    lo = jnp.where(valid, lo, 0)
    hi = jnp.where(valid, hi, 0)
    first = (valid & (s == starts[tile])).astype(jnp.int32)
    grp = bkt // N_PAIRS
    ea = grp * EXPERTS_PER_GROUP + jnp.asarray(PAIR_A)[bkt % N_PAIRS]
    eb = grp * EXPERTS_PER_GROUP + jnp.asarray(PAIR_B)[bkt % N_PAIRS]
    return perm, inv, (tile, ea.astype(jnp.int32), eb.astype(jnp.int32), lo.astype(jnp.int32),
                       hi.astype(jnp.int32), first)


def _tile_gain(g, d):
    return jnp.tile(g.astype(F32), d // g.shape[0]).reshape(1, d)


def kernel(x_prompt, x_sample, cache_diff_k, cache_diff_v, cache_fox_k, cache_fox_v, cache_fox_logf, norm_mix, norm_ffn, diff_w_in, diff_w_out, diff_q_norm, diff_k_norm, diff_lambda_q1, diff_lambda_k1, diff_lambda_q2, diff_lambda_k2, diff_subln, fox_w_in, fox_b_f, fox_w_out, fox_q_norm, fox_k_norm, moe_w_group, moe_b_group, moe_w_expert, moe_b_expert, moe_w_gate, moe_w_up, moe_w_down):
    bp, sp, d = x_prompt.shape
    bs, ss, _ = x_sample.shape
    n_p = bp * sp
    n_s = bs * ss
    n = n_p + n_s
    depth = norm_mix.shape[0]
    past = cache_diff_k.shape[2]
    assert n % TOK_TILE == 0 and n % MOE_TILE == 0 and n_p % TOK_TILE == 0
    assert d % LANES == 0 and sp % min(ATT_TQ, sp) == 0 and n_p % ss == 0
    assert DIFF_HEADS * 2 * HEAD_GROUP == d and FOX_HEADS * HEAD_GROUP == d

    summ, expand = _group_mats(d)
    x = jnp.concatenate([x_prompt.reshape(n_p, d), x_sample.reshape(n_s, d)], axis=0)

    diff_w_in_b = diff_w_in.astype(BF16)
    diff_w_out_b = diff_w_out.astype(BF16)
    fox_w_main_b = fox_w_in[:, :, :4 * d].astype(BF16)
    fox_w_f_b = jnp.pad(fox_w_in[:, :, 4 * d:], ((0, 0), (0, 0), (0, LANES - FOX_HEADS))).astype(BF16)
    fox_w_out_b = fox_w_out.astype(BF16)
    wgu_b = jnp.concatenate([moe_w_gate, moe_w_up], axis=-1).astype(BF16)
    wd_b = moe_w_down.astype(BF16)
    w_router = jnp.pad(jnp.concatenate([moe_w_group, moe_w_expert], axis=-1),
                       ((0, 0), (0, 0), (0, LANES - N_GROUPS - N_EXPERTS)))
    wr_hi = w_router.astype(BF16)
    wr_lo = (w_router - wr_hi.astype(F32)).astype(BF16)
    b_router = jnp.pad(jnp.concatenate([moe_b_group, moe_b_expert], axis=-1),
                       ((0, 0), (0, LANES - N_GROUPS - N_EXPERTS)))
    slopes2 = jnp.asarray(2.0 ** (-8.0 * np.arange(1, DIFF_HEADS + 1) / DIFF_HEADS) * LOG2E, F32)
    cache_fkt = jnp.transpose(cache_fox_k, (0, 1, 3, 4, 2))
    cache_fvt = jnp.transpose(cache_fox_v, (0, 1, 3, 4, 2))
    cache_flt = jnp.transpose(cache_fox_logf.astype(F32), (0, 1, 3, 2))

    def from_feature_major(a, seq):
        return a.reshape(a.shape[0], FOX_HEADS, HEAD_GROUP, seq).transpose(0, 3, 1, 2)

    outs = {name: [] for name in ("dkp", "dvp", "dks", "dvs", "fkp", "fvp", "flp", "fks", "fvs", "fls")}
    for i in range(depth):
        j = i // 2
        gm = norm_mix[i].reshape(1, d)
        if i % 2 == 0:
            lambda_init = 0.8 - 0.6 * math.exp(-0.3 * i)
            qb, kb, vb, k_p, v_p, k_s, v_s = _inproj_diff(
                x, gm, diff_w_in_b, j, _tile_gain(diff_q_norm[j], d), _tile_gain(diff_k_norm[j], d),
                summ, expand, n_p)
            lam_params = [p[j].reshape(1, HEAD_GROUP).astype(F32) for p in
                          (diff_lambda_q1, diff_lambda_k1, diff_lambda_q2, diff_lambda_k2)]
            subln = diff_subln[j].reshape(1, 2 * HEAD_GROUP)
            o_p = _diff_prompt(qb, kb, vb, slopes2, lam_params, subln, bp, sp, lambda_init)
            o_s = _diff_sample(qb, kb, vb, cache_diff_k, cache_diff_v, j, slopes2, lam_params, subln,
                               n_p, bs, ss, lambda_init)
            w_out_b = diff_w_out_b
            outs["dkp"].append(k_p.reshape(bp, sp, DIFF_HEADS, 2 * HEAD_GROUP))
            outs["dvp"].append(v_p.reshape(bp, sp, DIFF_HEADS, 2 * HEAD_GROUP))
            outs["dks"].append(k_s.reshape(bs, ss, DIFF_HEADS, 2 * HEAD_GROUP))
            outs["dvs"].append(v_s.reshape(bs, ss, DIFF_HEADS, 2 * HEAD_GROUP))
        else:
            bf = jnp.pad(fox_b_f[j], (0, LANES - FOX_HEADS)).reshape(1, LANES)
            qb, kb, vb, sg, lf, kt_p, vt_p, lt_p, kt_s, vt_s, lt_s = _inproj_fox(
                x, gm, fox_w_main_b, fox_w_f_b, j, bf, _tile_gain(fox_q_norm[j], d),
                _tile_gain(fox_k_norm[j], d), summ, expand, n_p, sp, ss)
            tk_c = SAMPLE_TK if past % SAMPLE_TK == 0 else past
            c_new = _coff(lt_s, ss, False)
            c_past = _coff(cache_flt[j], tk_c, True)
            o_p = _fox_prompt(qb, kb, vb, lf, sg, bp, sp)
            o_s = _fox_sample(qb, kb, vb, cache_fkt, cache_fvt, j, c_past, c_new, sg, n_p, bs, ss)
            w_out_b = fox_w_out_b
            outs["fkp"].append(from_feature_major(kt_p, sp))
            outs["fvp"].append(from_feature_major(vt_p, sp))
            outs["flp"].append(lt_p.transpose(0, 2, 1))
            outs["fks"].append(from_feature_major(kt_s, ss))
            outs["fvs"].append(from_feature_major(vt_s, ss))
            outs["fls"].append(lt_s.transpose(0, 2, 1))
        gf = norm_ffn[i].reshape(1, d)
        x, meta = _outproj(o_p, o_s, x, w_out_b, j, gf, wr_hi[i], wr_lo[i], b_router[i].reshape(1, LANES))
        bucket = meta[:, 0].astype(jnp.int32)
        perm, inv, sched = _moe_schedule(bucket, n)
        xs = x[perm]
        ms = meta[:, :8][perm]
        ys = _moe(sched, xs, ms, gf, wgu_b, wd_b, i)
        x = ys[inv]

    y_prompt = x[:n_p].reshape(bp, sp, d)
    y_sample = x[n_p:].reshape(bs, ss, d)
    stk = lambda name: jnp.stack(outs[name])
    return (y_prompt, y_sample, stk("dkp"), stk("dvp"), stk("fkp"), stk("fvp"), stk("flp"),
            stk("dks"), stk("dvs"), stk("fks"), stk("fvs"), stk("fls"))
```

```python
import functools
import math

import numpy as np
import jax
import jax.numpy as jnp
from jax import lax
from jax.experimental import pallas as pl
from jax.experimental.pallas import tpu as pltpu

F32 = jnp.float32
BF16 = jnp.bfloat16

CHUNK_SHIFT = 6
DIFF_HEADS = 8
FOX_HEADS = 16
HEAD_GROUP = 64
N_GROUPS = 4
EXPERTS_PER_GROUP = 4
N_EXPERTS = N_GROUPS * EXPERTS_PER_GROUP
N_PAIRS = 6
N_BUCKETS = N_GROUPS * N_PAIRS
EPS = 1e-6
NEG = -1e30
LOG2E = math.log2(math.e)
QSCALE = (HEAD_GROUP ** -0.5) * LOG2E

LANES = 128
TOK_TILE = 512
ATT_TQ = 512
ATT_TK = 512
DIAG_TK = 512
SAMPLE_TK = 1024
MOE_TILE = 256
VMEM_LIMIT = 56 * 1024 * 1024

PAIR_A = np.array([0, 0, 0, 1, 1, 2], np.int32)
PAIR_B = np.array([1, 2, 3, 2, 3, 3], np.int32)


def _cparams(n_axes):
    return pltpu.CompilerParams(dimension_semantics=("arbitrary",) * n_axes,
                                vmem_limit_bytes=VMEM_LIMIT)


def _rms_rows(x, gain):
    ms = jnp.mean(x * x, axis=-1, keepdims=True)
    return x * lax.rsqrt(ms + EPS) * gain


def _group_norm(z, gain, sum_ref, exp_ref):
    ms = jnp.dot((z * z).astype(BF16), sum_ref[...], preferred_element_type=F32)
    r = lax.rsqrt(ms + EPS)
    r_hi = r.astype(BF16)
    r_lo = (r - r_hi.astype(F32)).astype(BF16)
    rb = jnp.dot(jnp.concatenate([r_hi, r_lo], axis=-1), exp_ref[...], preferred_element_type=F32)
    return z * rb * gain


def _group_mats(d):
    n_g = d // HEAD_GROUP
    col = np.arange(d) // HEAD_GROUP
    summ = np.zeros((d, LANES), np.float32)
    summ[np.arange(d), col] = 1.0 / HEAD_GROUP
    expand = np.zeros((2 * LANES, d), np.float32)
    expand[col, np.arange(d)] = 1.0
    expand[LANES + col, np.arange(d)] = 1.0
    assert n_g <= LANES
    return jnp.asarray(summ, BF16), jnp.asarray(expand, BF16)


def _put_heads(ref, val):
    rows = val.shape[0]
    n_heads = ref.shape[0] // rows
    for h in range(n_heads):
        ref[pl.ds(h, rows, stride=n_heads), :] = val[:, h * LANES:(h + 1) * LANES]


def _inproj_diff_kernel(tiles_p, n_carried, x_ref, gm_ref, w_ref, qg_ref, kg_ref, sum_ref, exp_ref, *refs):
    qb_ref, kb_ref, vb_ref, kp_ref, vp_ref, ks_ref, vs_ref = refs[n_carried:]
    d = x_ref.shape[1]
    i = pl.program_id(0)
    h = _rms_rows(x_ref[...], gm_ref[...]).astype(BF16)
    q = jnp.dot(h, w_ref[:, 0:d], preferred_element_type=F32)
    qb_ref[...] = (_group_norm(q, qg_ref[...], sum_ref, exp_ref) * QSCALE).astype(BF16)
    k = jnp.dot(h, w_ref[:, d:2 * d], preferred_element_type=F32)
    kn = _group_norm(k, kg_ref[...], sum_ref, exp_ref)
    kb_ref[...] = kn.astype(BF16)
    v = jnp.dot(h, w_ref[:, 2 * d:3 * d], preferred_element_type=F32)
    vb_ref[...] = v.astype(BF16)

    @pl.when(i < tiles_p)
    def _():
        _put_heads(kp_ref, kn)
        _put_heads(vp_ref, v)

    @pl.when(i >= tiles_p)
    def _():
        _put_heads(ks_ref, kn)
        _put_heads(vs_ref, v)


def _carry_specs(carried, n_fixed_inputs, first_out):
    specs = [pl.BlockSpec(memory_space=pl.ANY)] * len(carried)
    aliases = {n_fixed_inputs + k: first_out + k for k in range(len(carried))}
    return specs, aliases


def _inproj_diff(x, gm, w_all, layer, n_layers, qg, kg, summ, expand, n_p, carried):
    n, d = x.shape
    tm = TOK_TILE
    tiles_p = n_p // tm
    row = lambda i: (i, 0)
    fixed = lambda i: (0, 0)
    big = pl.BlockSpec((tm, d), row)
    heads = pl.BlockSpec((None, tm * DIFF_HEADS, LANES), lambda i: (layer, jnp.minimum(i, tiles_p - 1), 0))
    heads_s = pl.BlockSpec((tm * DIFF_HEADS, LANES), lambda i: (jnp.maximum(i - tiles_p, 0), 0))
    cache_p = jax.ShapeDtypeStruct((n_layers, n_p * DIFF_HEADS, LANES), F32)
    cache_s = jax.ShapeDtypeStruct(((n - n_p) * DIFF_HEADS, LANES), F32)
    carry_specs, aliases = _carry_specs(carried, 7, 3)
    return pl.pallas_call(
        functools.partial(_inproj_diff_kernel, tiles_p, len(carried)),
        grid=(n // tm,),
        in_specs=[big,
                  pl.BlockSpec((1, d), fixed),
                  pl.BlockSpec((None, d, 3 * d), lambda i: (layer, 0, 0)),
                  pl.BlockSpec((1, d), fixed), pl.BlockSpec((1, d), fixed),
                  pl.BlockSpec(summ.shape, fixed), pl.BlockSpec(expand.shape, fixed)] + carry_specs,
        out_specs=[big, big, big, heads, heads, heads_s, heads_s],
        out_shape=[jax.ShapeDtypeStruct((n, d), BF16), jax.ShapeDtypeStruct((n, d), BF16),
                   jax.ShapeDtypeStruct((n, d), BF16), cache_p, cache_p, cache_s, cache_s],
        input_output_aliases=aliases,
        compiler_params=_cparams(1),
        name="inproj_diff",
    )(x, gm, w_all, qg, kg, summ, expand, *carried)


def _inproj_fox_kernel(tiles_p, n_carried, x_ref, gm_ref, w_ref, wf_ref, bf_ref, qg_ref, kg_ref,
                       sum_ref, exp_ref, *refs):
    (qb_ref, kb_ref, vb_ref, sg_ref, lf_ref,
     kp_ref, vp_ref, lp_ref, ks_ref, vs_ref, ls_ref) = refs[n_carried:]
    d = x_ref.shape[1]
    i = pl.program_id(0)
    h = _rms_rows(x_ref[...], gm_ref[...]).astype(BF16)
    q = jnp.dot(h, w_ref[:, 0:d], preferred_element_type=F32)
    qb_ref[...] = (_group_norm(q, qg_ref[...], sum_ref, exp_ref) * QSCALE).astype(BF16)
    k = jnp.dot(h, w_ref[:, d:2 * d], preferred_element_type=F32)
    kn = _group_norm(k, kg_ref[...], sum_ref, exp_ref)
    kb_ref[...] = kn.astype(BF16)
    v = jnp.dot(h, w_ref[:, 2 * d:3 * d], preferred_element_type=F32)
    vb_ref[...] = v.astype(BF16)
    g = jnp.dot(h, w_ref[:, 3 * d:4 * d], preferred_element_type=F32)
    sg_ref[...] = (1.0 / (1.0 + jnp.exp(-g))).astype(BF16)
    f = jnp.dot(h, wf_ref[...], preferred_element_type=F32) + bf_ref[...]
    lf = jnp.minimum(f, 0.0) - jnp.log(1.0 + jnp.exp(-jnp.abs(f)))
    lf_ref[...] = lf
    n_heads = lp_ref.shape[0]

    @pl.when(i < tiles_p)
    def _():
        kp_ref[...] = kn.T
        vp_ref[...] = v.T
        lp_ref[...] = lf.T[:n_heads, :]

    @pl.when(i >= tiles_p)
    def _():
        seq = ks_ref.shape[2]
        for bb in range(ks_ref.shape[0]):
            rows = slice(bb * seq, (bb + 1) * seq)
            ks_ref[bb] = kn[rows, :].T
            vs_ref[bb] = v[rows, :].T
            ls_ref[bb] = lf[rows, :].T[:n_heads, :]


def _inproj_fox(x, gm, w_all, wf_all, layer, n_layers, bf, qg, kg, summ, expand, n_p, seq_p, seq_s,
                carried):
    n, d = x.shape
    tm = TOK_TILE
    tiles_p = n_p // tm
    per_b = seq_p // tm
    grp = tm // seq_s
    row = lambda i: (i, 0)
    fixed = lambda i: (0, 0)
    big = pl.BlockSpec((tm, d), row)

    def p_map(i):
        ip = jnp.minimum(i, tiles_p - 1)
        return (ip // per_b, 0, ip % per_b)

    def stacked_map(i):
        return (layer,) + p_map(i)

    s_map = lambda i: (jnp.maximum(i - tiles_p, 0), 0, 0)
    f32 = lambda *shape: jax.ShapeDtypeStruct(shape, F32)
    bf16 = jax.ShapeDtypeStruct((n, d), BF16)
    carry_specs, aliases = _carry_specs(carried, 9, 5)
    return pl.pallas_call(
        functools.partial(_inproj_fox_kernel, tiles_p, len(carried)),
        grid=(n // tm,),
        in_specs=[big,
                  pl.BlockSpec((1, d), fixed),
                  pl.BlockSpec((None, d, 4 * d), lambda i: (layer, 0, 0)),
                  pl.BlockSpec((None, d, LANES), lambda i: (layer, 0, 0)),
                  pl.BlockSpec((1, LANES), fixed),
                  pl.BlockSpec((1, d), fixed), pl.BlockSpec((1, d), fixed),
                  pl.BlockSpec(summ.shape, fixed), pl.BlockSpec(expand.shape, fixed)] + carry_specs,
        out_specs=[big, big, big, big, pl.BlockSpec((tm, LANES), row),
                   pl.BlockSpec((None, None, d, tm), stacked_map),
                   pl.BlockSpec((None, None, d, tm), stacked_map),
                   pl.BlockSpec((None, FOX_HEADS, tm), p_map),
                   pl.BlockSpec((grp, d, seq_s), s_map), pl.BlockSpec((grp, d, seq_s), s_map),
                   pl.BlockSpec((grp, FOX_HEADS, seq_s), s_map)],
        out_shape=[bf16, bf16, bf16, bf16, f32(n, LANES),
                   f32(n_layers, n_p // seq_p, d, seq_p), f32(n_layers, n_p // seq_p, d, seq_p),
                   f32(n_p // seq_p, FOX_HEADS, seq_p),
                   f32((n - n_p) // seq_s, d, seq_s), f32((n - n_p) // seq_s, d, seq_s),
                   f32((n - n_p) // seq_s, FOX_HEADS, seq_s)],
        input_output_aliases=aliases,
        compiler_params=_cparams(1),
        name="inproj_fox",
    )(x, gm, w_all, wf_all, bf, qg, kg, summ, expand, *carried)


def _half_masks(rows):
    lane = lax.broadcasted_iota(jnp.int32, (rows, LANES), 1)
    return lane < HEAD_GROUP, lane >= HEAD_GROUP


NT_DIMS = (((1,), (1,)), ((), ()))


def _row_update(idx, s, pv, m_ref, l_ref, acc_ref):
    m_prev = m_ref[idx]
    m_new = jnp.maximum(m_prev, jnp.max(s, axis=-1, keepdims=True))
    alpha = jnp.exp2(m_prev - m_new)
    p = jnp.exp2(s - m_new)
    l_ref[idx] = alpha * l_ref[idx] + jnp.sum(p, axis=-1, keepdims=True)
    acc_ref[idx] = alpha * acc_ref[idx] + pv(p.astype(BF16))
    m_ref[idx] = m_new


def _init_state(m_ref, l_ref, acc_ref):
    m_ref[...] = jnp.full(m_ref.shape, NEG, F32)
    l_ref[...] = jnp.zeros(l_ref.shape, F32)
    acc_ref[...] = jnp.zeros(acc_ref.shape, F32)


def _diff_lambda(lq1, lk1, lq2, lk2, lambda_init):
    a = jnp.exp(jnp.sum(lq1[...] * lk1[...], axis=-1, keepdims=True))
    b = jnp.exp(jnp.sum(lq2[...] * lk2[...], axis=-1, keepdims=True))
    return a - b + lambda_init


def _alibi_table(slope2, rows, cols, q0, k0):
    i = lax.broadcasted_iota(jnp.int32, (rows, cols), 0)
    j = lax.broadcasted_iota(jnp.int32, (rows, cols), 1)
    qpos = i + q0
    kpos = j + k0
    allowed = jnp.right_shift(kpos, CHUNK_SHIFT) <= jnp.right_shift(qpos, CHUNK_SHIFT)
    bias = slope2 * (i - jnp.abs(qpos - kpos)).astype(F32)
    return jnp.where(allowed, bias, NEG)


BIAS_TERMS = 3


def _bias_lanes(c):
    return HEAD_GROUP * (1 - c)


def _own_lanes(lane, c):
    return (lane >= HEAD_GROUP * c) & (lane < HEAD_GROUP * (c + 1))


def _augment_keys(k_blk, bias, c):
    lane = lax.broadcasted_iota(jnp.int32, k_blk.shape, 1)
    b0 = _bias_lanes(c)
    out = jnp.where(_own_lanes(lane, c), k_blk, jnp.zeros_like(k_blk))
    rest = bias
    for t in range(BIAS_TERMS):
        term = rest.astype(BF16)
        out = jnp.where(lane == b0 + t, term, out)
        rest = rest - term.astype(F32)
    return out


def _augment_queries(q, c):
    lane = lax.broadcasted_iota(jnp.int32, q.shape, 1)
    b0 = _bias_lanes(c)
    ones = jnp.where((lane >= b0) & (lane < b0 + BIAS_TERMS), 1.0, 0.0).astype(q.dtype)
    return jnp.where(_own_lanes(lane, c), q, ones)


ONES_ROWS = 16


def _transpose_values(v_ref, vt_ref, v_cols):
    nblk, n_sets, rows, tk = vt_ref.shape
    r = lax.broadcasted_iota(jnp.int32, (ONES_ROWS, tk), 0)
    tail = jnp.where(r == 0, 1.0, 0.0).astype(vt_ref.dtype)
    for blk in range(nblk):
        vt = v_ref[blk * tk:(blk + 1) * tk, :].astype(F32).T.astype(vt_ref.dtype)
        for s in range(n_sets):
            vt_ref[blk, s] = jnp.concatenate([vt[v_cols[s], :], tail], axis=0)


def _scores_t(ka, qa, q_lo):
    return lax.dot_general(ka, qa[q_lo:, :], NT_DIMS, preferred_element_type=F32)


def _block_t(idx, st, vt, tbl, q_lo, m_ref, acc_ref):
    at = idx + (slice(None), slice(q_lo, None))
    if tbl is not None:
        st = st + tbl
    m_prev = m_ref[at]
    m_new = jnp.maximum(m_prev, jnp.max(st, axis=0, keepdims=True))
    alpha = jnp.exp2(m_prev - m_new)
    pt = jnp.exp2(st - m_new).astype(BF16)
    acc_ref[at] = alpha * acc_ref[at] + jnp.dot(vt, pt, preferred_element_type=F32)
    m_ref[at] = m_new


def _sweep_t(qa, kaug_ref, vt_ref, tbl_ref, m_ref, acc_ref):
    nblk, n_sets, _, tk = vt_ref.shape
    nq = len(qa)
    assert qa[0][0].shape[0] == tk and tk % DIAG_TK == 0

    def step(qi, ki, lo, width, q_lo, tbl):
        rows = slice(ki * tk + lo, ki * tk + lo + width)
        st = [_scores_t(kaug_ref[c, rows, :], qa[qi][c], q_lo) for c in range(2)]
        for c in range(2):
            vt = vt_ref[ki, c % n_sets][:, lo:lo + width]
            _block_t((qi, c), st[c], vt, tbl, q_lo, m_ref, acc_ref)

    for ki in range(nblk):
        for lo in range(0, tk, DIAG_TK):
            step(ki, ki, lo, DIAG_TK, lo, tbl_ref[lo:lo + DIAG_TK, lo:])
        for qi in range(ki + 1, nq):
            step(qi, ki, 0, tk, 0, None)


def _augmented_query_tiles(q_ref, tq):
    return [[_augment_queries(q_ref[qi * tq:(qi + 1) * tq, :], c) for c in range(2)]
            for qi in range(q_ref.shape[0] // tq)]


def _diff_prompt_kernel(lambda_init, sl_ref, lq1, lk1, lq2, lk2, q_ref, k_ref, v_ref, g_ref,
                        o_ref, kaug_ref, vt_ref, m_ref, acc_ref, tbl_ref):
    nq, _, _, tq = m_ref.shape
    nblk, _, _, tk = vt_ref.shape
    hd2 = v_ref.shape[1]
    slope2 = sl_ref[pl.program_id(1)]
    for blk in range(nblk):
        rows = slice(blk * tk, (blk + 1) * tk)
        kpos = lax.broadcasted_iota(jnp.int32, (tk, LANES), 0) + blk * tk
        bias = slope2 * kpos.astype(F32)
        for c in range(2):
            kaug_ref[c, rows, :] = _augment_keys(k_ref[rows, :], bias, c)
    _transpose_values(v_ref, vt_ref, (slice(None),))
    dj = lax.broadcasted_iota(jnp.int32, (tk, tq), 0)
    i = lax.broadcasted_iota(jnp.int32, (tk, tq), 1)
    allowed = jnp.right_shift(dj, CHUNK_SHIFT) <= jnp.right_shift(i, CHUNK_SHIFT)
    ahead = jnp.maximum(dj - i, 0).astype(F32)
    tbl_ref[...] = jnp.where(allowed, (-2.0 * slope2) * ahead, NEG)

    m_ref[...] = jnp.full(m_ref.shape, NEG, F32)
    acc_ref[...] = jnp.zeros(acc_ref.shape, F32)
    _sweep_t(_augmented_query_tiles(q_ref, tq), kaug_ref, vt_ref, tbl_ref, m_ref, acc_ref)
    lam = _diff_lambda(lq1, lk1, lq2, lk2, lambda_init)
    for qi in range(nq):
        num = [acc_ref[qi, c, :hd2, :] / acc_ref[qi, c, hd2:hd2 + 1, :] for c in range(2)]
        ot = num[0] - lam * num[1]
        ms = jnp.mean(ot * ot, axis=0, keepdims=True)
        o = (ot * lax.rsqrt(ms + EPS)).T
        o_ref[qi * tq:(qi + 1) * tq, :] = ((o * g_ref[...]) * (1.0 - lambda_init)).astype(o_ref.dtype)


def _diff_prompt(qb, kb, vb, slopes2, lam_params, subln, n_batch, seq, lambda_init):
    d = qb.shape[1]
    tq = min(ATT_TQ, seq)
    tk = min(ATT_TK, tq)
    nq = seq // tq
    hd2 = 2 * HEAD_GROUP
    small = pl.BlockSpec((1, HEAD_GROUP), lambda b, h: (0, 0))
    whole_seq = pl.BlockSpec((seq, hd2), lambda b, h: (b, h))
    return pl.pallas_call(
        functools.partial(_diff_prompt_kernel, lambda_init),
        grid=(n_batch, DIFF_HEADS),
        in_specs=[pl.BlockSpec(memory_space=pltpu.SMEM), small, small, small, small,
                  whole_seq, whole_seq, whole_seq,
                  pl.BlockSpec((1, hd2), lambda b, h: (0, 0))],
        out_specs=whole_seq,
        out_shape=jax.ShapeDtypeStruct((n_batch * seq, d), BF16),
        scratch_shapes=[pltpu.VMEM((2, seq, hd2), BF16),
                        pltpu.VMEM((seq // tk, 1, hd2 + ONES_ROWS, tk), BF16),
                        pltpu.VMEM((nq, 2, 1, tq), F32),
                        pltpu.VMEM((nq, 2, hd2 + ONES_ROWS, tq), F32), pltpu.VMEM((tk, tq), F32)],
        compiler_params=_cparams(2),
        name="diff_attn_prompt",
    )(slopes2, *lam_params, qb, kb, vb, subln)


def _diff_sample_kernel(lambda_init, past, sl_ref, lq1, lk1, lq2, lk2, q_ref, kp_ref, vp_ref,
                        kn_ref, vn_ref, g_ref, o_ref, m_ref, l_ref, acc_ref):
    tq = q_ref.shape[0]
    n_heads = m_ref.shape[0]
    hd2 = kp_ref.shape[1]
    tk = kp_ref.shape[0] // n_heads
    step = pl.program_id(1)

    @pl.when(step == 0)
    def _():
        _init_state(m_ref, l_ref, acc_ref)

    lo, hi = _half_masks(tq)
    col = lax.broadcasted_iota(jnp.int32, (1, tk), 1) + (step * tk - past)

    def q_streams(h):
        q = q_ref[:, h * hd2:(h + 1) * hd2]
        zero = jnp.zeros_like(q)
        return jnp.where(lo, q, zero), jnp.where(hi, q, zero)

    for h in range(n_heads):
        k_h = kp_ref[pl.ds(h, tk, stride=n_heads), :].astype(BF16)
        v_h = vp_ref[pl.ds(h, tk, stride=n_heads), :].astype(BF16)
        bias = sl_ref[h] * col.astype(F32)
        for c, q_c in enumerate(q_streams(h)):
            s = lax.dot_general(q_c, k_h, NT_DIMS, preferred_element_type=F32) + bias
            _row_update((h, c), s, lambda p: jnp.dot(p, v_h, preferred_element_type=F32),
                        m_ref, l_ref, acc_ref)

    @pl.when(step == pl.num_programs(1) - 1)
    def _():
        lam = _diff_lambda(lq1, lk1, lq2, lk2, lambda_init)
        for h in range(n_heads):
            cols = slice(h * hd2, (h + 1) * hd2)
            k_h = kn_ref[:, cols]
            v_h = vn_ref[:, cols]
            bias = _alibi_table(sl_ref[h], tq, tq, past, past)
            for c, q_c in enumerate(q_streams(h)):
                s = lax.dot_general(q_c, k_h, NT_DIMS, preferred_element_type=F32) + bias
                _row_update((h, c), s, lambda p: jnp.dot(p, v_h, preferred_element_type=F32),
                            m_ref, l_ref, acc_ref)
            o = acc_ref[h, 0] / l_ref[h, 0] - lam * (acc_ref[h, 1] / l_ref[h, 1])
            ms = jnp.mean(o * o, axis=-1, keepdims=True)
            o_ref[:, cols] = ((o * lax.rsqrt(ms + EPS) * g_ref[...]) * (1.0 - lambda_init)).astype(o_ref.dtype)


def _diff_sample(qb, kb, vb, cache_k, cache_v, layer, slopes2, lam_params, subln, row0, n_batch, seq,
                 lambda_init):
    d = qb.shape[1]
    past, n_heads, hd2 = cache_k.shape[2:]
    tk = SAMPLE_TK if past % SAMPLE_TK == 0 else past
    blk0 = row0 // seq
    rows = lambda a: a.reshape(a.shape[:2] + (past * n_heads, hd2))
    cache_k, cache_v = rows(cache_k), rows(cache_v)
    small = pl.BlockSpec((1, HEAD_GROUP), lambda b, c: (0, 0))
    new = pl.BlockSpec((seq, d), lambda b, c: (blk0 + b, 0))
    old = pl.BlockSpec((None, None, tk * n_heads, hd2), lambda b, c: (layer, b, c, 0))
    return pl.pallas_call(
        functools.partial(_diff_sample_kernel, lambda_init, past),
        grid=(n_batch, past // tk),
        in_specs=[pl.BlockSpec(memory_space=pltpu.SMEM), small, small, small, small,
                  new, old, old, new, new, pl.BlockSpec((1, hd2), lambda b, c: (0, 0))],
        out_specs=pl.BlockSpec((seq, d), lambda b, c: (b, 0)),
        out_shape=jax.ShapeDtypeStruct((n_batch * seq, d), BF16),
        scratch_shapes=[pltpu.VMEM((n_heads, 2, seq, 1), F32), pltpu.VMEM((n_heads, 2, seq, 1), F32),
                        pltpu.VMEM((n_heads, 2, seq, hd2), F32)],
        compiler_params=_cparams(2),
        name="diff_attn_sample",
    )(slopes2, *lam_params, qb, cache_k, cache_v, kb, vb, subln)


def _causal_table(rows, cols):
    i = lax.broadcasted_iota(jnp.int32, (rows, cols), 0)
    j = lax.broadcasted_iota(jnp.int32, (rows, cols), 1)
    return jnp.where(j <= i, 0.0, NEG).astype(F32)


def _fox_prompt_kernel(q_ref, k_ref, v_ref, lf_ref, sg_ref, o_ref,
                       cum_ref, kaug_ref, vt_ref, m_ref, acc_ref, tbl_ref):
    nq, _, _, tq = m_ref.shape
    nblk, _, _, tk = vt_ref.shape
    pair = pl.program_id(1)

    @pl.when(pair == 0)
    def _():
        r = lax.broadcasted_iota(jnp.int32, (tk, tk), 0)
        c = lax.broadcasted_iota(jnp.int32, (tk, tk), 1)
        tril = jnp.where(c <= r, 1.0, 0.0).astype(BF16)
        carry = jnp.zeros((1, LANES), F32)
        for blk in range(nblk):
            x = lf_ref[blk * tk:(blk + 1) * tk, :]
            hi = x.astype(BF16)
            lo = (x - hi.astype(F32)).astype(BF16)
            part = (jnp.dot(tril, hi, preferred_element_type=F32)
                    + jnp.dot(tril, lo, preferred_element_type=F32)) + carry
            cum_ref[blk * tk:(blk + 1) * tk, :] = part * (-LOG2E)
            carry = carry + jnp.sum(x, axis=0, keepdims=True)

    lane = lax.broadcasted_iota(jnp.int32, (tk, LANES), 1)
    for blk in range(nblk):
        rows = slice(blk * tk, (blk + 1) * tk)
        cum = cum_ref[rows, :]
        for c in range(2):
            col = jnp.sum(jnp.where(lane == 2 * pair + c, cum, 0.0), axis=-1, keepdims=True)
            bias = jnp.broadcast_to(col, (tk, LANES))
            kaug_ref[c, rows, :] = _augment_keys(k_ref[rows, :], bias, c)
    hd = HEAD_GROUP
    _transpose_values(v_ref, vt_ref, (slice(0, hd), slice(hd, 2 * hd)))
    dj = lax.broadcasted_iota(jnp.int32, (tk, tq), 0)
    i = lax.broadcasted_iota(jnp.int32, (tk, tq), 1)
    tbl_ref[...] = jnp.where(dj <= i, 0.0, NEG).astype(F32)

    m_ref[...] = jnp.full(m_ref.shape, NEG, F32)
    acc_ref[...] = jnp.zeros(acc_ref.shape, F32)
    _sweep_t(_augmented_query_tiles(q_ref, tq), kaug_ref, vt_ref, tbl_ref, m_ref, acc_ref)
    for qi in range(nq):
        rows = slice(qi * tq, (qi + 1) * tq)
        ot = jnp.concatenate([acc_ref[qi, c, :hd, :] / acc_ref[qi, c, hd:hd + 1, :] for c in range(2)],
                             axis=0)
        o_ref[rows, :] = (ot.T * sg_ref[rows, :].astype(F32)).astype(o_ref.dtype)


def _fox_prompt(qb, kb, vb, lf, sg, n_batch, seq):
    d = qb.shape[1]
    tq = min(ATT_TQ, seq)
    tk = min(ATT_TK, tq)
    nq = seq // tq
    w = LANES
    whole_seq = pl.BlockSpec((seq, w), lambda b, p: (b, p))
    return pl.pallas_call(
        _fox_prompt_kernel,
        grid=(n_batch, d // w),
        in_specs=[whole_seq, whole_seq, whole_seq,
                  pl.BlockSpec((seq, LANES), lambda b, p: (b, 0)),
                  whole_seq],
        out_specs=whole_seq,
        out_shape=jax.ShapeDtypeStruct((n_batch * seq, d), BF16),
        scratch_shapes=[pltpu.VMEM((seq, LANES), F32),
                        pltpu.VMEM((2, seq, w), BF16),
                        pltpu.VMEM((seq // tk, 2, HEAD_GROUP + ONES_ROWS, tk), BF16),
                        pltpu.VMEM((nq, 2, 1, tq), F32),
                        pltpu.VMEM((nq, 2, HEAD_GROUP + ONES_ROWS, tq), F32), pltpu.VMEM((tk, tq), F32)],
        compiler_params=_cparams(2),
        name="fox_attn_prompt",
    )(qb, kb, vb, lf, sg)


def _fox_sample_kernel(q_ref, kp_ref, vp_ref, kn_ref, vn_ref, cp_ref, cn_ref, sg_ref,
                       o_ref, m_ref, l_ref, acc_ref):
    tq = q_ref.shape[0]
    n_heads, hd, tk = kp_ref.shape
    step = pl.program_id(1)

    @pl.when(step == 0)
    def _():
        _init_state(m_ref, l_ref, acc_ref)

    for h in range(n_heads):
        q_h = q_ref[:, h * hd:(h + 1) * hd]
        kt_h = kp_ref[h].astype(BF16)
        vt_h = vp_ref[h].astype(BF16)
        s = jnp.dot(q_h, kt_h, preferred_element_type=F32) + cp_ref[h:h + 1, :]
        _row_update(h, s, lambda p: lax.dot_general(p, vt_h, NT_DIMS, preferred_element_type=F32),
                    m_ref, l_ref, acc_ref)

    @pl.when(step == pl.num_programs(1) - 1)
    def _():
        tbl = _causal_table(tq, tq)
        for h in range(n_heads):
            cols = slice(h * hd, (h + 1) * hd)
            q_h = q_ref[:, cols]
            k_h = kn_ref[:, cols]
            v_h = vn_ref[:, cols]
            s = lax.dot_general(q_h, k_h, NT_DIMS, preferred_element_type=F32) + (tbl + cn_ref[h:h + 1, :])
            _row_update(h, s, lambda p: jnp.dot(p, v_h, preferred_element_type=F32),
                        m_ref, l_ref, acc_ref)
            o = acc_ref[h] / l_ref[h]
            o_ref[:, cols] = (o * sg_ref[:, cols].astype(F32)).astype(o_ref.dtype)


def _fox_sample(qb, kb, vb, cache_kt, cache_vt, layer, c_past, c_new, sg, row0, n_batch, seq):
    d = qb.shape[1]
    n_heads, hd, past = cache_kt.shape[2:]
    tk = SAMPLE_TK if past % SAMPLE_TK == 0 else past
    blk0 = row0 // seq
    new = pl.BlockSpec((seq, d), lambda b, c: (blk0 + b, 0))
    old = pl.BlockSpec((None, None, n_heads, hd, tk), lambda b, c: (layer, b, 0, 0, c))
    return pl.pallas_call(
        _fox_sample_kernel,
        grid=(n_batch, past // tk),
        in_specs=[new, old, old, new, new,
                  pl.BlockSpec((None, n_heads, tk), lambda b, c: (b, 0, c)),
                  pl.BlockSpec((None, n_heads, seq), lambda b, c: (b, 0, 0)),
                  new],
        out_specs=pl.BlockSpec((seq, d), lambda b, c: (b, 0)),
        out_shape=jax.ShapeDtypeStruct((n_batch * seq, d), BF16),
        scratch_shapes=[pltpu.VMEM((n_heads, seq, 1), F32), pltpu.VMEM((n_heads, seq, 1), F32),
                        pltpu.VMEM((n_heads, seq, hd), F32)],
        compiler_params=_cparams(2),
        name="fox_attn_sample",
    )(qb, cache_kt, cache_vt, kb, vb, c_past, c_new, sg)


def _split_dot(x, mat):
    hi = x.astype(BF16)
    lo = (x - hi.astype(F32)).astype(BF16)
    return (jnp.dot(hi, mat, preferred_element_type=F32) + jnp.dot(lo, mat, preferred_element_type=F32))


def _coff_kernel(suffix, tk, x_ref, o_ref):
    nblk = x_ref.shape[1] // tk
    r = lax.broadcasted_iota(jnp.int32, (tk, tk), 0)
    c = lax.broadcasted_iota(jnp.int32, (tk, tk), 1)
    mat = jnp.where((r > c) if suffix else (r <= c), 1.0, 0.0).astype(BF16)
    carry = jnp.zeros((x_ref.shape[0], 1), F32)
    order = range(nblk - 1, -1, -1) if suffix else range(nblk)
    for blk in order:
        cols = slice(blk * tk, (blk + 1) * tk)
        x = x_ref[:, cols]
        part = _split_dot(x, mat) + carry
        o_ref[:, cols] = (part if suffix else -part) * LOG2E
        carry = carry + jnp.sum(x, axis=-1, keepdims=True)


def _coff(xt, tk, suffix):
    n_batch, n_head, length = xt.shape
    spec = pl.BlockSpec((None, n_head, length), lambda b: (b, 0, 0))
    return pl.pallas_call(
        functools.partial(_coff_kernel, suffix, tk),
        grid=(n_batch,),
        in_specs=[spec],
        out_specs=spec,
        out_shape=jax.ShapeDtypeStruct((n_batch, n_head, length), F32),
        compiler_params=_cparams(1),
        name="fox_coff_suffix" if suffix else "fox_coff_prefix",
    )(xt)


def _outproj_kernel(tiles_p, op_ref, os_ref, x_ref, w_ref, gf_ref, wr_hi_ref, wr_lo_ref, br_ref,
                    xo_ref, meta_ref):
    o = jnp.where(pl.program_id(0) < tiles_p, op_ref[...], os_ref[...])
    xn = x_ref[...] + jnp.dot(o, w_ref[...], preferred_element_type=F32)
    xo_ref[...] = xn
    t = _rms_rows(xn, gf_ref[...])
    t_hi = t.astype(BF16)
    t_lo = (t - t_hi.astype(F32)).astype(BF16)
    logits = (jnp.dot(t_hi, wr_hi_ref[...], preferred_element_type=F32)
              + jnp.dot(t_lo, wr_hi_ref[...], preferred_element_type=F32)
              + jnp.dot(t_hi, wr_lo_ref[...], preferred_element_type=F32)) + br_ref[...]
    rows = logits.shape[0]
    lane = lax.broadcasted_iota(jnp.int32, (rows, LANES), 1).astype(F32)
    big = float(LANES)

    def first_argmax(vals, vmax):
        return jnp.min(jnp.where(vals == vmax, lane, big), axis=-1, keepdims=True)

    gl = jnp.where(lane < N_GROUPS, logits, NEG)
    gmax = jnp.max(gl, axis=-1, keepdims=True)
    gsum = jnp.sum(jnp.where(lane < N_GROUPS, jnp.exp(logits - gmax), 0.0), axis=-1, keepdims=True)
    p_top = 1.0 / gsum
    g_idx = first_argmax(gl, gmax)
    base = N_GROUPS + EXPERTS_PER_GROUP * g_idx
    el = jnp.where((lane >= base) & (lane < base + EXPERTS_PER_GROUP), logits, NEG)
    v1 = jnp.max(el, axis=-1, keepdims=True)
    i1 = first_argmax(el, v1)
    el2 = jnp.where(lane == i1, NEG, el)
    v2 = jnp.max(el2, axis=-1, keepdims=True)
    i2 = first_argmax(el2, v2)
    e2 = jnp.exp(v2 - v1)
    w1 = p_top / (1.0 + e2)
    w2 = p_top * e2 / (1.0 + e2)
    a1 = i1 - base
    a2 = i2 - base
    lo = jnp.minimum(a1, a2)
    hi = jnp.maximum(a1, a2)
    wa = jnp.where(a1 < a2, w1, w2)
    wb = jnp.where(a1 < a2, w2, w1)
    pair = jnp.where(lo == 0.0, hi - 1.0, jnp.where(lo == 1.0, hi + 1.0, 5.0))
    bucket = g_idx * N_PAIRS + pair
    meta_ref[...] = jnp.where(lane == 0.0, bucket, jnp.where(lane == 1.0, wa, jnp.where(lane == 2.0, wb, 0.0)))


def _outproj(o_p, o_s, x, w_all, layer, gf, wr_hi, wr_lo, br):
    n, d = x.shape
    tm = TOK_TILE
    tiles_p = o_p.shape[0] // tm
    row = lambda i: (i, 0)
    fixed = lambda i: (0, 0)
    big = pl.BlockSpec((tm, d), row)
    return pl.pallas_call(
        functools.partial(_outproj_kernel, tiles_p),
        grid=(n // tm,),
        in_specs=[pl.BlockSpec((tm, d), lambda i: (jnp.minimum(i, tiles_p - 1), 0)),
                  pl.BlockSpec((tm, d), lambda i: (jnp.maximum(i - tiles_p, 0), 0)),
                  big,
                  pl.BlockSpec((None, d, d), lambda i: (layer, 0, 0)),
                  pl.BlockSpec((1, d), fixed),
                  pl.BlockSpec((d, LANES), fixed), pl.BlockSpec((d, LANES), fixed),
                  pl.BlockSpec((1, LANES), fixed)],
        out_specs=[big, pl.BlockSpec((tm, LANES), row)],
        out_shape=[jax.ShapeDtypeStruct((n, d), F32), jax.ShapeDtypeStruct((n, LANES), F32)],
        compiler_params=_cparams(1),
        name="outproj_router",
    )(o_p, o_s, x, w_all, gf, wr_hi, wr_lo, br)


def _moe_kernel(tile_ref, ea_ref, eb_ref, lo_ref, hi_ref, first_ref,
                xs_ref, ms_ref, gf_ref, wgu_a, wd_a, wgu_b, wd_b, out_ref):
    s = pl.program_id(0)
    lo = lo_ref[s]
    hi = hi_ref[s]

    @pl.when(first_ref[s] == 1)
    def _():
        out_ref[...] = xs_ref[...]

    @pl.when(hi > lo)
    def _():
        t = _rms_rows(xs_ref[...], gf_ref[...]).astype(BF16)
        rows = t.shape[0]
        r = lax.broadcasted_iota(jnp.int32, (rows, 1), 0)
        inside = (r >= lo) & (r < hi)
        ms = ms_ref[...]
        wa = jnp.where(inside, ms[:, 1:2], 0.0)
        wb = jnp.where(inside, ms[:, 2:3], 0.0)

        def expert(wgu, wd):
            de = wd.shape[0]
            gu = jnp.dot(t, wgu[...], preferred_element_type=F32)
            g = gu[:, :de]
            hid = (g / (1.0 + jnp.exp(-g)) * gu[:, de:]).astype(BF16)
            return jnp.dot(hid, wd[...], preferred_element_type=F32)

        out_ref[...] += wa * expert(wgu_a, wd_a) + wb * expert(wgu_b, wd_b)


def _moe(sched, xs, ms, gf, wgu_all, wd_all, layer):
    n, d = xs.shape
    de = wd_all.shape[2]
    n_steps = sched[0].shape[0]
    tile_map = lambda s, tile, ea, eb, lo, hi, first: (tile[s], 0)
    fixed = lambda s, *_: (0, 0)
    wa_map = lambda s, tile, ea, eb, lo, hi, first: (layer, ea[s], 0, 0)
    wb_map = lambda s, tile, ea, eb, lo, hi, first: (layer, eb[s], 0, 0)
    grid_spec = pltpu.PrefetchScalarGridSpec(
        num_scalar_prefetch=6,
        grid=(n_steps,),
        in_specs=[pl.BlockSpec((MOE_TILE, d), tile_map),
                  pl.BlockSpec((MOE_TILE, ms.shape[1]), tile_map),
                  pl.BlockSpec((1, d), fixed),
                  pl.BlockSpec((None, None, d, 2 * de), wa_map),
                  pl.BlockSpec((None, None, de, d), wa_map),
                  pl.BlockSpec((None, None, d, 2 * de), wb_map),
                  pl.BlockSpec((None, None, de, d), wb_map)],
        out_specs=pl.BlockSpec((MOE_TILE, d), tile_map),
    )
    return pl.pallas_call(
        _moe_kernel,
        grid_spec=grid_spec,
        out_shape=jax.ShapeDtypeStruct((n, d), F32),
        compiler_params=_cparams(1),
        name="moe_experts",
    )(*sched, xs, ms, gf, wgu_all, wd_all, wgu_all, wd_all)


def _moe_schedule(bucket, n):
    n_tiles = n // MOE_TILE
    n_steps = n_tiles + N_BUCKETS - 1
    perm = jnp.argsort(bucket, stable=True).astype(jnp.int32)
    sorted_b = bucket[perm]
    member = (bucket[None, :] == jnp.arange(N_BUCKETS, dtype=jnp.int32)[:, None]).astype(jnp.int32)
    running = jnp.cumsum(member, axis=1)
    counts = running[:, -1]
    offs = jnp.concatenate([jnp.zeros((1,), jnp.int32), jnp.cumsum(counts)])
    inv = offs[bucket] + jnp.sum(running * member, axis=0) - 1
    fb = sorted_b[::MOE_TILE]
    lb = sorted_b[MOE_TILE - 1::MOE_TILE]
    per_tile = lb - fb + 1
    starts = jnp.cumsum(per_tile) - per_tile
    total = jnp.sum(per_tile)
    s = jnp.arange(n_steps, dtype=jnp.int32)
    tile = jnp.clip(jnp.searchsorted(starts, s, side="right").astype(jnp.int32) - 1, 0, n_tiles - 1)
    valid = s < total
    bkt = jnp.where(valid, fb[tile] + (s - starts[tile]), lb[n_tiles - 1])
    lo = jnp.clip(offs[bkt] - tile * MOE_TILE, 0, MOE_TILE)
    hi = jnp.clip(offs[bkt + 1] - tile * MOE_TILE, 0, MOE_TILE)
    lo = jnp.where(valid, lo, 0)
    hi = jnp.where(valid, hi, 0)
    first = (valid & (s == starts[tile])).astype(jnp.int32)
    grp = bkt // N_PAIRS
    ea = grp * EXPERTS_PER_GROUP + jnp.asarray(PAIR_A)[bkt % N_PAIRS]
    eb = grp * EXPERTS_PER_GROUP + jnp.asarray(PAIR_B)[bkt % N_PAIRS]
    return perm, inv, (tile, ea.astype(jnp.int32), eb.astype(jnp.int32), lo.astype(jnp.int32),
                       hi.astype(jnp.int32), first)


def _tile_gain(g, d):
    return jnp.tile(g.astype(F32), d // g.shape[0]).reshape(1, d)


def kernel(x_prompt, x_sample, cache_diff_k, cache_diff_v, cache_fox_k, cache_fox_v, cache_fox_logf, norm_mix, norm_ffn, diff_w_in, diff_w_out, diff_q_norm, diff_k_norm, diff_lambda_q1, diff_lambda_k1, diff_lambda_q2, diff_lambda_k2, diff_subln, fox_w_in, fox_b_f, fox_w_out, fox_q_norm, fox_k_norm, moe_w_group, moe_b_group, moe_w_expert, moe_b_expert, moe_w_gate, moe_w_up, moe_w_down):
    bp, sp, d = x_prompt.shape
    bs, ss, _ = x_sample.shape
    n_p = bp * sp
    n_s = bs * ss
    n = n_p + n_s
    depth = norm_mix.shape[0]
    past = cache_diff_k.shape[2]
    assert n % TOK_TILE == 0 and n % MOE_TILE == 0 and n_p % TOK_TILE == 0
    assert d % LANES == 0 and sp % min(ATT_TQ, sp) == 0 and n_p % ss == 0
    assert DIFF_HEADS * 2 * HEAD_GROUP == d and FOX_HEADS * HEAD_GROUP == d

    summ, expand = _group_mats(d)
    x = jnp.concatenate([x_prompt.reshape(n_p, d), x_sample.reshape(n_s, d)], axis=0)

    diff_w_in_b = diff_w_in.astype(BF16)
    diff_w_out_b = diff_w_out.astype(BF16)
    fox_w_main_b = fox_w_in[:, :, :4 * d].astype(BF16)
    fox_w_f_b = jnp.pad(fox_w_in[:, :, 4 * d:], ((0, 0), (0, 0), (0, LANES - FOX_HEADS))).astype(BF16)
    fox_w_out_b = fox_w_out.astype(BF16)
    wgu_b = jnp.concatenate([moe_w_gate, moe_w_up], axis=-1).astype(BF16)
    wd_b = moe_w_down.astype(BF16)
    w_router = jnp.pad(jnp.concatenate([moe_w_group, moe_w_expert], axis=-1),
                       ((0, 0), (0, 0), (0, LANES - N_GROUPS - N_EXPERTS)))
    wr_hi = w_router.astype(BF16)
    wr_lo = (w_router - wr_hi.astype(F32)).astype(BF16)
    b_router = jnp.pad(jnp.concatenate([moe_b_group, moe_b_expert], axis=-1),
                       ((0, 0), (0, LANES - N_GROUPS - N_EXPERTS)))
    slopes2 = jnp.asarray(2.0 ** (-8.0 * np.arange(1, DIFF_HEADS + 1) / DIFF_HEADS) * LOG2E, F32)
    cache_fkt = jnp.transpose(cache_fox_k, (0, 1, 3, 4, 2))
    cache_fvt = jnp.transpose(cache_fox_v, (0, 1, 3, 4, 2))
    cache_flt = jnp.transpose(cache_fox_logf.astype(F32), (0, 1, 3, 2))

    def from_feature_major(a, seq):
        return a.reshape(a.shape[0], FOX_HEADS, HEAD_GROUP, seq).transpose(0, 3, 1, 2)

    outs = {name: [] for name in ("dks", "dvs", "flp", "fks", "fvs", "fls")}
    n_diff, n_fox = diff_w_in.shape[0], fox_w_in.shape[0]
    diff_cache_p, fox_cache_p = (), ()
    for i in range(depth):
        j = i // 2
        gm = norm_mix[i].reshape(1, d)
        if i % 2 == 0:
            lambda_init = 0.8 - 0.6 * math.exp(-0.3 * i)
            qb, kb, vb, k_p, v_p, k_s, v_s = _inproj_diff(
                x, gm, diff_w_in_b, j, n_diff, _tile_gain(diff_q_norm[j], d), _tile_gain(diff_k_norm[j], d),
                summ, expand, n_p, diff_cache_p)
            diff_cache_p = (k_p, v_p)
            lam_params = [p[j].reshape(1, HEAD_GROUP).astype(F32) for p in
                          (diff_lambda_q1, diff_lambda_k1, diff_lambda_q2, diff_lambda_k2)]
            subln = diff_subln[j].reshape(1, 2 * HEAD_GROUP)
            o_p = _diff_prompt(qb, kb, vb, slopes2, lam_params, subln, bp, sp, lambda_init)
            o_s = _diff_sample(qb, kb, vb, cache_diff_k, cache_diff_v, j, slopes2, lam_params, subln,
                               n_p, bs, ss, lambda_init)
            w_out_b = diff_w_out_b
            outs["dks"].append(k_s.reshape(bs, ss, DIFF_HEADS, 2 * HEAD_GROUP))
            outs["dvs"].append(v_s.reshape(bs, ss, DIFF_HEADS, 2 * HEAD_GROUP))
        else:
            bf = jnp.pad(fox_b_f[j], (0, LANES - FOX_HEADS)).reshape(1, LANES)
            qb, kb, vb, sg, lf, kt_p, vt_p, lt_p, kt_s, vt_s, lt_s = _inproj_fox(
                x, gm, fox_w_main_b, fox_w_f_b, j, n_fox, bf, _tile_gain(fox_q_norm[j], d),
                _tile_gain(fox_k_norm[j], d), summ, expand, n_p, sp, ss, fox_cache_p)
            fox_cache_p = (kt_p, vt_p)
            tk_c = SAMPLE_TK if past % SAMPLE_TK == 0 else past
            c_new = _coff(lt_s, ss, False)
            c_past = _coff(cache_flt[j], tk_c, True)
            o_p = _fox_prompt(qb, kb, vb, lf, sg, bp, sp)
            o_s = _fox_sample(qb, kb, vb, cache_fkt, cache_fvt, j, c_past, c_new, sg, n_p, bs, ss)
            w_out_b = fox_w_out_b
            outs["flp"].append(lt_p.transpose(0, 2, 1))
            outs["fks"].append(from_feature_major(kt_s, ss))
            outs["fvs"].append(from_feature_major(vt_s, ss))
            outs["fls"].append(lt_s.transpose(0, 2, 1))
        gf = norm_ffn[i].reshape(1, d)
        x, meta = _outproj(o_p, o_s, x, w_out_b, j, gf, wr_hi[i], wr_lo[i], b_router[i].reshape(1, LANES))
        bucket = meta[:, 0].astype(jnp.int32)
        perm, inv, sched = _moe_schedule(bucket, n)
        xs = x[perm]
        ms = meta[:, :8][perm]
        ys = _moe(sched, xs, ms, gf, wgu_b, wd_b, i)
        x = ys[inv]

    y_prompt = x[:n_p].reshape(bp, sp, d)
    y_sample = x[n_p:].reshape(bs, ss, d)
    stk = lambda name: jnp.stack(outs[name])
    dkp, dvp = (a.reshape(n_diff, bp, sp, DIFF_HEADS, 2 * HEAD_GROUP) for a in diff_cache_p)
    fkp, fvp = (a.reshape(n_fox, bp, FOX_HEADS, HEAD_GROUP, sp).transpose(0, 1, 4, 2, 3)
                for a in fox_cache_p)
    return (y_prompt, y_sample, dkp, dvp, fkp, fvp, stk("flp"),
            stk("dks"), stk("dvs"), stk("fks"), stk("fvs"), stk("fls"))
```

```python
import functools
import math

import numpy as np
import jax
import jax.numpy as jnp
from jax import lax
from jax.experimental import pallas as pl
from jax.experimental.pallas import tpu as pltpu

F32 = jnp.float32
BF16 = jnp.bfloat16

CHUNK_SHIFT = 6
DIFF_HEADS = 8
FOX_HEADS = 16
HEAD_GROUP = 64
N_GROUPS = 4
EXPERTS_PER_GROUP = 4
N_EXPERTS = N_GROUPS * EXPERTS_PER_GROUP
N_PAIRS = 6
N_BUCKETS = N_GROUPS * N_PAIRS
EPS = 1e-6
NEG = -1e30
LOG2E = math.log2(math.e)
QSCALE = (HEAD_GROUP ** -0.5) * LOG2E

LANES = 128
TOK_TILE = 512
ATT_TQ = 512
ATT_TK = 512
DIAG_TK = 512
SAMPLE_TK = 1024
MOE_TILE = 256
VMEM_LIMIT = 56 * 1024 * 1024

PAIR_A = np.array([0, 0, 0, 1, 1, 2], np.int32)
PAIR_B = np.array([1, 2, 3, 2, 3, 3], np.int32)


def _cparams(n_axes):
    return pltpu.CompilerParams(dimension_semantics=("arbitrary",) * n_axes,
                                vmem_limit_bytes=VMEM_LIMIT)


def _rms_rows(x, gain):
    ms = jnp.mean(x * x, axis=-1, keepdims=True)
    return x * lax.rsqrt(ms + EPS) * gain


def _group_norm(z, gain, sum_ref, exp_ref):
    ms = jnp.dot((z * z).astype(BF16), sum_ref[...], preferred_element_type=F32)
    r = lax.rsqrt(ms + EPS)
    r_hi = r.astype(BF16)
    r_lo = (r - r_hi.astype(F32)).astype(BF16)
    rb = jnp.dot(jnp.concatenate([r_hi, r_lo], axis=-1), exp_ref[...], preferred_element_type=F32)
    return z * rb * gain


def _group_mats(d):
    n_g = d // HEAD_GROUP
    col = np.arange(d) // HEAD_GROUP
    summ = np.zeros((d, LANES), np.float32)
    summ[np.arange(d), col] = 1.0 / HEAD_GROUP
    expand = np.zeros((2 * LANES, d), np.float32)
    expand[col, np.arange(d)] = 1.0
    expand[LANES + col, np.arange(d)] = 1.0
    assert n_g <= LANES
    return jnp.asarray(summ, BF16), jnp.asarray(expand, BF16)


def _put_heads(ref, val):
    rows = val.shape[0]
    n_heads = ref.shape[0] // rows
    for h in range(n_heads):
        ref[pl.ds(h, rows, stride=n_heads), :] = val[:, h * LANES:(h + 1) * LANES]


def _inproj_diff_kernel(tiles_p, n_carried, x_ref, gm_ref, w_ref, qg_ref, kg_ref, sum_ref, exp_ref, *refs):
    qb_ref, kb_ref, vb_ref, kp_ref, vp_ref, ks_ref, vs_ref = refs[n_carried:]
    d = x_ref.shape[1]
    i = pl.program_id(0)
    h = _rms_rows(x_ref[...], gm_ref[...]).astype(BF16)
    q = jnp.dot(h, w_ref[:, 0:d], preferred_element_type=F32)
    k = jnp.dot(h, w_ref[:, d:2 * d], preferred_element_type=F32)
    v = jnp.dot(h, w_ref[:, 2 * d:3 * d], preferred_element_type=F32)
    qb_ref[...] = (_group_norm(q, qg_ref[...], sum_ref, exp_ref) * QSCALE).astype(BF16)
    kn = _group_norm(k, kg_ref[...], sum_ref, exp_ref)
    kb_ref[...] = kn.astype(BF16)
    vb_ref[...] = v.astype(BF16)

    @pl.when(i < tiles_p)
    def _():
        _put_heads(kp_ref, kn)
        _put_heads(vp_ref, v)

    @pl.when(i >= tiles_p)
    def _():
        _put_heads(ks_ref, kn)
        _put_heads(vs_ref, v)


def _carry_specs(carried, n_fixed_inputs, first_out):
    specs = [pl.BlockSpec(memory_space=pl.ANY)] * len(carried)
    aliases = {n_fixed_inputs + k: first_out + k for k in range(len(carried))}
    return specs, aliases


def _inproj_diff(x, gm, w_all, layer, n_layers, qg, kg, summ, expand, n_p, carried):
    n, d = x.shape
    tm = TOK_TILE
    tiles_p = n_p // tm
    row = lambda i: (i, 0)
    fixed = lambda i: (0, 0)
    big = pl.BlockSpec((tm, d), row)
    heads = pl.BlockSpec((None, tm * DIFF_HEADS, LANES), lambda i: (layer, jnp.minimum(i, tiles_p - 1), 0))
    heads_s = pl.BlockSpec((tm * DIFF_HEADS, LANES), lambda i: (jnp.maximum(i - tiles_p, 0), 0))
    cache_p = jax.ShapeDtypeStruct((n_layers, n_p * DIFF_HEADS, LANES), F32)
    cache_s = jax.ShapeDtypeStruct(((n - n_p) * DIFF_HEADS, LANES), F32)
    carry_specs, aliases = _carry_specs(carried, 7, 3)
    return pl.pallas_call(
        functools.partial(_inproj_diff_kernel, tiles_p, len(carried)),
        grid=(n // tm,),
        in_specs=[big,
                  pl.BlockSpec((1, d), fixed),
                  pl.BlockSpec((None, d, 3 * d), lambda i: (layer, 0, 0)),
                  pl.BlockSpec((1, d), fixed), pl.BlockSpec((1, d), fixed),
                  pl.BlockSpec(summ.shape, fixed), pl.BlockSpec(expand.shape, fixed)] + carry_specs,
        out_specs=[big, big, big, heads, heads, heads_s, heads_s],
        out_shape=[jax.ShapeDtypeStruct((n, d), BF16), jax.ShapeDtypeStruct((n, d), BF16),
                   jax.ShapeDtypeStruct((n, d), BF16), cache_p, cache_p, cache_s, cache_s],
        input_output_aliases=aliases,
        compiler_params=_cparams(1),
        name="inproj_diff",
    )(x, gm, w_all, qg, kg, summ, expand, *carried)


def _inproj_fox_kernel(tiles_p, n_carried, x_ref, gm_ref, w_ref, wf_ref, bf_ref, qg_ref, kg_ref,
                       sum_ref, exp_ref, *refs):
    (qb_ref, kb_ref, vb_ref, sg_ref, lf_ref,
     kp_ref, vp_ref, lp_ref, ks_ref, vs_ref, ls_ref) = refs[n_carried:]
    d = x_ref.shape[1]
    i = pl.program_id(0)
    h = _rms_rows(x_ref[...], gm_ref[...]).astype(BF16)
    q = jnp.dot(h, w_ref[:, 0:d], preferred_element_type=F32)
    qb_ref[...] = (_group_norm(q, qg_ref[...], sum_ref, exp_ref) * QSCALE).astype(BF16)
    k = jnp.dot(h, w_ref[:, d:2 * d], preferred_element_type=F32)
    kn = _group_norm(k, kg_ref[...], sum_ref, exp_ref)
    kb_ref[...] = kn.astype(BF16)
    v = jnp.dot(h, w_ref[:, 2 * d:3 * d], preferred_element_type=F32)
    vb_ref[...] = v.astype(BF16)
    g = jnp.dot(h, w_ref[:, 3 * d:4 * d], preferred_element_type=F32)
    sg_ref[...] = (1.0 / (1.0 + jnp.exp(-g))).astype(BF16)
    f = jnp.dot(h, wf_ref[...], preferred_element_type=F32) + bf_ref[...]
    lf = jnp.minimum(f, 0.0) - jnp.log(1.0 + jnp.exp(-jnp.abs(f)))
    lf_ref[...] = lf
    n_heads = lp_ref.shape[0]

    @pl.when(i < tiles_p)
    def _():
        kp_ref[...] = kn.T
        vp_ref[...] = v.T
        lp_ref[...] = lf.T[:n_heads, :]

    @pl.when(i >= tiles_p)
    def _():
        seq = ks_ref.shape[2]
        for bb in range(ks_ref.shape[0]):
            rows = slice(bb * seq, (bb + 1) * seq)
            ks_ref[bb] = kn[rows, :].T
            vs_ref[bb] = v[rows, :].T
            ls_ref[bb] = lf[rows, :].T[:n_heads, :]


def _inproj_fox(x, gm, w_all, wf_all, layer, n_layers, bf, qg, kg, summ, expand, n_p, seq_p, seq_s,
                carried):
    n, d = x.shape
    tm = TOK_TILE
    tiles_p = n_p // tm
    per_b = seq_p // tm
    grp = tm // seq_s
    row = lambda i: (i, 0)
    fixed = lambda i: (0, 0)
    big = pl.BlockSpec((tm, d), row)

    def p_map(i):
        ip = jnp.minimum(i, tiles_p - 1)
        return (ip // per_b, 0, ip % per_b)

    def stacked_map(i):
        return (layer,) + p_map(i)

    s_map = lambda i: (jnp.maximum(i - tiles_p, 0), 0, 0)
    f32 = lambda *shape: jax.ShapeDtypeStruct(shape, F32)
    bf16 = jax.ShapeDtypeStruct((n, d), BF16)
    carry_specs, aliases = _carry_specs(carried, 9, 5)
    return pl.pallas_call(
        functools.partial(_inproj_fox_kernel, tiles_p, len(carried)),
        grid=(n // tm,),
        in_specs=[big,
                  pl.BlockSpec((1, d), fixed),
                  pl.BlockSpec((None, d, 4 * d), lambda i: (layer, 0, 0)),
                  pl.BlockSpec((None, d, LANES), lambda i: (layer, 0, 0)),
                  pl.BlockSpec((1, LANES), fixed),
                  pl.BlockSpec((1, d), fixed), pl.BlockSpec((1, d), fixed),
                  pl.BlockSpec(summ.shape, fixed), pl.BlockSpec(expand.shape, fixed)] + carry_specs,
        out_specs=[big, big, big, big, pl.BlockSpec((tm, LANES), row),
                   pl.BlockSpec((None, None, d, tm), stacked_map),
                   pl.BlockSpec((None, None, d, tm), stacked_map),
                   pl.BlockSpec((None, FOX_HEADS, tm), p_map),
                   pl.BlockSpec((grp, d, seq_s), s_map), pl.BlockSpec((grp, d, seq_s), s_map),
                   pl.BlockSpec((grp, FOX_HEADS, seq_s), s_map)],
        out_shape=[bf16, bf16, bf16, bf16, f32(n, LANES),
                   f32(n_layers, n_p // seq_p, d, seq_p), f32(n_layers, n_p // seq_p, d, seq_p),
                   f32(n_p // seq_p, FOX_HEADS, seq_p),
                   f32((n - n_p) // seq_s, d, seq_s), f32((n - n_p) // seq_s, d, seq_s),
                   f32((n - n_p) // seq_s, FOX_HEADS, seq_s)],
        input_output_aliases=aliases,
        compiler_params=_cparams(1),
        name="inproj_fox",
    )(x, gm, w_all, wf_all, bf, qg, kg, summ, expand, *carried)


def _half_masks(rows):
    lane = lax.broadcasted_iota(jnp.int32, (rows, LANES), 1)
    return lane < HEAD_GROUP, lane >= HEAD_GROUP


NT_DIMS = (((1,), (1,)), ((), ()))


def _row_update(idx, s, pv, m_ref, l_ref, acc_ref):
    m_prev = m_ref[idx]
    m_new = jnp.maximum(m_prev, jnp.max(s, axis=-1, keepdims=True))
    alpha = jnp.exp2(m_prev - m_new)
    p = jnp.exp2(s - m_new)
    l_ref[idx] = alpha * l_ref[idx] + jnp.sum(p, axis=-1, keepdims=True)
    acc_ref[idx] = alpha * acc_ref[idx] + pv(p.astype(BF16))
    m_ref[idx] = m_new


def _init_state(m_ref, l_ref, acc_ref):
    m_ref[...] = jnp.full(m_ref.shape, NEG, F32)
    l_ref[...] = jnp.zeros(l_ref.shape, F32)
    acc_ref[...] = jnp.zeros(acc_ref.shape, F32)


def _diff_lambda(lq1, lk1, lq2, lk2, lambda_init):
    a = jnp.exp(jnp.sum(lq1[...] * lk1[...], axis=-1, keepdims=True))
    b = jnp.exp(jnp.sum(lq2[...] * lk2[...], axis=-1, keepdims=True))
    return a - b + lambda_init


def _alibi_table(slope2, rows, cols, q0, k0):
    i = lax.broadcasted_iota(jnp.int32, (rows, cols), 0)
    j = lax.broadcasted_iota(jnp.int32, (rows, cols), 1)
    qpos = i + q0
    kpos = j + k0
    allowed = jnp.right_shift(kpos, CHUNK_SHIFT) <= jnp.right_shift(qpos, CHUNK_SHIFT)
    bias = slope2 * (i - jnp.abs(qpos - kpos)).astype(F32)
    return jnp.where(allowed, bias, NEG)


BIAS_TERMS = 3


def _bias_lanes(c):
    return HEAD_GROUP * (1 - c)


def _own_lanes(lane, c):
    return (lane >= HEAD_GROUP * c) & (lane < HEAD_GROUP * (c + 1))


def _augment_keys(k_blk, bias, c):
    lane = lax.broadcasted_iota(jnp.int32, k_blk.shape, 1)
    b0 = _bias_lanes(c)
    out = jnp.where(_own_lanes(lane, c), k_blk, jnp.zeros_like(k_blk))
    rest = bias
    for t in range(BIAS_TERMS):
        term = rest.astype(BF16)
        out = jnp.where(lane == b0 + t, term, out)
        rest = rest - term.astype(F32)
    return out


def _augment_queries(q, c):
    lane = lax.broadcasted_iota(jnp.int32, q.shape, 1)
    b0 = _bias_lanes(c)
    ones = jnp.where((lane >= b0) & (lane < b0 + BIAS_TERMS), 1.0, 0.0).astype(q.dtype)
    return jnp.where(_own_lanes(lane, c), q, ones)


ONES_ROWS = 16


def _transpose_values(v_ref, vt_ref, v_cols):
    nblk, n_sets, rows, tk = vt_ref.shape
    r = lax.broadcasted_iota(jnp.int32, (ONES_ROWS, tk), 0)
    tail = jnp.where(r == 0, 1.0, 0.0).astype(vt_ref.dtype)
    for blk in range(nblk):
        vt = v_ref[blk * tk:(blk + 1) * tk, :].astype(F32).T.astype(vt_ref.dtype)
        for s in range(n_sets):
            vt_ref[blk, s] = jnp.concatenate([vt[v_cols[s], :], tail], axis=0)


def _scores_t(ka, qa, q_lo):
    return lax.dot_general(ka, qa[q_lo:, :], NT_DIMS, preferred_element_type=F32)


def _block_t(idx, st, vt, tbl, q_lo, m_ref, acc_ref):
    at = idx + (slice(None), slice(q_lo, None))
    if tbl is not None:
        st = st + tbl
    m_prev = m_ref[at]
    m_new = jnp.maximum(m_prev, jnp.max(st, axis=0, keepdims=True))
    alpha = jnp.exp2(m_prev - m_new)
    pt = jnp.exp2(st - m_new).astype(BF16)
    acc_ref[at] = alpha * acc_ref[at] + jnp.dot(vt, pt, preferred_element_type=F32)
    m_ref[at] = m_new


def _sweep_t(qa, kaug_ref, vt_ref, tbl_ref, m_ref, acc_ref):
    nblk, n_sets, _, tk = vt_ref.shape
    nq = len(qa)
    assert qa[0][0].shape[0] == tk and tk % DIAG_TK == 0

    def step(qi, ki, lo, width, q_lo, tbl):
        rows = slice(ki * tk + lo, ki * tk + lo + width)
        st = [_scores_t(kaug_ref[c, rows, :], qa[qi][c], q_lo) for c in range(2)]
        for c in range(2):
            vt = vt_ref[ki, c % n_sets][:, lo:lo + width]
            _block_t((qi, c), st[c], vt, tbl, q_lo, m_ref, acc_ref)

    for ki in range(nblk):
        for lo in range(0, tk, DIAG_TK):
            step(ki, ki, lo, DIAG_TK, lo, tbl_ref[lo:lo + DIAG_TK, lo:])
        for qi in range(ki + 1, nq):
            step(qi, ki, 0, tk, 0, None)


def _augmented_query_tiles(q_ref, tq):
    return [[_augment_queries(q_ref[qi * tq:(qi + 1) * tq, :], c) for c in range(2)]
            for qi in range(q_ref.shape[0] // tq)]


def _diff_prompt_kernel(lambda_init, sl_ref, lq1, lk1, lq2, lk2, q_ref, k_ref, v_ref, g_ref,
                        o_ref, kaug_ref, vt_ref, m_ref, acc_ref, tbl_ref):
    nq, _, _, tq = m_ref.shape
    nblk, _, _, tk = vt_ref.shape
    hd2 = v_ref.shape[1]
    slope2 = sl_ref[pl.program_id(1)]
    for blk in range(nblk):
        rows = slice(blk * tk, (blk + 1) * tk)
        kpos = lax.broadcasted_iota(jnp.int32, (tk, LANES), 0) + blk * tk
        bias = slope2 * kpos.astype(F32)
        for c in range(2):
            kaug_ref[c, rows, :] = _augment_keys(k_ref[rows, :], bias, c)
    _transpose_values(v_ref, vt_ref, (slice(None),))
    dj = lax.broadcasted_iota(jnp.int32, (tk, tq), 0)
    i = lax.broadcasted_iota(jnp.int32, (tk, tq), 1)
    allowed = jnp.right_shift(dj, CHUNK_SHIFT) <= jnp.right_shift(i, CHUNK_SHIFT)
    ahead = jnp.maximum(dj - i, 0).astype(F32)
    tbl_ref[...] = jnp.where(allowed, (-2.0 * slope2) * ahead, NEG)

    m_ref[...] = jnp.full(m_ref.shape, NEG, F32)
    acc_ref[...] = jnp.zeros(acc_ref.shape, F32)
    _sweep_t(_augmented_query_tiles(q_ref, tq), kaug_ref, vt_ref, tbl_ref, m_ref, acc_ref)
    lam = _diff_lambda(lq1, lk1, lq2, lk2, lambda_init)
    for qi in range(nq):
        num = [acc_ref[qi, c, :hd2, :] / acc_ref[qi, c, hd2:hd2 + 1, :] for c in range(2)]
        ot = num[0] - lam * num[1]
        ms = jnp.mean(ot * ot, axis=0, keepdims=True)
        o = (ot * lax.rsqrt(ms + EPS)).T
        o_ref[qi * tq:(qi + 1) * tq, :] = ((o * g_ref[...]) * (1.0 - lambda_init)).astype(o_ref.dtype)


def _diff_prompt(qb, kb, vb, slopes2, lam_params, subln, n_batch, seq, lambda_init):
    d = qb.shape[1]
    tq = min(ATT_TQ, seq)
    tk = min(ATT_TK, tq)
    nq = seq // tq
    hd2 = 2 * HEAD_GROUP
    small = pl.BlockSpec((1, HEAD_GROUP), lambda b, h: (0, 0))
    whole_seq = pl.BlockSpec((seq, hd2), lambda b, h: (b, h))
    return pl.pallas_call(
        functools.partial(_diff_prompt_kernel, lambda_init),
        grid=(n_batch, DIFF_HEADS),
        in_specs=[pl.BlockSpec(memory_space=pltpu.SMEM), small, small, small, small,
                  whole_seq, whole_seq, whole_seq,
                  pl.BlockSpec((1, hd2), lambda b, h: (0, 0))],
        out_specs=whole_seq,
        out_shape=jax.ShapeDtypeStruct((n_batch * seq, d), BF16),
        scratch_shapes=[pltpu.VMEM((2, seq, hd2), BF16),
                        pltpu.VMEM((seq // tk, 1, hd2 + ONES_ROWS, tk), BF16),
                        pltpu.VMEM((nq, 2, 1, tq), F32),
                        pltpu.VMEM((nq, 2, hd2 + ONES_ROWS, tq), F32), pltpu.VMEM((tk, tq), F32)],
        compiler_params=_cparams(2),
        name="diff_attn_prompt",
    )(slopes2, *lam_params, qb, kb, vb, subln)


def _diff_sample_kernel(lambda_init, past, sl_ref, lq1, lk1, lq2, lk2, q_ref, kp_ref, vp_ref,
                        kn_ref, vn_ref, g_ref, o_ref, m_ref, l_ref, acc_ref):
    tq = q_ref.shape[0]
    n_heads = m_ref.shape[0]
    hd2 = kp_ref.shape[1]
    tk = kp_ref.shape[0] // n_heads
    step = pl.program_id(1)

    @pl.when(step == 0)
    def _():
        _init_state(m_ref, l_ref, acc_ref)

    lo, hi = _half_masks(tq)
    col = lax.broadcasted_iota(jnp.int32, (1, tk), 1) + (step * tk - past)

    def q_streams(h):
        q = q_ref[:, h * hd2:(h + 1) * hd2]
        zero = jnp.zeros_like(q)
        return jnp.where(lo, q, zero), jnp.where(hi, q, zero)

    scores = {}
    for h in range(n_heads):
        k_h = kp_ref[pl.ds(h, tk, stride=n_heads), :].astype(BF16)
        bias = sl_ref[h] * col.astype(F32)
        for c, q_c in enumerate(q_streams(h)):
            scores[h, c] = lax.dot_general(q_c, k_h, NT_DIMS, preferred_element_type=F32) + bias
    for h in range(n_heads):
        v_h = vp_ref[pl.ds(h, tk, stride=n_heads), :].astype(BF16)
        for c in range(2):
            _row_update((h, c), scores[h, c], lambda p: jnp.dot(p, v_h, preferred_element_type=F32),
                        m_ref, l_ref, acc_ref)

    @pl.when(step == pl.num_programs(1) - 1)
    def _():
        lam = _diff_lambda(lq1, lk1, lq2, lk2, lambda_init)
        new_scores = {}
        for h in range(n_heads):
            k_h = kn_ref[:, h * hd2:(h + 1) * hd2]
            bias = _alibi_table(sl_ref[h], tq, tq, past, past)
            for c, q_c in enumerate(q_streams(h)):
                new_scores[h, c] = lax.dot_general(q_c, k_h, NT_DIMS, preferred_element_type=F32) + bias
        for h in range(n_heads):
            cols = slice(h * hd2, (h + 1) * hd2)
            v_h = vn_ref[:, cols]
            for c in range(2):
                _row_update((h, c), new_scores[h, c], lambda p: jnp.dot(p, v_h, preferred_element_type=F32),
                            m_ref, l_ref, acc_ref)
            o = acc_ref[h, 0] / l_ref[h, 0] - lam * (acc_ref[h, 1] / l_ref[h, 1])
            ms = jnp.mean(o * o, axis=-1, keepdims=True)
            o_ref[:, cols] = ((o * lax.rsqrt(ms + EPS) * g_ref[...]) * (1.0 - lambda_init)).astype(o_ref.dtype)


def _diff_sample(qb, kb, vb, cache_k, cache_v, layer, slopes2, lam_params, subln, row0, n_batch, seq,
                 lambda_init):
    d = qb.shape[1]
    past, n_heads, hd2 = cache_k.shape[2:]
    tk = SAMPLE_TK if past % SAMPLE_TK == 0 else past
    blk0 = row0 // seq
    rows = lambda a: a.reshape(a.shape[:2] + (past * n_heads, hd2))
    cache_k, cache_v = rows(cache_k), rows(cache_v)
    small = pl.BlockSpec((1, HEAD_GROUP), lambda b, c: (0, 0))
    new = pl.BlockSpec((seq, d), lambda b, c: (blk0 + b, 0))
    old = pl.BlockSpec((None, None, tk * n_heads, hd2), lambda b, c: (layer, b, c, 0))
    return pl.pallas_call(
        functools.partial(_diff_sample_kernel, lambda_init, past),
        grid=(n_batch, past // tk),
        in_specs=[pl.BlockSpec(memory_space=pltpu.SMEM), small, small, small, small,
                  new, old, old, new, new, pl.BlockSpec((1, hd2), lambda b, c: (0, 0))],
        out_specs=pl.BlockSpec((seq, d), lambda b, c: (b, 0)),
        out_shape=jax.ShapeDtypeStruct((n_batch * seq, d), BF16),
        scratch_shapes=[pltpu.VMEM((n_heads, 2, seq, 1), F32), pltpu.VMEM((n_heads, 2, seq, 1), F32),
                        pltpu.VMEM((n_heads, 2, seq, hd2), F32)],
        compiler_params=_cparams(2),
        name="diff_attn_sample",
    )(slopes2, *lam_params, qb, cache_k, cache_v, kb, vb, subln)


def _causal_table(rows, cols):
    i = lax.broadcasted_iota(jnp.int32, (rows, cols), 0)
    j = lax.broadcasted_iota(jnp.int32, (rows, cols), 1)
    return jnp.where(j <= i, 0.0, NEG).astype(F32)


def _fox_prompt_kernel(q_ref, k_ref, v_ref, lf_ref, sg_ref, o_ref,
                       cum_ref, kaug_ref, vt_ref, m_ref, acc_ref, tbl_ref):
    nq, _, _, tq = m_ref.shape
    nblk, _, _, tk = vt_ref.shape
    pair = pl.program_id(1)

    @pl.when(pair == 0)
    def _():
        r = lax.broadcasted_iota(jnp.int32, (tk, tk), 0)
        c = lax.broadcasted_iota(jnp.int32, (tk, tk), 1)
        tril = jnp.where(c <= r, 1.0, 0.0).astype(BF16)
        carry = jnp.zeros((1, LANES), F32)
        for blk in range(nblk):
            x = lf_ref[blk * tk:(blk + 1) * tk, :]
            hi = x.astype(BF16)
            lo = (x - hi.astype(F32)).astype(BF16)
            part = (jnp.dot(tril, hi, preferred_element_type=F32)
                    + jnp.dot(tril, lo, preferred_element_type=F32)) + carry
            cum_ref[blk * tk:(blk + 1) * tk, :] = part * (-LOG2E)
            carry = carry + jnp.sum(x, axis=0, keepdims=True)

    lane = lax.broadcasted_iota(jnp.int32, (tk, LANES), 1)
    for blk in range(nblk):
        rows = slice(blk * tk, (blk + 1) * tk)
        cum = cum_ref[rows, :]
        for c in range(2):
            col = jnp.sum(jnp.where(lane == 2 * pair + c, cum, 0.0), axis=-1, keepdims=True)
            bias = jnp.broadcast_to(col, (tk, LANES))
            kaug_ref[c, rows, :] = _augment_keys(k_ref[rows, :], bias, c)
    hd = HEAD_GROUP
    _transpose_values(v_ref, vt_ref, (slice(0, hd), slice(hd, 2 * hd)))
    dj = lax.broadcasted_iota(jnp.int32, (tk, tq), 0)
    i = lax.broadcasted_iota(jnp.int32, (tk, tq), 1)
    tbl_ref[...] = jnp.where(dj <= i, 0.0, NEG).astype(F32)

    m_ref[...] = jnp.full(m_ref.shape, NEG, F32)
    acc_ref[...] = jnp.zeros(acc_ref.shape, F32)
    _sweep_t(_augmented_query_tiles(q_ref, tq), kaug_ref, vt_ref, tbl_ref, m_ref, acc_ref)
    for qi in range(nq):
        rows = slice(qi * tq, (qi + 1) * tq)
        ot = jnp.concatenate([acc_ref[qi, c, :hd, :] / acc_ref[qi, c, hd:hd + 1, :] for c in range(2)],
                             axis=0)
        o_ref[rows, :] = (ot.T * sg_ref[rows, :].astype(F32)).astype(o_ref.dtype)


def _fox_prompt(qb, kb, vb, lf, sg, n_batch, seq):
    d = qb.shape[1]
    tq = min(ATT_TQ, seq)
    tk = min(ATT_TK, tq)
    nq = seq // tq
    w = LANES
    whole_seq = pl.BlockSpec((seq, w), lambda b, p: (b, p))
    return pl.pallas_call(
        _fox_prompt_kernel,
        grid=(n_batch, d // w),
        in_specs=[whole_seq, whole_seq, whole_seq,
                  pl.BlockSpec((seq, LANES), lambda b, p: (b, 0)),
                  whole_seq],
        out_specs=whole_seq,
        out_shape=jax.ShapeDtypeStruct((n_batch * seq, d), BF16),
        scratch_shapes=[pltpu.VMEM((seq, LANES), F32),
                        pltpu.VMEM((2, seq, w), BF16),
                        pltpu.VMEM((seq // tk, 2, HEAD_GROUP + ONES_ROWS, tk), BF16),
                        pltpu.VMEM((nq, 2, 1, tq), F32),
                        pltpu.VMEM((nq, 2, HEAD_GROUP + ONES_ROWS, tq), F32), pltpu.VMEM((tk, tq), F32)],
        compiler_params=_cparams(2),
        name="fox_attn_prompt",
    )(qb, kb, vb, lf, sg)


def _fox_sample_kernel(q_ref, kp_ref, vp_ref, kn_ref, vn_ref, cp_ref, cn_ref, sg_ref,
                       o_ref, m_ref, l_ref, acc_ref):
    tq = q_ref.shape[0]
    n_heads, hd, tk = kp_ref.shape
    step = pl.program_id(1)

    @pl.when(step == 0)
    def _():
        _init_state(m_ref, l_ref, acc_ref)

    scores = []
    for h in range(n_heads):
        q_h = q_ref[:, h * hd:(h + 1) * hd]
        kt_h = kp_ref[h].astype(BF16)
        scores.append(jnp.dot(q_h, kt_h, preferred_element_type=F32) + cp_ref[h:h + 1, :])
    for h in range(n_heads):
        vt_h = vp_ref[h].astype(BF16)
        _row_update(h, scores[h], lambda p: lax.dot_general(p, vt_h, NT_DIMS, preferred_element_type=F32),
                    m_ref, l_ref, acc_ref)

    @pl.when(step == pl.num_programs(1) - 1)
    def _():
        tbl = _causal_table(tq, tq)
        new_scores = []
        for h in range(n_heads):
            cols = slice(h * hd, (h + 1) * hd)
            new_scores.append(lax.dot_general(q_ref[:, cols], kn_ref[:, cols], NT_DIMS,
                                              preferred_element_type=F32) + (tbl + cn_ref[h:h + 1, :]))
        for h in range(n_heads):
            cols = slice(h * hd, (h + 1) * hd)
            v_h = vn_ref[:, cols]
            _row_update(h, new_scores[h], lambda p: jnp.dot(p, v_h, preferred_element_type=F32),
                        m_ref, l_ref, acc_ref)
            o = acc_ref[h] / l_ref[h]
            o_ref[:, cols] = (o * sg_ref[:, cols].astype(F32)).astype(o_ref.dtype)


def _fox_sample(qb, kb, vb, cache_kt, cache_vt, layer, c_past, c_new, sg, row0, n_batch, seq):
    d = qb.shape[1]
    n_heads, hd, past = cache_kt.shape[2:]
    tk = SAMPLE_TK if past % SAMPLE_TK == 0 else past
    blk0 = row0 // seq
    new = pl.BlockSpec((seq, d), lambda b, c: (blk0 + b, 0))
    old = pl.BlockSpec((None, None, n_heads, hd, tk), lambda b, c: (layer, b, 0, 0, c))
    return pl.pallas_call(
        _fox_sample_kernel,
        grid=(n_batch, past // tk),
        in_specs=[new, old, old, new, new,
                  pl.BlockSpec((None, n_heads, tk), lambda b, c: (b, 0, c)),
                  pl.BlockSpec((None, n_heads, seq), lambda b, c: (b, 0, 0)),
                  new],
        out_specs=pl.BlockSpec((seq, d), lambda b, c: (b, 0)),
        out_shape=jax.ShapeDtypeStruct((n_batch * seq, d), BF16),
        scratch_shapes=[pltpu.VMEM((n_heads, seq, 1), F32), pltpu.VMEM((n_heads, seq, 1), F32),
                        pltpu.VMEM((n_heads, seq, hd), F32)],
        compiler_params=_cparams(2),
        name="fox_attn_sample",
    )(qb, cache_kt, cache_vt, kb, vb, c_past, c_new, sg)


def _split_dot(x, mat):
    hi = x.astype(BF16)
    lo = (x - hi.astype(F32)).astype(BF16)
    return (jnp.dot(hi, mat, preferred_element_type=F32) + jnp.dot(lo, mat, preferred_element_type=F32))


def _coff_kernel(suffix, tk, x_ref, o_ref):
    nblk = x_ref.shape[1] // tk
    r = lax.broadcasted_iota(jnp.int32, (tk, tk), 0)
    c = lax.broadcasted_iota(jnp.int32, (tk, tk), 1)
    mat = jnp.where((r > c) if suffix else (r <= c), 1.0, 0.0).astype(BF16)
    carry = jnp.zeros((x_ref.shape[0], 1), F32)
    order = range(nblk - 1, -1, -1) if suffix else range(nblk)
    for blk in order:
        cols = slice(blk * tk, (blk + 1) * tk)
        x = x_ref[:, cols]
        part = _split_dot(x, mat) + carry
        o_ref[:, cols] = (part if suffix else -part) * LOG2E
        carry = carry + jnp.sum(x, axis=-1, keepdims=True)


def _coff(xt, tk, suffix):
    n_batch, n_head, length = xt.shape
    spec = pl.BlockSpec((None, n_head, length), lambda b: (b, 0, 0))
    return pl.pallas_call(
        functools.partial(_coff_kernel, suffix, tk),
        grid=(n_batch,),
        in_specs=[spec],
        out_specs=spec,
        out_shape=jax.ShapeDtypeStruct((n_batch, n_head, length), F32),
        compiler_params=_cparams(1),
        name="fox_coff_suffix" if suffix else "fox_coff_prefix",
    )(xt)


def _outproj_kernel(tiles_p, op_ref, os_ref, x_ref, w_ref, gf_ref, wr_hi_ref, wr_lo_ref, br_ref,
                    xo_ref, meta_ref):
    o = jnp.where(pl.program_id(0) < tiles_p, op_ref[...], os_ref[...])
    xn = x_ref[...] + jnp.dot(o, w_ref[...], preferred_element_type=F32)
    xo_ref[...] = xn
    t = _rms_rows(xn, gf_ref[...])
    t_hi = t.astype(BF16)
    t_lo = (t - t_hi.astype(F32)).astype(BF16)
    logits = (jnp.dot(t_hi, wr_hi_ref[...], preferred_element_type=F32)
              + jnp.dot(t_lo, wr_hi_ref[...], preferred_element_type=F32)
              + jnp.dot(t_hi, wr_lo_ref[...], preferred_element_type=F32)) + br_ref[...]
    rows = logits.shape[0]
    lane = lax.broadcasted_iota(jnp.int32, (rows, LANES), 1).astype(F32)
    big = float(LANES)

    def first_argmax(vals, vmax):
        return jnp.min(jnp.where(vals == vmax, lane, big), axis=-1, keepdims=True)

    gl = jnp.where(lane < N_GROUPS, logits, NEG)
    gmax = jnp.max(gl, axis=-1, keepdims=True)
    gsum = jnp.sum(jnp.where(lane < N_GROUPS, jnp.exp(logits - gmax), 0.0), axis=-1, keepdims=True)
    p_top = 1.0 / gsum
    g_idx = first_argmax(gl, gmax)
    base = N_GROUPS + EXPERTS_PER_GROUP * g_idx
    el = jnp.where((lane >= base) & (lane < base + EXPERTS_PER_GROUP), logits, NEG)
    v1 = jnp.max(el, axis=-1, keepdims=True)
    i1 = first_argmax(el, v1)
    el2 = jnp.where(lane == i1, NEG, el)
    v2 = jnp.max(el2, axis=-1, keepdims=True)
    i2 = first_argmax(el2, v2)
    e2 = jnp.exp(v2 - v1)
    w1 = p_top / (1.0 + e2)
    w2 = p_top * e2 / (1.0 + e2)
    a1 = i1 - base
    a2 = i2 - base
    lo = jnp.minimum(a1, a2)
    hi = jnp.maximum(a1, a2)
    wa = jnp.where(a1 < a2, w1, w2)
    wb = jnp.where(a1 < a2, w2, w1)
    pair = jnp.where(lo == 0.0, hi - 1.0, jnp.where(lo == 1.0, hi + 1.0, 5.0))
    bucket = g_idx * N_PAIRS + pair
    meta_ref[...] = jnp.where(lane == 0.0, bucket, jnp.where(lane == 1.0, wa, jnp.where(lane == 2.0, wb, 0.0)))


def _outproj(o_p, o_s, x, w_all, layer, gf, wr_hi, wr_lo, br):
    n, d = x.shape
    tm = TOK_TILE
    tiles_p = o_p.shape[0] // tm
    row = lambda i: (i, 0)
    fixed = lambda i: (0, 0)
    big = pl.BlockSpec((tm, d), row)
    return pl.pallas_call(
        functools.partial(_outproj_kernel, tiles_p),
        grid=(n // tm,),
        in_specs=[pl.BlockSpec((tm, d), lambda i: (jnp.minimum(i, tiles_p - 1), 0)),
                  pl.BlockSpec((tm, d), lambda i: (jnp.maximum(i - tiles_p, 0), 0)),
                  big,
                  pl.BlockSpec((None, d, d), lambda i: (layer, 0, 0)),
                  pl.BlockSpec((1, d), fixed),
                  pl.BlockSpec((d, LANES), fixed), pl.BlockSpec((d, LANES), fixed),
                  pl.BlockSpec((1, LANES), fixed)],
        out_specs=[big, pl.BlockSpec((tm, LANES), row)],
        out_shape=[jax.ShapeDtypeStruct((n, d), F32), jax.ShapeDtypeStruct((n, LANES), F32)],
        compiler_params=_cparams(1),
        name="outproj_router",
    )(o_p, o_s, x, w_all, gf, wr_hi, wr_lo, br)


def _moe_kernel(tile_ref, ea_ref, eb_ref, lo_ref, hi_ref, first_ref,
                xs_ref, ms_ref, gf_ref, wgu_a, wd_a, wgu_b, wd_b, out_ref):
    s = pl.program_id(0)
    lo = lo_ref[s]
    hi = hi_ref[s]

    @pl.when(first_ref[s] == 1)
    def _():
        out_ref[...] = xs_ref[...]

    @pl.when(hi > lo)
    def _():
        t = _rms_rows(xs_ref[...], gf_ref[...]).astype(BF16)
        rows = t.shape[0]
        r = lax.broadcasted_iota(jnp.int32, (rows, 1), 0)
        inside = (r >= lo) & (r < hi)
        ms = ms_ref[...]
        wa = jnp.where(inside, ms[:, 1:2], 0.0)
        wb = jnp.where(inside, ms[:, 2:3], 0.0)

        def hidden(gu):
            de = gu.shape[1] // 2
            g = gu[:, :de]
            return (g / (1.0 + jnp.exp(-g)) * gu[:, de:]).astype(BF16)

        gu_a = jnp.dot(t, wgu_a[...], preferred_element_type=F32)
        gu_b = jnp.dot(t, wgu_b[...], preferred_element_type=F32)
        y_a = jnp.dot(hidden(gu_a), wd_a[...], preferred_element_type=F32)
        y_b = jnp.dot(hidden(gu_b), wd_b[...], preferred_element_type=F32)
        out_ref[...] += wa * y_a + wb * y_b


def _moe(sched, xs, ms, gf, wgu_all, wd_all, layer):
    n, d = xs.shape
    de = wd_all.shape[2]
    n_steps = sched[0].shape[0]
    tile_map = lambda s, tile, ea, eb, lo, hi, first: (tile[s], 0)
    fixed = lambda s, *_: (0, 0)
    wa_map = lambda s, tile, ea, eb, lo, hi, first: (layer, ea[s], 0, 0)
    wb_map = lambda s, tile, ea, eb, lo, hi, first: (layer, eb[s], 0, 0)
    grid_spec = pltpu.PrefetchScalarGridSpec(
        num_scalar_prefetch=6,
        grid=(n_steps,),
        in_specs=[pl.BlockSpec((MOE_TILE, d), tile_map),
                  pl.BlockSpec((MOE_TILE, ms.shape[1]), tile_map),
                  pl.BlockSpec((1, d), fixed),
                  pl.BlockSpec((None, None, d, 2 * de), wa_map),
                  pl.BlockSpec((None, None, de, d), wa_map),
                  pl.BlockSpec((None, None, d, 2 * de), wb_map),
                  pl.BlockSpec((None, None, de, d), wb_map)],
        out_specs=pl.BlockSpec((MOE_TILE, d), tile_map),
    )
    return pl.pallas_call(
        _moe_kernel,
        grid_spec=grid_spec,
        out_shape=jax.ShapeDtypeStruct((n, d), F32),
        compiler_params=_cparams(1),
        name="moe_experts",
    )(*sched, xs, ms, gf, wgu_all, wd_all, wgu_all, wd_all)


def _moe_schedule(bucket, n):
    n_tiles = n // MOE_TILE
    n_steps = n_tiles + N_BUCKETS - 1
    perm = jnp.argsort(bucket, stable=True).astype(jnp.int32)
    sorted_b = bucket[perm]
    member = (bucket[None, :] == jnp.arange(N_BUCKETS, dtype=jnp.int32)[:, None]).astype(jnp.int32)
    running = jnp.cumsum(member, axis=1)
    counts = running[:, -1]
    offs = jnp.concatenate([jnp.zeros((1,), jnp.int32), jnp.cumsum(counts)])
    inv = offs[bucket] + jnp.sum(running * member, axis=0) - 1
    fb = sorted_b[::MOE_TILE]
    lb = sorted_b[MOE_TILE - 1::MOE_TILE]
    per_tile = lb - fb + 1
    starts = jnp.cumsum(per_tile) - per_tile
    total = jnp.sum(per_tile)
    s = jnp.arange(n_steps, dtype=jnp.int32)
    tile = jnp.clip(jnp.searchsorted(starts, s, side="right").astype(jnp.int32) - 1, 0, n_tiles - 1)
    valid = s < total
    bkt = jnp.where(valid, fb[tile] + (s - starts[tile]), lb[n_tiles - 1])
    lo = jnp.clip(offs[bkt] - tile * MOE_TILE, 0, MOE_TILE)
    hi = jnp.clip(offs[bkt + 1] - tile * MOE_TILE, 0, MOE_TILE)
    lo = jnp.where(valid, lo, 0)
    hi = jnp.where(valid, hi, 0)
    first = (valid & (s == starts[tile])).astype(jnp.int32)
    grp = bkt // N_PAIRS
    ea = grp * EXPERTS_PER_GROUP + jnp.asarray(PAIR_A)[bkt % N_PAIRS]
    eb = grp * EXPERTS_PER_GROUP + jnp.asarray(PAIR_B)[bkt % N_PAIRS]
    return perm, inv, (tile, ea.astype(jnp.int32), eb.astype(jnp.int32), lo.astype(jnp.int32),
                       hi.astype(jnp.int32), first)


def _tile_gain(g, d):
    return jnp.tile(g.astype(F32), d // g.shape[0]).reshape(1, d)


def kernel(x_prompt, x_sample, cache_diff_k, cache_diff_v, cache_fox_k, cache_fox_v, cache_fox_logf, norm_mix, norm_ffn, diff_w_in, diff_w_out, diff_q_norm, diff_k_norm, diff_lambda_q1, diff_lambda_k1, diff_lambda_q2, diff_lambda_k2, diff_subln, fox_w_in, fox_b_f, fox_w_out, fox_q_norm, fox_k_norm, moe_w_group, moe_b_group, moe_w_expert, moe_b_expert, moe_w_gate, moe_w_up, moe_w_down):
    bp, sp, d = x_prompt.shape
    bs, ss, _ = x_sample.shape
    n_p = bp * sp
    n_s = bs * ss
    n = n_p + n_s
    depth = norm_mix.shape[0]
    past = cache_diff_k.shape[2]
    assert n % TOK_TILE == 0 and n % MOE_TILE == 0 and n_p % TOK_TILE == 0
    assert d % LANES == 0 and sp % min(ATT_TQ, sp) == 0 and n_p % ss == 0
    assert DIFF_HEADS * 2 * HEAD_GROUP == d and FOX_HEADS * HEAD_GROUP == d

    summ, expand = _group_mats(d)
    x = jnp.concatenate([x_prompt.reshape(n_p, d), x_sample.reshape(n_s, d)], axis=0)

    diff_w_in_b = diff_w_in.astype(BF16)
    diff_w_out_b = diff_w_out.astype(BF16)
    fox_w_main_b = fox_w_in[:, :, :4 * d].astype(BF16)
    fox_w_f_b = jnp.pad(fox_w_in[:, :, 4 * d:], ((0, 0), (0, 0), (0, LANES - FOX_HEADS))).astype(BF16)
    fox_w_out_b = fox_w_out.astype(BF16)
    wgu_b = jnp.concatenate([moe_w_gate, moe_w_up], axis=-1).astype(BF16)
    wd_b = moe_w_down.astype(BF16)
    w_router = jnp.pad(jnp.concatenate([moe_w_group, moe_w_expert], axis=-1),
                       ((0, 0), (0, 0), (0, LANES - N_GROUPS - N_EXPERTS)))
    wr_hi = w_router.astype(BF16)
    wr_lo = (w_router - wr_hi.astype(F32)).astype(BF16)
    b_router = jnp.pad(jnp.concatenate([moe_b_group, moe_b_expert], axis=-1),
                       ((0, 0), (0, LANES - N_GROUPS - N_EXPERTS)))
    slopes2 = jnp.asarray(2.0 ** (-8.0 * np.arange(1, DIFF_HEADS + 1) / DIFF_HEADS) * LOG2E, F32)
    cache_fkt = jnp.transpose(cache_fox_k, (0, 1, 3, 4, 2))
    cache_fvt = jnp.transpose(cache_fox_v, (0, 1, 3, 4, 2))
    cache_flt = jnp.transpose(cache_fox_logf.astype(F32), (0, 1, 3, 2))

    def from_feature_major(a, seq):
        return a.reshape(a.shape[0], FOX_HEADS, HEAD_GROUP, seq).transpose(0, 3, 1, 2)

    outs = {name: [] for name in ("dks", "dvs", "flp", "fks", "fvs", "fls")}
    n_diff, n_fox = diff_w_in.shape[0], fox_w_in.shape[0]
    diff_cache_p, fox_cache_p = (), ()
    for i in range(depth):
        j = i // 2
        gm = norm_mix[i].reshape(1, d)
        if i % 2 == 0:
            lambda_init = 0.8 - 0.6 * math.exp(-0.3 * i)
            qb, kb, vb, k_p, v_p, k_s, v_s = _inproj_diff(
                x, gm, diff_w_in_b, j, n_diff, _tile_gain(diff_q_norm[j], d), _tile_gain(diff_k_norm[j], d),
                summ, expand, n_p, diff_cache_p)
            diff_cache_p = (k_p, v_p)
            lam_params = [p[j].reshape(1, HEAD_GROUP).astype(F32) for p in
                          (diff_lambda_q1, diff_lambda_k1, diff_lambda_q2, diff_lambda_k2)]
            subln = diff_subln[j].reshape(1, 2 * HEAD_GROUP)
            o_p = _diff_prompt(qb, kb, vb, slopes2, lam_params, subln, bp, sp, lambda_init)
            o_s = _diff_sample(qb, kb, vb, cache_diff_k, cache_diff_v, j, slopes2, lam_params, subln,
                               n_p, bs, ss, lambda_init)
            w_out_b = diff_w_out_b
            outs["dks"].append(k_s.reshape(bs, ss, DIFF_HEADS, 2 * HEAD_GROUP))
            outs["dvs"].append(v_s.reshape(bs, ss, DIFF_HEADS, 2 * HEAD_GROUP))
        else:
            bf = jnp.pad(fox_b_f[j], (0, LANES - FOX_HEADS)).reshape(1, LANES)
            qb, kb, vb, sg, lf, kt_p, vt_p, lt_p, kt_s, vt_s, lt_s = _inproj_fox(
                x, gm, fox_w_main_b, fox_w_f_b, j, n_fox, bf, _tile_gain(fox_q_norm[j], d),
                _tile_gain(fox_k_norm[j], d), summ, expand, n_p, sp, ss, fox_cache_p)
            fox_cache_p = (kt_p, vt_p)
            tk_c = SAMPLE_TK if past % SAMPLE_TK == 0 else past
            c_new = _coff(lt_s, ss, False)
            c_past = _coff(cache_flt[j], tk_c, True)
            o_p = _fox_prompt(qb, kb, vb, lf, sg, bp, sp)
            o_s = _fox_sample(qb, kb, vb, cache_fkt, cache_fvt, j, c_past, c_new, sg, n_p, bs, ss)
            w_out_b = fox_w_out_b
            outs["flp"].append(lt_p.transpose(0, 2, 1))
            outs["fks"].append(from_feature_major(kt_s, ss))
            outs["fvs"].append(from_feature_major(vt_s, ss))
            outs["fls"].append(lt_s.transpose(0, 2, 1))
        gf = norm_ffn[i].reshape(1, d)
        x, meta = _outproj(o_p, o_s, x, w_out_b, j, gf, wr_hi[i], wr_lo[i], b_router[i].reshape(1, LANES))
        bucket = meta[:, 0].astype(jnp.int32)
        perm, inv, sched = _moe_schedule(bucket, n)
        xs = x[perm]
        ms = meta[:, :8][perm]
        ys = _moe(sched, xs, ms, gf, wgu_b, wd_b, i)
        if i + 1 < depth:
            x = ys[inv]

    y_prompt = ys[inv[:n_p]].reshape(bp, sp, d)
    y_sample = ys[inv[n_p:]].reshape(bs, ss, d)
    stk = lambda name: jnp.stack(outs[name])
    dkp, dvp = (a.reshape(n_diff, bp, sp, DIFF_HEADS, 2 * HEAD_GROUP) for a in diff_cache_p)
    fkp, fvp = (a.reshape(n_fox, bp, FOX_HEADS, HEAD_GROUP, sp).transpose(0, 1, 4, 2, 3)
                for a in fox_cache_p)
    return (y_prompt, y_sample, dkp, dvp, fkp, fvp, stk("flp"),
            stk("dks"), stk("dvs"), stk("fks"), stk("fvs"), stk("fls"))
```

```python
import functools
import math

import numpy as np
import jax
import jax.numpy as jnp
from jax import lax
from jax.experimental import pallas as pl
from jax.experimental.pallas import tpu as pltpu

F32 = jnp.float32
BF16 = jnp.bfloat16

CHUNK_SHIFT = 6
DIFF_HEADS = 8
FOX_HEADS = 16
HEAD_GROUP = 64
N_GROUPS = 4
EXPERTS_PER_GROUP = 4
N_EXPERTS = N_GROUPS * EXPERTS_PER_GROUP
N_PAIRS = 6
N_BUCKETS = N_GROUPS * N_PAIRS
EPS = 1e-6
NEG = -1e30
LOG2E = math.log2(math.e)
QSCALE = (HEAD_GROUP ** -0.5) * LOG2E

LANES = 128
TOK_TILE = 512
ATT_TQ = 512
ATT_TK = 512
DIAG_TK = 512
SAMPLE_TK = 1024
MOE_TILE = 256
VMEM_LIMIT = 56 * 1024 * 1024

PAIR_A = np.array([0, 0, 0, 1, 1, 2], np.int32)
PAIR_B = np.array([1, 2, 3, 2, 3, 3], np.int32)


def _cparams(n_axes):
    return pltpu.CompilerParams(dimension_semantics=("arbitrary",) * n_axes,
                                vmem_limit_bytes=VMEM_LIMIT)


def _rms_rows(x, gain):
    ms = jnp.mean(x * x, axis=-1, keepdims=True)
    return x * lax.rsqrt(ms + EPS) * gain


def _group_norm(z, gain, sum_ref, exp_ref):
    ms = jnp.dot((z * z).astype(BF16), sum_ref[...], preferred_element_type=F32)
    r = lax.rsqrt(ms + EPS)
    r_hi = r.astype(BF16)
    r_lo = (r - r_hi.astype(F32)).astype(BF16)
    rb = jnp.dot(jnp.concatenate([r_hi, r_lo], axis=-1), exp_ref[...], preferred_element_type=F32)
    return z * rb * gain


def _group_mats(d):
    n_g = d // HEAD_GROUP
    col = np.arange(d) // HEAD_GROUP
    summ = np.zeros((d, LANES), np.float32)
    summ[np.arange(d), col] = 1.0 / HEAD_GROUP
    expand = np.zeros((2 * LANES, d), np.float32)
    expand[col, np.arange(d)] = 1.0
    expand[LANES + col, np.arange(d)] = 1.0
    assert n_g <= LANES
    return jnp.asarray(summ, BF16), jnp.asarray(expand, BF16)


def _put_heads(ref, val):
    rows = val.shape[0]
    n_heads = ref.shape[0] // rows
    for h in range(n_heads):
        ref[pl.ds(h, rows, stride=n_heads), :] = val[:, h * LANES:(h + 1) * LANES]


def _inproj_diff_kernel(tiles_p, n_carried, x_ref, gm_ref, w_ref, qg_ref, kg_ref, sum_ref, exp_ref, *refs):
    qb_ref, kb_ref, vb_ref, kp_ref, vp_ref, ks_ref, vs_ref = refs[n_carried:]
    d = x_ref.shape[1]
    i = pl.program_id(0)
    h = _rms_rows(x_ref[...], gm_ref[...]).astype(BF16)
    q = jnp.dot(h, w_ref[:, 0:d], preferred_element_type=F32)
    k = jnp.dot(h, w_ref[:, d:2 * d], preferred_element_type=F32)
    v = jnp.dot(h, w_ref[:, 2 * d:3 * d], preferred_element_type=F32)
    qb_ref[...] = (_group_norm(q, qg_ref[...], sum_ref, exp_ref) * QSCALE).astype(BF16)
    kn = _group_norm(k, kg_ref[...], sum_ref, exp_ref)
    kb_ref[...] = kn.astype(BF16)
    vb_ref[...] = v.astype(BF16)

    @pl.when(i < tiles_p)
    def _():
        _put_heads(kp_ref, kn)
        _put_heads(vp_ref, v)

    @pl.when(i >= tiles_p)
    def _():
        _put_heads(ks_ref, kn)
        _put_heads(vs_ref, v)


def _carry_specs(carried, n_fixed_inputs, first_out):
    specs = [pl.BlockSpec(memory_space=pl.ANY)] * len(carried)
    aliases = {n_fixed_inputs + k: first_out + k for k in range(len(carried))}
    return specs, aliases


def _inproj_diff(x, gm, w_all, layer, n_layers, qg, kg, summ, expand, n_p, carried):
    n, d = x.shape
    tm = TOK_TILE
    tiles_p = n_p // tm
    row = lambda i: (i, 0)
    fixed = lambda i: (0, 0)
    big = pl.BlockSpec((tm, d), row)
    heads = pl.BlockSpec((None, tm * DIFF_HEADS, LANES), lambda i: (layer, jnp.minimum(i, tiles_p - 1), 0))
    heads_s = pl.BlockSpec((tm * DIFF_HEADS, LANES), lambda i: (jnp.maximum(i - tiles_p, 0), 0))
    cache_p = jax.ShapeDtypeStruct((n_layers, n_p * DIFF_HEADS, LANES), F32)
    cache_s = jax.ShapeDtypeStruct(((n - n_p) * DIFF_HEADS, LANES), F32)
    carry_specs, aliases = _carry_specs(carried, 7, 3)
    return pl.pallas_call(
        functools.partial(_inproj_diff_kernel, tiles_p, len(carried)),
        grid=(n // tm,),
        in_specs=[big,
                  pl.BlockSpec((1, d), fixed),
                  pl.BlockSpec((None, d, 3 * d), lambda i: (layer, 0, 0)),
                  pl.BlockSpec((1, d), fixed), pl.BlockSpec((1, d), fixed),
                  pl.BlockSpec(summ.shape, fixed), pl.BlockSpec(expand.shape, fixed)] + carry_specs,
        out_specs=[big, big, big, heads, heads, heads_s, heads_s],
        out_shape=[jax.ShapeDtypeStruct((n, d), BF16), jax.ShapeDtypeStruct((n, d), BF16),
                   jax.ShapeDtypeStruct((n, d), BF16), cache_p, cache_p, cache_s, cache_s],
        input_output_aliases=aliases,
        compiler_params=_cparams(1),
        name="inproj_diff",
    )(x, gm, w_all, qg, kg, summ, expand, *carried)


def _inproj_fox_kernel(tiles_p, n_carried, x_ref, gm_ref, w_ref, wf_ref, bf_ref, qg_ref, kg_ref,
                       sum_ref, exp_ref, *refs):
    (qb_ref, kb_ref, vb_ref, sg_ref, lf_ref,
     kp_ref, vp_ref, lp_ref, ks_ref, vs_ref, ls_ref) = refs[n_carried:]
    d = x_ref.shape[1]
    i = pl.program_id(0)
    h = _rms_rows(x_ref[...], gm_ref[...]).astype(BF16)
    q = jnp.dot(h, w_ref[:, 0:d], preferred_element_type=F32)
    qb_ref[...] = (_group_norm(q, qg_ref[...], sum_ref, exp_ref) * QSCALE).astype(BF16)
    k = jnp.dot(h, w_ref[:, d:2 * d], preferred_element_type=F32)
    kn = _group_norm(k, kg_ref[...], sum_ref, exp_ref)
    kb_ref[...] = kn.astype(BF16)
    v = jnp.dot(h, w_ref[:, 2 * d:3 * d], preferred_element_type=F32)
    vb_ref[...] = v.astype(BF16)
    g = jnp.dot(h, w_ref[:, 3 * d:4 * d], preferred_element_type=F32)
    sg_ref[...] = (1.0 / (1.0 + jnp.exp(-g))).astype(BF16)
    f = jnp.dot(h, wf_ref[...], preferred_element_type=F32) + bf_ref[...]
    lf = jnp.minimum(f, 0.0) - jnp.log(1.0 + jnp.exp(-jnp.abs(f)))
    lf_ref[...] = lf
    n_heads = lp_ref.shape[0]

    @pl.when(i < tiles_p)
    def _():
        kp_ref[...] = kn.T
        vp_ref[...] = v.T
        lp_ref[...] = lf.T[:n_heads, :]

    @pl.when(i >= tiles_p)
    def _():
        seq = ks_ref.shape[2]
        for bb in range(ks_ref.shape[0]):
            rows = slice(bb * seq, (bb + 1) * seq)
            ks_ref[bb] = kn[rows, :].T
            vs_ref[bb] = v[rows, :].T
            ls_ref[bb] = lf[rows, :].T[:n_heads, :]


def _inproj_fox(x, gm, w_all, wf_all, layer, n_layers, bf, qg, kg, summ, expand, n_p, seq_p, seq_s,
                carried):
    n, d = x.shape
    tm = TOK_TILE
    tiles_p = n_p // tm
    per_b = seq_p // tm
    grp = tm // seq_s
    row = lambda i: (i, 0)
    fixed = lambda i: (0, 0)
    big = pl.BlockSpec((tm, d), row)

    def p_map(i):
        ip = jnp.minimum(i, tiles_p - 1)
        return (ip // per_b, 0, ip % per_b)

    def stacked_map(i):
        return (layer,) + p_map(i)

    s_map = lambda i: (jnp.maximum(i - tiles_p, 0), 0, 0)
    f32 = lambda *shape: jax.ShapeDtypeStruct(shape, F32)
    bf16 = jax.ShapeDtypeStruct((n, d), BF16)
    carry_specs, aliases = _carry_specs(carried, 9, 5)
    return pl.pallas_call(
        functools.partial(_inproj_fox_kernel, tiles_p, len(carried)),
        grid=(n // tm,),
        in_specs=[big,
                  pl.BlockSpec((1, d), fixed),
                  pl.BlockSpec((None, d, 4 * d), lambda i: (layer, 0, 0)),
                  pl.BlockSpec((None, d, LANES), lambda i: (layer, 0, 0)),
                  pl.BlockSpec((1, LANES), fixed),
                  pl.BlockSpec((1, d), fixed), pl.BlockSpec((1, d), fixed),
                  pl.BlockSpec(summ.shape, fixed), pl.BlockSpec(expand.shape, fixed)] + carry_specs,
        out_specs=[big, big, big, big, pl.BlockSpec((tm, LANES), row),
                   pl.BlockSpec((None, None, d, tm), stacked_map),
                   pl.BlockSpec((None, None, d, tm), stacked_map),
                   pl.BlockSpec((None, FOX_HEADS, tm), p_map),
                   pl.BlockSpec((grp, d, seq_s), s_map), pl.BlockSpec((grp, d, seq_s), s_map),
                   pl.BlockSpec((grp, FOX_HEADS, seq_s), s_map)],
        out_shape=[bf16, bf16, bf16, bf16, f32(n, LANES),
                   f32(n_layers, n_p // seq_p, d, seq_p), f32(n_layers, n_p // seq_p, d, seq_p),
                   f32(n_p // seq_p, FOX_HEADS, seq_p),
                   f32((n - n_p) // seq_s, d, seq_s), f32((n - n_p) // seq_s, d, seq_s),
                   f32((n - n_p) // seq_s, FOX_HEADS, seq_s)],
        input_output_aliases=aliases,
        compiler_params=_cparams(1),
        name="inproj_fox",
    )(x, gm, w_all, wf_all, bf, qg, kg, summ, expand, *carried)


def _half_masks(rows):
    lane = lax.broadcasted_iota(jnp.int32, (rows, LANES), 1)
    return lane < HEAD_GROUP, lane >= HEAD_GROUP


NT_DIMS = (((1,), (1,)), ((), ()))


def _row_update(idx, s, pv, m_ref, l_ref, acc_ref):
    m_prev = m_ref[idx]
    m_new = jnp.maximum(m_prev, jnp.max(s, axis=-1, keepdims=True))
    alpha = jnp.exp2(m_prev - m_new)
    p = jnp.exp2(s - m_new)
    l_ref[idx] = alpha * l_ref[idx] + jnp.sum(p, axis=-1, keepdims=True)
    acc_ref[idx] = alpha * acc_ref[idx] + pv(p.astype(BF16))
    m_ref[idx] = m_new


def _init_state(m_ref, l_ref, acc_ref):
    m_ref[...] = jnp.full(m_ref.shape, NEG, F32)
    l_ref[...] = jnp.zeros(l_ref.shape, F32)
    acc_ref[...] = jnp.zeros(acc_ref.shape, F32)


def _diff_lambda(lq1, lk1, lq2, lk2, lambda_init):
    a = jnp.exp(jnp.sum(lq1[...] * lk1[...], axis=-1, keepdims=True))
    b = jnp.exp(jnp.sum(lq2[...] * lk2[...], axis=-1, keepdims=True))
    return a - b + lambda_init


def _alibi_table(slope2, rows, cols, q0, k0):
    i = lax.broadcasted_iota(jnp.int32, (rows, cols), 0)
    j = lax.broadcasted_iota(jnp.int32, (rows, cols), 1)
    qpos = i + q0
    kpos = j + k0
    allowed = jnp.right_shift(kpos, CHUNK_SHIFT) <= jnp.right_shift(qpos, CHUNK_SHIFT)
    bias = slope2 * (i - jnp.abs(qpos - kpos)).astype(F32)
    return jnp.where(allowed, bias, NEG)


BIAS_TERMS = 3


def _bias_lanes(c):
    return HEAD_GROUP * (1 - c)


def _own_lanes(lane, c):
    return (lane >= HEAD_GROUP * c) & (lane < HEAD_GROUP * (c + 1))


def _augment_keys(k_blk, bias, c):
    lane = lax.broadcasted_iota(jnp.int32, k_blk.shape, 1)
    b0 = _bias_lanes(c)
    out = jnp.where(_own_lanes(lane, c), k_blk, jnp.zeros_like(k_blk))
    rest = bias
    for t in range(BIAS_TERMS):
        term = rest.astype(BF16)
        out = jnp.where(lane == b0 + t, term, out)
        rest = rest - term.astype(F32)
    return out


def _augment_queries(q, c):
    lane = lax.broadcasted_iota(jnp.int32, q.shape, 1)
    b0 = _bias_lanes(c)
    ones = jnp.where((lane >= b0) & (lane < b0 + BIAS_TERMS), 1.0, 0.0).astype(q.dtype)
    return jnp.where(_own_lanes(lane, c), q, ones)


ONES_ROWS = 16


def _transpose_values(v_ref, vt_ref, v_cols):
    nblk, n_sets, rows, tk = vt_ref.shape
    r = lax.broadcasted_iota(jnp.int32, (ONES_ROWS, tk), 0)
    tail = jnp.where(r == 0, 1.0, 0.0).astype(vt_ref.dtype)
    for blk in range(nblk):
        vt = v_ref[blk * tk:(blk + 1) * tk, :].astype(F32).T.astype(vt_ref.dtype)
        for s in range(n_sets):
            vt_ref[blk, s] = jnp.concatenate([vt[v_cols[s], :], tail], axis=0)


def _scores_t(ka, qa, q_lo):
    return lax.dot_general(ka, qa[q_lo:, :], NT_DIMS, preferred_element_type=F32)


def _block_t(idx, st, vt, tbl, q_lo, m_ref, acc_ref):
    at = idx + (slice(None), slice(q_lo, None))
    if tbl is not None:
        st = st + tbl
    m_prev = m_ref[at]
    m_new = jnp.maximum(m_prev, jnp.max(st, axis=0, keepdims=True))
    alpha = jnp.exp2(m_prev - m_new)
    pt = jnp.exp2(st - m_new).astype(BF16)
    acc_ref[at] = alpha * acc_ref[at] + jnp.dot(vt, pt, preferred_element_type=F32)
    m_ref[at] = m_new


def _sweep_t(qa, kaug_ref, vt_ref, tbl_ref, m_ref, acc_ref):
    nblk, n_sets, _, tk = vt_ref.shape
    nq = len(qa)
    assert qa[0][0].shape[0] == tk and tk % DIAG_TK == 0

    def step(qi, ki, lo, width, q_lo, tbl):
        rows = slice(ki * tk + lo, ki * tk + lo + width)
        st = [_scores_t(kaug_ref[c, rows, :], qa[qi][c], q_lo) for c in range(2)]
        for c in range(2):
            vt = vt_ref[ki, c % n_sets][:, lo:lo + width]
            _block_t((qi, c), st[c], vt, tbl, q_lo, m_ref, acc_ref)

    for ki in range(nblk):
        for lo in range(0, tk, DIAG_TK):
            step(ki, ki, lo, DIAG_TK, lo, tbl_ref[lo:lo + DIAG_TK, lo:])
        for qi in range(ki + 1, nq):
            step(qi, ki, 0, tk, 0, None)


def _augmented_query_tiles(q_ref, tq):
    return [[_augment_queries(q_ref[qi * tq:(qi + 1) * tq, :], c) for c in range(2)]
            for qi in range(q_ref.shape[0] // tq)]


def _diff_prompt_kernel(lambda_init, sl_ref, lq1, lk1, lq2, lk2, q_ref, k_ref, v_ref, g_ref,
                        o_ref, kaug_ref, vt_ref, m_ref, acc_ref, tbl_ref):
    nq, _, _, tq = m_ref.shape
    nblk, _, _, tk = vt_ref.shape
    hd2 = v_ref.shape[1]
    slope2 = sl_ref[pl.program_id(1)]
    for blk in range(nblk):
        rows = slice(blk * tk, (blk + 1) * tk)
        kpos = lax.broadcasted_iota(jnp.int32, (tk, LANES), 0) + blk * tk
        bias = slope2 * kpos.astype(F32)
        for c in range(2):
            kaug_ref[c, rows, :] = _augment_keys(k_ref[rows, :], bias, c)
    _transpose_values(v_ref, vt_ref, (slice(None),))
    dj = lax.broadcasted_iota(jnp.int32, (tk, tq), 0)
    i = lax.broadcasted_iota(jnp.int32, (tk, tq), 1)
    allowed = jnp.right_shift(dj, CHUNK_SHIFT) <= jnp.right_shift(i, CHUNK_SHIFT)
    ahead = jnp.maximum(dj - i, 0).astype(F32)
    tbl_ref[...] = jnp.where(allowed, (-2.0 * slope2) * ahead, NEG)

    m_ref[...] = jnp.full(m_ref.shape, NEG, F32)
    acc_ref[...] = jnp.zeros(acc_ref.shape, F32)
    _sweep_t(_augmented_query_tiles(q_ref, tq), kaug_ref, vt_ref, tbl_ref, m_ref, acc_ref)
    lam = _diff_lambda(lq1, lk1, lq2, lk2, lambda_init)
    for qi in range(nq):
        num = [acc_ref[qi, c, :hd2, :] / acc_ref[qi, c, hd2:hd2 + 1, :] for c in range(2)]
        ot = num[0] - lam * num[1]
        ms = jnp.mean(ot * ot, axis=0, keepdims=True)
        o = (ot * lax.rsqrt(ms + EPS)).T
        o_ref[qi * tq:(qi + 1) * tq, :] = ((o * g_ref[...]) * (1.0 - lambda_init)).astype(o_ref.dtype)


def _diff_prompt(qb, kb, vb, slopes2, lam_params, subln, n_batch, seq, lambda_init):
    d = qb.shape[1]
    tq = min(ATT_TQ, seq)
    tk = min(ATT_TK, tq)
    nq = seq // tq
    hd2 = 2 * HEAD_GROUP
    small = pl.BlockSpec((1, HEAD_GROUP), lambda b, h: (0, 0))
    whole_seq = pl.BlockSpec((seq, hd2), lambda b, h: (b, h))
    return pl.pallas_call(
        functools.partial(_diff_prompt_kernel, lambda_init),
        grid=(n_batch, DIFF_HEADS),
        in_specs=[pl.BlockSpec(memory_space=pltpu.SMEM), small, small, small, small,
                  whole_seq, whole_seq, whole_seq,
                  pl.BlockSpec((1, hd2), lambda b, h: (0, 0))],
        out_specs=whole_seq,
        out_shape=jax.ShapeDtypeStruct((n_batch * seq, d), BF16),
        scratch_shapes=[pltpu.VMEM((2, seq, hd2), BF16),
                        pltpu.VMEM((seq // tk, 1, hd2 + ONES_ROWS, tk), BF16),
                        pltpu.VMEM((nq, 2, 1, tq), F32),
                        pltpu.VMEM((nq, 2, hd2 + ONES_ROWS, tq), F32), pltpu.VMEM((tk, tq), F32)],
        compiler_params=_cparams(2),
        name="diff_attn_prompt",
    )(slopes2, *lam_params, qb, kb, vb, subln)


def _diff_sample_kernel(lambda_init, past, sl_ref, lq1, lk1, lq2, lk2, q_ref, kp_ref, vp_ref,
                        kn_ref, vn_ref, g_ref, o_ref, m_ref, l_ref, acc_ref):
    tq = q_ref.shape[0]
    n_heads = m_ref.shape[0]
    hd2 = kp_ref.shape[1]
    tk = kp_ref.shape[0] // n_heads
    step = pl.program_id(1)

    @pl.when(step == 0)
    def _():
        _init_state(m_ref, l_ref, acc_ref)

    lo, hi = _half_masks(tq)
    col = lax.broadcasted_iota(jnp.int32, (1, tk), 1) + (step * tk - past)

    def q_streams(h):
        q = q_ref[:, h * hd2:(h + 1) * hd2]
        zero = jnp.zeros_like(q)
        return jnp.where(lo, q, zero), jnp.where(hi, q, zero)

    scores = {}
    for h in range(n_heads):
        k_h = kp_ref[pl.ds(h, tk, stride=n_heads), :].astype(BF16)
        bias = sl_ref[h] * col.astype(F32)
        for c, q_c in enumerate(q_streams(h)):
            scores[h, c] = lax.dot_general(q_c, k_h, NT_DIMS, preferred_element_type=F32) + bias
    for h in range(n_heads):
        v_h = vp_ref[pl.ds(h, tk, stride=n_heads), :].astype(BF16)
        for c in range(2):
            _row_update((h, c), scores[h, c], lambda p: jnp.dot(p, v_h, preferred_element_type=F32),
                        m_ref, l_ref, acc_ref)

    @pl.when(step == pl.num_programs(1) - 1)
    def _():
        lam = _diff_lambda(lq1, lk1, lq2, lk2, lambda_init)
        new_scores = {}
        for h in range(n_heads):
            k_h = kn_ref[:, h * hd2:(h + 1) * hd2]
            bias = _alibi_table(sl_ref[h], tq, tq, past, past)
            for c, q_c in enumerate(q_streams(h)):
                new_scores[h, c] = lax.dot_general(q_c, k_h, NT_DIMS, preferred_element_type=F32) + bias
        for h in range(n_heads):
            cols = slice(h * hd2, (h + 1) * hd2)
            v_h = vn_ref[:, cols]
            for c in range(2):
                _row_update((h, c), new_scores[h, c], lambda p: jnp.dot(p, v_h, preferred_element_type=F32),
                            m_ref, l_ref, acc_ref)
            o = acc_ref[h, 0] / l_ref[h, 0] - lam * (acc_ref[h, 1] / l_ref[h, 1])
            ms = jnp.mean(o * o, axis=-1, keepdims=True)
            o_ref[:, cols] = ((o * lax.rsqrt(ms + EPS) * g_ref[...]) * (1.0 - lambda_init)).astype(o_ref.dtype)


def _diff_sample(qb, kb, vb, cache_k, cache_v, layer, slopes2, lam_params, subln, row0, n_batch, seq,
                 lambda_init):
    d = qb.shape[1]
    past, n_heads, hd2 = cache_k.shape[2:]
    tk = SAMPLE_TK if past % SAMPLE_TK == 0 else past
    blk0 = row0 // seq
    rows = lambda a: a.reshape(a.shape[:2] + (past * n_heads, hd2))
    cache_k, cache_v = rows(cache_k), rows(cache_v)
    small = pl.BlockSpec((1, HEAD_GROUP), lambda b, c: (0, 0))
    new = pl.BlockSpec((seq, d), lambda b, c: (blk0 + b, 0))
    old = pl.BlockSpec((None, None, tk * n_heads, hd2), lambda b, c: (layer, b, c, 0))
    return pl.pallas_call(
        functools.partial(_diff_sample_kernel, lambda_init, past),
        grid=(n_batch, past // tk),
        in_specs=[pl.BlockSpec(memory_space=pltpu.SMEM), small, small, small, small,
                  new, old, old, new, new, pl.BlockSpec((1, hd2), lambda b, c: (0, 0))],
        out_specs=pl.BlockSpec((seq, d), lambda b, c: (b, 0)),
        out_shape=jax.ShapeDtypeStruct((n_batch * seq, d), BF16),
        scratch_shapes=[pltpu.VMEM((n_heads, 2, seq, 1), F32), pltpu.VMEM((n_heads, 2, seq, 1), F32),
                        pltpu.VMEM((n_heads, 2, seq, hd2), F32)],
        compiler_params=_cparams(2),
        name="diff_attn_sample",
    )(slopes2, *lam_params, qb, cache_k, cache_v, kb, vb, subln)


def _causal_table(rows, cols):
    i = lax.broadcasted_iota(jnp.int32, (rows, cols), 0)
    j = lax.broadcasted_iota(jnp.int32, (rows, cols), 1)
    return jnp.where(j <= i, 0.0, NEG).astype(F32)


def _fox_prompt_kernel(q_ref, k_ref, v_ref, lf_ref, sg_ref, o_ref,
                       cum_ref, kaug_ref, vt_ref, m_ref, acc_ref, tbl_ref):
    nq, _, _, tq = m_ref.shape
    nblk, _, _, tk = vt_ref.shape
    pair = pl.program_id(1)

    @pl.when(pair == 0)
    def _():
        r = lax.broadcasted_iota(jnp.int32, (tk, tk), 0)
        c = lax.broadcasted_iota(jnp.int32, (tk, tk), 1)
        tril = jnp.where(c <= r, 1.0, 0.0).astype(BF16)
        carry = jnp.zeros((1, LANES), F32)
        for blk in range(nblk):
            x = lf_ref[blk * tk:(blk + 1) * tk, :]
            hi = x.astype(BF16)
            lo = (x - hi.astype(F32)).astype(BF16)
            part = (jnp.dot(tril, hi, preferred_element_type=F32)
                    + jnp.dot(tril, lo, preferred_element_type=F32)) + carry
            cum_ref[blk * tk:(blk + 1) * tk, :] = part * (-LOG2E)
            carry = carry + jnp.sum(x, axis=0, keepdims=True)

    lane = lax.broadcasted_iota(jnp.int32, (tk, LANES), 1)
    for blk in range(nblk):
        rows = slice(blk * tk, (blk + 1) * tk)
        cum = cum_ref[rows, :]
        for c in range(2):
            col = jnp.sum(jnp.where(lane == 2 * pair + c, cum, 0.0), axis=-1, keepdims=True)
            bias = jnp.broadcast_to(col, (tk, LANES))
            kaug_ref[c, rows, :] = _augment_keys(k_ref[rows, :], bias, c)
    hd = HEAD_GROUP
    _transpose_values(v_ref, vt_ref, (slice(0, hd), slice(hd, 2 * hd)))
    dj = lax.broadcasted_iota(jnp.int32, (tk, tq), 0)
    i = lax.broadcasted_iota(jnp.int32, (tk, tq), 1)
    tbl_ref[...] = jnp.where(dj <= i, 0.0, NEG).astype(F32)

    m_ref[...] = jnp.full(m_ref.shape, NEG, F32)
    acc_ref[...] = jnp.zeros(acc_ref.shape, F32)
    _sweep_t(_augmented_query_tiles(q_ref, tq), kaug_ref, vt_ref, tbl_ref, m_ref, acc_ref)
    for qi in range(nq):
        rows = slice(qi * tq, (qi + 1) * tq)
        ot = jnp.concatenate([acc_ref[qi, c, :hd, :] / acc_ref[qi, c, hd:hd + 1, :] for c in range(2)],
                             axis=0)
        o_ref[rows, :] = (ot.T * sg_ref[rows, :].astype(F32)).astype(o_ref.dtype)


def _fox_prompt(qb, kb, vb, lf, sg, n_batch, seq):
    d = qb.shape[1]
    tq = min(ATT_TQ, seq)
    tk = min(ATT_TK, tq)
    nq = seq // tq
    w = LANES
    whole_seq = pl.BlockSpec((seq, w), lambda b, p: (b, p))
    return pl.pallas_call(
        _fox_prompt_kernel,
        grid=(n_batch, d // w),
        in_specs=[whole_seq, whole_seq, whole_seq,
                  pl.BlockSpec((seq, LANES), lambda b, p: (b, 0)),
                  whole_seq],
        out_specs=whole_seq,
        out_shape=jax.ShapeDtypeStruct((n_batch * seq, d), BF16),
        scratch_shapes=[pltpu.VMEM((seq, LANES), F32),
                        pltpu.VMEM((2, seq, w), BF16),
                        pltpu.VMEM((seq // tk, 2, HEAD_GROUP + ONES_ROWS, tk), BF16),
                        pltpu.VMEM((nq, 2, 1, tq), F32),
                        pltpu.VMEM((nq, 2, HEAD_GROUP + ONES_ROWS, tq), F32), pltpu.VMEM((tk, tq), F32)],
        compiler_params=_cparams(2),
        name="fox_attn_prompt",
    )(qb, kb, vb, lf, sg)


def _fox_sample_kernel(q_ref, kp_ref, vp_ref, kn_ref, vn_ref, cp_ref, cn_ref, sg_ref,
                       o_ref, m_ref, l_ref, acc_ref):
    tq = q_ref.shape[0]
    n_heads, hd, tk = kp_ref.shape
    step = pl.program_id(1)

    @pl.when(step == 0)
    def _():
        _init_state(m_ref, l_ref, acc_ref)

    scores = []
    for h in range(n_heads):
        q_h = q_ref[:, h * hd:(h + 1) * hd]
        kt_h = kp_ref[h].astype(BF16)
        scores.append(jnp.dot(q_h, kt_h, preferred_element_type=F32) + cp_ref[h:h + 1, :])
    for h in range(n_heads):
        vt_h = vp_ref[h].astype(BF16)
        _row_update(h, scores[h], lambda p: lax.dot_general(p, vt_h, NT_DIMS, preferred_element_type=F32),
                    m_ref, l_ref, acc_ref)

    @pl.when(step == pl.num_programs(1) - 1)
    def _():
        tbl = _causal_table(tq, tq)
        new_scores = []
        for h in range(n_heads):
            cols = slice(h * hd, (h + 1) * hd)
            new_scores.append(lax.dot_general(q_ref[:, cols], kn_ref[:, cols], NT_DIMS,
                                              preferred_element_type=F32) + (tbl + cn_ref[h:h + 1, :]))
        for h in range(n_heads):
            cols = slice(h * hd, (h + 1) * hd)
            v_h = vn_ref[:, cols]
            _row_update(h, new_scores[h], lambda p: jnp.dot(p, v_h, preferred_element_type=F32),
                        m_ref, l_ref, acc_ref)
            o = acc_ref[h] / l_ref[h]
            o_ref[:, cols] = (o * sg_ref[:, cols].astype(F32)).astype(o_ref.dtype)


def _fox_sample(qb, kb, vb, cache_kt, cache_vt, layer, c_past, c_new, sg, row0, n_batch, seq):
    d = qb.shape[1]
    n_heads, hd, past = cache_kt.shape[2:]
    tk = SAMPLE_TK if past % SAMPLE_TK == 0 else past
    blk0 = row0 // seq
    new = pl.BlockSpec((seq, d), lambda b, c: (blk0 + b, 0))
    old = pl.BlockSpec((None, None, n_heads, hd, tk), lambda b, c: (layer, b, 0, 0, c))
    return pl.pallas_call(
        _fox_sample_kernel,
        grid=(n_batch, past // tk),
        in_specs=[new, old, old, new, new,
                  pl.BlockSpec((None, n_heads, tk), lambda b, c: (b, 0, c)),
                  pl.BlockSpec((None, n_heads, seq), lambda b, c: (b, 0, 0)),
                  new],
        out_specs=pl.BlockSpec((seq, d), lambda b, c: (b, 0)),
        out_shape=jax.ShapeDtypeStruct((n_batch * seq, d), BF16),
        scratch_shapes=[pltpu.VMEM((n_heads, seq, 1), F32), pltpu.VMEM((n_heads, seq, 1), F32),
                        pltpu.VMEM((n_heads, seq, hd), F32)],
        compiler_params=_cparams(2),
        name="fox_attn_sample",
    )(qb, cache_kt, cache_vt, kb, vb, c_past, c_new, sg)


def _split_dot(x, mat):
    hi = x.astype(BF16)
    lo = (x - hi.astype(F32)).astype(BF16)
    return (jnp.dot(hi, mat, preferred_element_type=F32) + jnp.dot(lo, mat, preferred_element_type=F32))


def _coff_kernel(suffix, tk, x_ref, o_ref):
    nblk = x_ref.shape[1] // tk
    r = lax.broadcasted_iota(jnp.int32, (tk, tk), 0)
    c = lax.broadcasted_iota(jnp.int32, (tk, tk), 1)
    mat = jnp.where((r > c) if suffix else (r <= c), 1.0, 0.0).astype(BF16)
    carry = jnp.zeros((x_ref.shape[0], 1), F32)
    order = range(nblk - 1, -1, -1) if suffix else range(nblk)
    for blk in order:
        cols = slice(blk * tk, (blk + 1) * tk)
        x = x_ref[:, cols]
        part = _split_dot(x, mat) + carry
        o_ref[:, cols] = (part if suffix else -part) * LOG2E
        carry = carry + jnp.sum(x, axis=-1, keepdims=True)


def _coff(xt, tk, suffix):
    n_batch, n_head, length = xt.shape
    spec = pl.BlockSpec((None, n_head, length), lambda b: (b, 0, 0))
    return pl.pallas_call(
        functools.partial(_coff_kernel, suffix, tk),
        grid=(n_batch,),
        in_specs=[spec],
        out_specs=spec,
        out_shape=jax.ShapeDtypeStruct((n_batch, n_head, length), F32),
        compiler_params=_cparams(1),
        name="fox_coff_suffix" if suffix else "fox_coff_prefix",
    )(xt)


def _outproj_kernel(tiles_p, op_ref, os_ref, x_ref, w_ref, gf_ref, wr_hi_ref, wr_lo_ref, br_ref,
                    xo_ref, meta_ref):
    o = jnp.where(pl.program_id(0) < tiles_p, op_ref[...], os_ref[...])
    xn = x_ref[...] + jnp.dot(o, w_ref[...], preferred_element_type=F32)
    xo_ref[...] = xn
    t = _rms_rows(xn, gf_ref[...])
    t_hi = t.astype(BF16)
    t_lo = (t - t_hi.astype(F32)).astype(BF16)
    logits = (jnp.dot(t_hi, wr_hi_ref[...], preferred_element_type=F32)
              + jnp.dot(t_lo, wr_hi_ref[...], preferred_element_type=F32)
              + jnp.dot(t_hi, wr_lo_ref[...], preferred_element_type=F32)) + br_ref[...]
    rows = logits.shape[0]
    lane = lax.broadcasted_iota(jnp.int32, (rows, LANES), 1).astype(F32)
    big = float(LANES)

    def first_argmax(vals, vmax):
        return jnp.min(jnp.where(vals == vmax, lane, big), axis=-1, keepdims=True)

    gl = jnp.where(lane < N_GROUPS, logits, NEG)
    gmax = jnp.max(gl, axis=-1, keepdims=True)
    gsum = jnp.sum(jnp.where(lane < N_GROUPS, jnp.exp(logits - gmax), 0.0), axis=-1, keepdims=True)
    p_top = 1.0 / gsum
    g_idx = first_argmax(gl, gmax)
    base = N_GROUPS + EXPERTS_PER_GROUP * g_idx
    el = jnp.where((lane >= base) & (lane < base + EXPERTS_PER_GROUP), logits, NEG)
    v1 = jnp.max(el, axis=-1, keepdims=True)
    i1 = first_argmax(el, v1)
    el2 = jnp.where(lane == i1, NEG, el)
    v2 = jnp.max(el2, axis=-1, keepdims=True)
    i2 = first_argmax(el2, v2)
    e2 = jnp.exp(v2 - v1)
    w1 = p_top / (1.0 + e2)
    w2 = p_top * e2 / (1.0 + e2)
    a1 = i1 - base
    a2 = i2 - base
    lo = jnp.minimum(a1, a2)
    hi = jnp.maximum(a1, a2)
    wa = jnp.where(a1 < a2, w1, w2)
    wb = jnp.where(a1 < a2, w2, w1)
    pair = jnp.where(lo == 0.0, hi - 1.0, jnp.where(lo == 1.0, hi + 1.0, 5.0))
    bucket = g_idx * N_PAIRS + pair
    meta_ref[...] = jnp.where(lane == 0.0, bucket, jnp.where(lane == 1.0, wa, jnp.where(lane == 2.0, wb, 0.0)))


def _outproj(o_p, o_s, x, w_all, layer, gf, wr_hi, wr_lo, br):
    n, d = x.shape
    tm = TOK_TILE
    tiles_p = o_p.shape[0] // tm
    row = lambda i: (i, 0)
    fixed = lambda i: (0, 0)
    big = pl.BlockSpec((tm, d), row)
    return pl.pallas_call(
        functools.partial(_outproj_kernel, tiles_p),
        grid=(n // tm,),
        in_specs=[pl.BlockSpec((tm, d), lambda i: (jnp.minimum(i, tiles_p - 1), 0)),
                  pl.BlockSpec((tm, d), lambda i: (jnp.maximum(i - tiles_p, 0), 0)),
                  big,
                  pl.BlockSpec((None, d, d), lambda i: (layer, 0, 0)),
                  pl.BlockSpec((1, d), fixed),
                  pl.BlockSpec((d, LANES), fixed), pl.BlockSpec((d, LANES), fixed),
                  pl.BlockSpec((1, LANES), fixed)],
        out_specs=[big, pl.BlockSpec((tm, LANES), row)],
        out_shape=[jax.ShapeDtypeStruct((n, d), F32), jax.ShapeDtypeStruct((n, LANES), F32)],
        compiler_params=_cparams(1),
        name="outproj_router",
    )(o_p, o_s, x, w_all, gf, wr_hi, wr_lo, br)


def _moe_kernel(tile_ref, ea_ref, eb_ref, lo_ref, hi_ref, first_ref,
                xs_ref, ms_ref, gf_ref, wgu_a, wd_a, wgu_b, wd_b, out_ref):
    s = pl.program_id(0)
    lo = lo_ref[s]
    hi = hi_ref[s]

    @pl.when(first_ref[s] == 1)
    def _():
        out_ref[...] = xs_ref[...]

    @pl.when(hi > lo)
    def _():
        t = _rms_rows(xs_ref[...], gf_ref[...]).astype(BF16)
        rows = t.shape[0]
        r = lax.broadcasted_iota(jnp.int32, (rows, 1), 0)
        inside = (r >= lo) & (r < hi)
        ms = ms_ref[...]
        wa = jnp.where(inside, ms[:, 1:2], 0.0)
        wb = jnp.where(inside, ms[:, 2:3], 0.0)

        def hidden(gu):
            de = gu.shape[1] // 2
            g = gu[:, :de]
            return (g / (1.0 + jnp.exp(-g)) * gu[:, de:]).astype(BF16)

        gu_a = jnp.dot(t, wgu_a[...], preferred_element_type=F32)
        gu_b = jnp.dot(t, wgu_b[...], preferred_element_type=F32)
        y_a = jnp.dot(hidden(gu_a), wd_a[...], preferred_element_type=F32)
        y_b = jnp.dot(hidden(gu_b), wd_b[...], preferred_element_type=F32)
        out_ref[...] += wa * y_a + wb * y_b


def _moe(sched, xs, ms, gf, wgu_all, wd_all, layer):
    n, d = xs.shape
    de = wd_all.shape[2]
    n_steps = sched[0].shape[0]
    tile_map = lambda s, tile, ea, eb, lo, hi, first: (tile[s], 0)
    fixed = lambda s, *_: (0, 0)
    wa_map = lambda s, tile, ea, eb, lo, hi, first: (layer, ea[s], 0, 0)
    wb_map = lambda s, tile, ea, eb, lo, hi, first: (layer, eb[s], 0, 0)
    grid_spec = pltpu.PrefetchScalarGridSpec(
        num_scalar_prefetch=6,
        grid=(n_steps,),
        in_specs=[pl.BlockSpec((MOE_TILE, d), tile_map),
                  pl.BlockSpec((MOE_TILE, ms.shape[1]), tile_map),
                  pl.BlockSpec((1, d), fixed),
                  pl.BlockSpec((None, None, d, 2 * de), wa_map),
                  pl.BlockSpec((None, None, de, d), wa_map),
                  pl.BlockSpec((None, None, d, 2 * de), wb_map),
                  pl.BlockSpec((None, None, de, d), wb_map)],
        out_specs=pl.BlockSpec((MOE_TILE, d), tile_map),
    )
    return pl.pallas_call(
        _moe_kernel,
        grid_spec=grid_spec,
        out_shape=jax.ShapeDtypeStruct((n, d), F32),
        compiler_params=_cparams(1),
        name="moe_experts",
    )(*sched, xs, ms, gf, wgu_all, wd_all, wgu_all, wd_all)


def _moe_schedule(bucket, n):
    n_tiles = n // MOE_TILE
    n_steps = n_tiles + N_BUCKETS - 1
    perm = jnp.argsort(bucket, stable=True).astype(jnp.int32)
    sorted_b = bucket[perm]
    inv = jnp.argsort(perm).astype(jnp.int32)
    offs = jnp.searchsorted(sorted_b, jnp.arange(N_BUCKETS + 1, dtype=jnp.int32), side="left")
    offs = offs.astype(jnp.int32)
    fb = sorted_b[::MOE_TILE]
    lb = sorted_b[MOE_TILE - 1::MOE_TILE]
    per_tile = lb - fb + 1
    starts = jnp.cumsum(per_tile) - per_tile
    total = jnp.sum(per_tile)
    s = jnp.arange(n_steps, dtype=jnp.int32)
    tile = jnp.clip(jnp.searchsorted(starts, s, side="right").astype(jnp.int32) - 1, 0, n_tiles - 1)
    valid = s < total
    bkt = jnp.where(valid, fb[tile] + (s - starts[tile]), lb[n_tiles - 1])
    lo = jnp.clip(offs[bkt] - tile * MOE_TILE, 0, MOE_TILE)
    hi = jnp.clip(offs[bkt + 1] - tile * MOE_TILE, 0, MOE_TILE)
    lo = jnp.where(valid, lo, 0)
    hi = jnp.where(valid, hi, 0)
    first = (valid & (s == starts[tile])).astype(jnp.int32)
    grp = bkt // N_PAIRS
    ea = grp * EXPERTS_PER_GROUP + jnp.asarray(PAIR_A)[bkt % N_PAIRS]
    eb = grp * EXPERTS_PER_GROUP + jnp.asarray(PAIR_B)[bkt % N_PAIRS]
    return perm, inv, (tile, ea.astype(jnp.int32), eb.astype(jnp.int32), lo.astype(jnp.int32),
                       hi.astype(jnp.int32), first)


def _tile_gain(g, d):
    return jnp.tile(g.astype(F32), d // g.shape[0]).reshape(1, d)


def kernel(x_prompt, x_sample, cache_diff_k, cache_diff_v, cache_fox_k, cache_fox_v, cache_fox_logf, norm_mix, norm_ffn, diff_w_in, diff_w_out, diff_q_norm, diff_k_norm, diff_lambda_q1, diff_lambda_k1, diff_lambda_q2, diff_lambda_k2, diff_subln, fox_w_in, fox_b_f, fox_w_out, fox_q_norm, fox_k_norm, moe_w_group, moe_b_group, moe_w_expert, moe_b_expert, moe_w_gate, moe_w_up, moe_w_down):
    bp, sp, d = x_prompt.shape
    bs, ss, _ = x_sample.shape
    n_p = bp * sp
    n_s = bs * ss
    n = n_p + n_s
    depth = norm_mix.shape[0]
    past = cache_diff_k.shape[2]
    assert n % TOK_TILE == 0 and n % MOE_TILE == 0 and n_p % TOK_TILE == 0
    assert d % LANES == 0 and sp % min(ATT_TQ, sp) == 0 and n_p % ss == 0
    assert DIFF_HEADS * 2 * HEAD_GROUP == d and FOX_HEADS * HEAD_GROUP == d

    summ, expand = _group_mats(d)
    x = jnp.concatenate([x_prompt.reshape(n_p, d), x_sample.reshape(n_s, d)], axis=0)

    diff_w_in_b = diff_w_in.astype(BF16)
    diff_w_out_b = diff_w_out.astype(BF16)
    fox_w_main_b = fox_w_in[:, :, :4 * d].astype(BF16)
    fox_w_f_b = jnp.pad(fox_w_in[:, :, 4 * d:], ((0, 0), (0, 0), (0, LANES - FOX_HEADS))).astype(BF16)
    fox_w_out_b = fox_w_out.astype(BF16)
    wgu_b = jnp.concatenate([moe_w_gate, moe_w_up], axis=-1).astype(BF16)
    wd_b = moe_w_down.astype(BF16)
    w_router = jnp.pad(jnp.concatenate([moe_w_group, moe_w_expert], axis=-1),
                       ((0, 0), (0, 0), (0, LANES - N_GROUPS - N_EXPERTS)))
    wr_hi = w_router.astype(BF16)
    wr_lo = (w_router - wr_hi.astype(F32)).astype(BF16)
    b_router = jnp.pad(jnp.concatenate([moe_b_group, moe_b_expert], axis=-1),
                       ((0, 0), (0, LANES - N_GROUPS - N_EXPERTS)))
    slopes2 = jnp.asarray(2.0 ** (-8.0 * np.arange(1, DIFF_HEADS + 1) / DIFF_HEADS) * LOG2E, F32)
    cache_fkt = jnp.transpose(cache_fox_k, (0, 1, 3, 4, 2))
    cache_fvt = jnp.transpose(cache_fox_v, (0, 1, 3, 4, 2))
    cache_flt = jnp.transpose(cache_fox_logf.astype(F32), (0, 1, 3, 2))

    def from_feature_major(a, seq):
        return a.reshape(a.shape[0], FOX_HEADS, HEAD_GROUP, seq).transpose(0, 3, 1, 2)

    outs = {name: [] for name in ("dks", "dvs", "flp", "fks", "fvs", "fls")}
    n_diff, n_fox = diff_w_in.shape[0], fox_w_in.shape[0]
    diff_cache_p = tuple(jnp.zeros((n_diff, n_p * DIFF_HEADS, LANES), F32) for _ in range(2))
    fox_cache_p = tuple(jnp.zeros((n_fox, bp, d, sp), F32) for _ in range(2))
    for i in range(depth):
        j = i // 2
        gm = norm_mix[i].reshape(1, d)
        if i % 2 == 0:
            lambda_init = 0.8 - 0.6 * math.exp(-0.3 * i)
            qb, kb, vb, k_p, v_p, k_s, v_s = _inproj_diff(
                x, gm, diff_w_in_b, j, n_diff, _tile_gain(diff_q_norm[j], d), _tile_gain(diff_k_norm[j], d),
                summ, expand, n_p, diff_cache_p)
            diff_cache_p = (k_p, v_p)
            lam_params = [p[j].reshape(1, HEAD_GROUP).astype(F32) for p in
                          (diff_lambda_q1, diff_lambda_k1, diff_lambda_q2, diff_lambda_k2)]
            subln = diff_subln[j].reshape(1, 2 * HEAD_GROUP)
            o_p = _diff_prompt(qb, kb, vb, slopes2, lam_params, subln, bp, sp, lambda_init)
            o_s = _diff_sample(qb, kb, vb, cache_diff_k, cache_diff_v, j, slopes2, lam_params, subln,
                               n_p, bs, ss, lambda_init)
            w_out_b = diff_w_out_b
            outs["dks"].append(k_s.reshape(bs, ss, DIFF_HEADS, 2 * HEAD_GROUP))
            outs["dvs"].append(v_s.reshape(bs, ss, DIFF_HEADS, 2 * HEAD_GROUP))
        else:
            bf = jnp.pad(fox_b_f[j], (0, LANES - FOX_HEADS)).reshape(1, LANES)
            qb, kb, vb, sg, lf, kt_p, vt_p, lt_p, kt_s, vt_s, lt_s = _inproj_fox(
                x, gm, fox_w_main_b, fox_w_f_b, j, n_fox, bf, _tile_gain(fox_q_norm[j], d),
                _tile_gain(fox_k_norm[j], d), summ, expand, n_p, sp, ss, fox_cache_p)
            fox_cache_p = (kt_p, vt_p)
            tk_c = SAMPLE_TK if past % SAMPLE_TK == 0 else past
            c_new = _coff(lt_s, ss, False)
            c_past = _coff(cache_flt[j], tk_c, True)
            o_p = _fox_prompt(qb, kb, vb, lf, sg, bp, sp)
            o_s = _fox_sample(qb, kb, vb, cache_fkt, cache_fvt, j, c_past, c_new, sg, n_p, bs, ss)
            w_out_b = fox_w_out_b
            outs["flp"].append(lt_p.transpose(0, 2, 1))
            outs["fks"].append(from_feature_major(kt_s, ss))
            outs["fvs"].append(from_feature_major(vt_s, ss))
            outs["fls"].append(lt_s.transpose(0, 2, 1))
        gf = norm_ffn[i].reshape(1, d)
        x, meta = _outproj(o_p, o_s, x, w_out_b, j, gf, wr_hi[i], wr_lo[i], b_router[i].reshape(1, LANES))
        bucket = meta[:, 0].astype(jnp.int32)
        perm, inv, sched = _moe_schedule(bucket, n)
        xs = x[perm]
        ms = meta[:, :8][perm]
        ys = _moe(sched, xs, ms, gf, wgu_b, wd_b, i)
        if i + 1 < depth:
            x = ys[inv]

    y_prompt = ys[inv[:n_p]].reshape(bp, sp, d)
    y_sample = ys[inv[n_p:]].reshape(bs, ss, d)
    stk = lambda name: jnp.stack(outs[name])
    dkp, dvp = (a.reshape(n_diff, bp, sp, DIFF_HEADS, 2 * HEAD_GROUP) for a in diff_cache_p)
    fkp, fvp = (a.reshape(n_fox, bp, FOX_HEADS, HEAD_GROUP, sp).transpose(0, 1, 4, 2, 3)
                for a in fox_cache_p)
    return (y_prompt, y_sample, dkp, dvp, fkp, fvp, stk("flp"),
            stk("dks"), stk("dvs"), stk("fks"), stk("fvs"), stk("fls"))
```

```python
import functools
import math

import numpy as np
import jax
import jax.numpy as jnp
from jax import lax
from jax.experimental import pallas as pl
from jax.experimental.pallas import tpu as pltpu

F32 = jnp.float32
BF16 = jnp.bfloat16

CHUNK_SHIFT = 6
DIFF_HEADS = 8
FOX_HEADS = 16
HEAD_GROUP = 64
N_GROUPS = 4
EXPERTS_PER_GROUP = 4
N_EXPERTS = N_GROUPS * EXPERTS_PER_GROUP
N_PAIRS = 6
N_BUCKETS = N_GROUPS * N_PAIRS
EPS = 1e-6
NEG = -1e30
LOG2E = math.log2(math.e)
QSCALE = (HEAD_GROUP ** -0.5) * LOG2E

LANES = 128
TOK_TILE = 512
ATT_TQ = 512
ATT_TK = 512
DIAG_TK = 512
SAMPLE_TK = 1024
MOE_TILE = 256
VMEM_LIMIT = 56 * 1024 * 1024

PAIR_A = np.array([0, 0, 0, 1, 1, 2], np.int32)
PAIR_B = np.array([1, 2, 3, 2, 3, 3], np.int32)


def _cparams(n_axes):
    return pltpu.CompilerParams(dimension_semantics=("arbitrary",) * n_axes,
                                vmem_limit_bytes=VMEM_LIMIT)


def _rms_rows(x, gain):
    ms = jnp.mean(x * x, axis=-1, keepdims=True)
    return x * lax.rsqrt(ms + EPS) * gain


def _group_norm(z, gain, sum_ref, exp_ref):
    ms = jnp.dot((z * z).astype(BF16), sum_ref[...], preferred_element_type=F32)
    r = lax.rsqrt(ms + EPS)
    r_hi = r.astype(BF16)
    r_lo = (r - r_hi.astype(F32)).astype(BF16)
    rb = jnp.dot(jnp.concatenate([r_hi, r_lo], axis=-1), exp_ref[...], preferred_element_type=F32)
    return z * rb * gain


def _group_mats(d):
    n_g = d // HEAD_GROUP
    col = np.arange(d) // HEAD_GROUP
    summ = np.zeros((d, LANES), np.float32)
    summ[np.arange(d), col] = 1.0 / HEAD_GROUP
    expand = np.zeros((2 * LANES, d), np.float32)
    expand[col, np.arange(d)] = 1.0
    expand[LANES + col, np.arange(d)] = 1.0
    assert n_g <= LANES
    return jnp.asarray(summ, BF16), jnp.asarray(expand, BF16)


def _put_heads(ref, val):
    rows = val.shape[0]
    n_heads = ref.shape[0] // rows
    for h in range(n_heads):
        ref[pl.ds(h, rows, stride=n_heads), :] = val[:, h * LANES:(h + 1) * LANES]


def _inproj_diff_kernel(tiles_p, n_carried, x_ref, gm_ref, w_ref, qg_ref, kg_ref, sum_ref, exp_ref, *refs):
    qb_ref, kb_ref, vb_ref, kp_ref, vp_ref, ks_ref, vs_ref = refs[n_carried:]
    d = x_ref.shape[1]
    i = pl.program_id(0)
    h = _rms_rows(x_ref[...], gm_ref[...]).astype(BF16)
    q = jnp.dot(h, w_ref[:, 0:d], preferred_element_type=F32)
    k = jnp.dot(h, w_ref[:, d:2 * d], preferred_element_type=F32)
    v = jnp.dot(h, w_ref[:, 2 * d:3 * d], preferred_element_type=F32)
    qb_ref[...] = (_group_norm(q, qg_ref[...], sum_ref, exp_ref) * QSCALE).astype(BF16)
    kn = _group_norm(k, kg_ref[...], sum_ref, exp_ref)
    kb_ref[...] = kn.astype(BF16)
    vb_ref[...] = v.astype(BF16)

    @pl.when(i < tiles_p)
    def _():
        _put_heads(kp_ref, kn)
        _put_heads(vp_ref, v)

    @pl.when(i >= tiles_p)
    def _():
        _put_heads(ks_ref, kn)
        _put_heads(vs_ref, v)


def _carry_specs(carried, n_fixed_inputs, first_out):
    specs = [pl.BlockSpec(memory_space=pl.ANY)] * len(carried)
    aliases = {n_fixed_inputs + k: first_out + k for k in range(len(carried))}
    return specs, aliases


def _inproj_diff(x, gm, w_all, layer, n_layers, qg, kg, summ, expand, n_p, carried):
    n, d = x.shape
    tm = TOK_TILE
    tiles_p = n_p // tm
    row = lambda i: (i, 0)
    fixed = lambda i: (0, 0)
    big = pl.BlockSpec((tm, d), row)
    heads = pl.BlockSpec((None, tm * DIFF_HEADS, LANES), lambda i: (layer, jnp.minimum(i, tiles_p - 1), 0))
    heads_s = pl.BlockSpec((tm * DIFF_HEADS, LANES), lambda i: (jnp.maximum(i - tiles_p, 0), 0))
    cache_p = jax.ShapeDtypeStruct((n_layers, n_p * DIFF_HEADS, LANES), F32)
    cache_s = jax.ShapeDtypeStruct(((n - n_p) * DIFF_HEADS, LANES), F32)
    carry_specs, aliases = _carry_specs(carried, 7, 3)
    return pl.pallas_call(
        functools.partial(_inproj_diff_kernel, tiles_p, len(carried)),
        grid=(n // tm,),
        in_specs=[big,
                  pl.BlockSpec((1, d), fixed),
                  pl.BlockSpec((None, d, 3 * d), lambda i: (layer, 0, 0)),
                  pl.BlockSpec((1, d), fixed), pl.BlockSpec((1, d), fixed),
                  pl.BlockSpec(summ.shape, fixed), pl.BlockSpec(expand.shape, fixed)] + carry_specs,
        out_specs=[big, big, big, heads, heads, heads_s, heads_s],
        out_shape=[jax.ShapeDtypeStruct((n, d), BF16), jax.ShapeDtypeStruct((n, d), BF16),
                   jax.ShapeDtypeStruct((n, d), BF16), cache_p, cache_p, cache_s, cache_s],
        input_output_aliases=aliases,
        compiler_params=_cparams(1),
        name="inproj_diff",
    )(x, gm, w_all, qg, kg, summ, expand, *carried)


def _inproj_fox_kernel(tiles_p, n_carried, x_ref, gm_ref, w_ref, wf_ref, bf_ref, qg_ref, kg_ref,
                       sum_ref, exp_ref, *refs):
    (qb_ref, kb_ref, vb_ref, sg_ref, lf_ref,
     kp_ref, vp_ref, lp_ref, ks_ref, vs_ref, ls_ref) = refs[n_carried:]
    d = x_ref.shape[1]
    i = pl.program_id(0)
    h = _rms_rows(x_ref[...], gm_ref[...]).astype(BF16)
    q = jnp.dot(h, w_ref[:, 0:d], preferred_element_type=F32)
    qb_ref[...] = (_group_norm(q, qg_ref[...], sum_ref, exp_ref) * QSCALE).astype(BF16)
    k = jnp.dot(h, w_ref[:, d:2 * d], preferred_element_type=F32)
    kn = _group_norm(k, kg_ref[...], sum_ref, exp_ref)
    kb_ref[...] = kn.astype(BF16)
    v = jnp.dot(h, w_ref[:, 2 * d:3 * d], preferred_element_type=F32)
    vb_ref[...] = v.astype(BF16)
    g = jnp.dot(h, w_ref[:, 3 * d:4 * d], preferred_element_type=F32)
    sg_ref[...] = (1.0 / (1.0 + jnp.exp(-g))).astype(BF16)
    f = jnp.dot(h, wf_ref[...], preferred_element_type=F32) + bf_ref[...]
    lf = jnp.minimum(f, 0.0) - jnp.log(1.0 + jnp.exp(-jnp.abs(f)))
    lf_ref[...] = lf
    n_heads = lp_ref.shape[0]

    @pl.when(i < tiles_p)
    def _():
        kp_ref[...] = kn.T
        vp_ref[...] = v.T
        lp_ref[...] = lf.T[:n_heads, :]

    @pl.when(i >= tiles_p)
    def _():
        seq = ks_ref.shape[2]
        for bb in range(ks_ref.shape[0]):
            rows = slice(bb * seq, (bb + 1) * seq)
            ks_ref[bb] = kn[rows, :].T
            vs_ref[bb] = v[rows, :].T
            ls_ref[bb] = lf[rows, :].T[:n_heads, :]


def _inproj_fox(x, gm, w_all, wf_all, layer, n_layers, bf, qg, kg, summ, expand, n_p, seq_p, seq_s,
                carried):
    n, d = x.shape
    tm = TOK_TILE
    tiles_p = n_p // tm
    per_b = seq_p // tm
    grp = tm // seq_s
    row = lambda i: (i, 0)
    fixed = lambda i: (0, 0)
    big = pl.BlockSpec((tm, d), row)

    def p_map(i):
        ip = jnp.minimum(i, tiles_p - 1)
        return (ip // per_b, 0, ip % per_b)

    def stacked_map(i):
        return (layer,) + p_map(i)

    s_map = lambda i: (jnp.maximum(i - tiles_p, 0), 0, 0)
    f32 = lambda *shape: jax.ShapeDtypeStruct(shape, F32)
    bf16 = jax.ShapeDtypeStruct((n, d), BF16)
    carry_specs, aliases = _carry_specs(carried, 9, 5)
    return pl.pallas_call(
        functools.partial(_inproj_fox_kernel, tiles_p, len(carried)),
        grid=(n // tm,),
        in_specs=[big,
                  pl.BlockSpec((1, d), fixed),
                  pl.BlockSpec((None, d, 4 * d), lambda i: (layer, 0, 0)),
                  pl.BlockSpec((None, d, LANES), lambda i: (layer, 0, 0)),
                  pl.BlockSpec((1, LANES), fixed),
                  pl.BlockSpec((1, d), fixed), pl.BlockSpec((1, d), fixed),
                  pl.BlockSpec(summ.shape, fixed), pl.BlockSpec(expand.shape, fixed)] + carry_specs,
        out_specs=[big, big, big, big, pl.BlockSpec((tm, LANES), row),
                   pl.BlockSpec((None, None, d, tm), stacked_map),
                   pl.BlockSpec((None, None, d, tm), stacked_map),
                   pl.BlockSpec((None, FOX_HEADS, tm), p_map),
                   pl.BlockSpec((grp, d, seq_s), s_map), pl.BlockSpec((grp, d, seq_s), s_map),
                   pl.BlockSpec((grp, FOX_HEADS, seq_s), s_map)],
        out_shape=[bf16, bf16, bf16, bf16, f32(n, LANES),
                   f32(n_layers, n_p // seq_p, d, seq_p), f32(n_layers, n_p // seq_p, d, seq_p),
                   f32(n_p // seq_p, FOX_HEADS, seq_p),
                   f32((n - n_p) // seq_s, d, seq_s), f32((n - n_p) // seq_s, d, seq_s),
                   f32((n - n_p) // seq_s, FOX_HEADS, seq_s)],
        input_output_aliases=aliases,
        compiler_params=_cparams(1),
        name="inproj_fox",
    )(x, gm, w_all, wf_all, bf, qg, kg, summ, expand, *carried)


def _half_masks(rows):
    lane = lax.broadcasted_iota(jnp.int32, (rows, LANES), 1)
    return lane < HEAD_GROUP, lane >= HEAD_GROUP


NT_DIMS = (((1,), (1,)), ((), ()))


def _row_update(idx, s, pv, m_ref, l_ref, acc_ref):
    m_prev = m_ref[idx]
    m_new = jnp.maximum(m_prev, jnp.max(s, axis=-1, keepdims=True))
    alpha = jnp.exp2(m_prev - m_new)
    p = jnp.exp2(s - m_new)
    l_ref[idx] = alpha * l_ref[idx] + jnp.sum(p, axis=-1, keepdims=True)
    acc_ref[idx] = alpha * acc_ref[idx] + pv(p.astype(BF16))
    m_ref[idx] = m_new


def _init_state(m_ref, l_ref, acc_ref):
    m_ref[...] = jnp.full(m_ref.shape, NEG, F32)
    l_ref[...] = jnp.zeros(l_ref.shape, F32)
    acc_ref[...] = jnp.zeros(acc_ref.shape, F32)


def _diff_lambda(lq1, lk1, lq2, lk2, lambda_init):
    a = jnp.exp(jnp.sum(lq1[...] * lk1[...], axis=-1, keepdims=True))
    b = jnp.exp(jnp.sum(lq2[...] * lk2[...], axis=-1, keepdims=True))
    return a - b + lambda_init


def _alibi_table(slope2, rows, cols, q0, k0):
    i = lax.broadcasted_iota(jnp.int32, (rows, cols), 0)
    j = lax.broadcasted_iota(jnp.int32, (rows, cols), 1)
    qpos = i + q0
    kpos = j + k0
    allowed = jnp.right_shift(kpos, CHUNK_SHIFT) <= jnp.right_shift(qpos, CHUNK_SHIFT)
    bias = slope2 * (i - jnp.abs(qpos - kpos)).astype(F32)
    return jnp.where(allowed, bias, NEG)


BIAS_TERMS = 3


def _bias_lanes(c):
    return HEAD_GROUP * (1 - c)


def _own_lanes(lane, c):
    return (lane >= HEAD_GROUP * c) & (lane < HEAD_GROUP * (c + 1))


def _augment_keys(k_blk, bias, c):
    lane = lax.broadcasted_iota(jnp.int32, k_blk.shape, 1)
    b0 = _bias_lanes(c)
    out = jnp.where(_own_lanes(lane, c), k_blk, jnp.zeros_like(k_blk))
    rest = bias
    for t in range(BIAS_TERMS):
        term = rest.astype(BF16)
        out = jnp.where(lane == b0 + t, term, out)
        rest = rest - term.astype(F32)
    return out


def _augment_queries(q, c):
    lane = lax.broadcasted_iota(jnp.int32, q.shape, 1)
    b0 = _bias_lanes(c)
    ones = jnp.where((lane >= b0) & (lane < b0 + BIAS_TERMS), 1.0, 0.0).astype(q.dtype)
    return jnp.where(_own_lanes(lane, c), q, ones)


ONES_ROWS = 16


def _transpose_values(v_ref, vt_ref, v_cols):
    nblk, n_sets, rows, tk = vt_ref.shape
    r = lax.broadcasted_iota(jnp.int32, (ONES_ROWS, tk), 0)
    tail = jnp.where(r == 0, 1.0, 0.0).astype(vt_ref.dtype)
    for blk in range(nblk):
        vt = v_ref[blk * tk:(blk + 1) * tk, :].astype(F32).T.astype(vt_ref.dtype)
        for s in range(n_sets):
            vt_ref[blk, s] = jnp.concatenate([vt[v_cols[s], :], tail], axis=0)


def _scores_t(ka, qa, q_lo):
    return lax.dot_general(ka, qa[q_lo:, :], NT_DIMS, preferred_element_type=F32)


def _block_t(idx, st, vt, tbl, q_lo, m_ref, acc_ref):
    at = idx + (slice(None), slice(q_lo, None))
    if tbl is not None:
        st = st + tbl
    m_prev = m_ref[at]
    m_new = jnp.maximum(m_prev, jnp.max(st, axis=0, keepdims=True))
    alpha = jnp.exp2(m_prev - m_new)
    pt = jnp.exp2(st - m_new).astype(BF16)
    acc_ref[at] = alpha * acc_ref[at] + jnp.dot(vt, pt, preferred_element_type=F32)
    m_ref[at] = m_new


def _sweep_t(qa, kaug_ref, vt_ref, tbl_ref, m_ref, acc_ref):
    nblk, n_sets, _, tk = vt_ref.shape
    nq = len(qa)
    assert qa[0][0].shape[0] == tk and tk % DIAG_TK == 0

    def step(qi, ki, lo, width, q_lo, tbl):
        rows = slice(ki * tk + lo, ki * tk + lo + width)
        st = [_scores_t(kaug_ref[c, rows, :], qa[qi][c], q_lo) for c in range(2)]
        for c in range(2):
            vt = vt_ref[ki, c % n_sets][:, lo:lo + width]
            _block_t((qi, c), st[c], vt, tbl, q_lo, m_ref, acc_ref)

    for ki in range(nblk):
        for lo in range(0, tk, DIAG_TK):
            step(ki, ki, lo, DIAG_TK, lo, tbl_ref[lo:lo + DIAG_TK, lo:])
        for qi in range(ki + 1, nq):
            step(qi, ki, 0, tk, 0, None)


def _augmented_query_tiles(q_ref, tq):
    return [[_augment_queries(q_ref[qi * tq:(qi + 1) * tq, :], c) for c in range(2)]
            for qi in range(q_ref.shape[0] // tq)]


def _diff_prompt_kernel(lambda_init, sl_ref, lq1, lk1, lq2, lk2, q_ref, k_ref, v_ref, g_ref,
                        o_ref, kaug_ref, vt_ref, m_ref, acc_ref, tbl_ref):
    nq, _, _, tq = m_ref.shape
    nblk, _, _, tk = vt_ref.shape
    hd2 = v_ref.shape[1]
    slope2 = sl_ref[pl.program_id(1)]
    for blk in range(nblk):
        rows = slice(blk * tk, (blk + 1) * tk)
        kpos = lax.broadcasted_iota(jnp.int32, (tk, LANES), 0) + blk * tk
        bias = slope2 * kpos.astype(F32)
        for c in range(2):
            kaug_ref[c, rows, :] = _augment_keys(k_ref[rows, :], bias, c)
    _transpose_values(v_ref, vt_ref, (slice(None),))
    dj = lax.broadcasted_iota(jnp.int32, (tk, tq), 0)
    i = lax.broadcasted_iota(jnp.int32, (tk, tq), 1)
    allowed = jnp.right_shift(dj, CHUNK_SHIFT) <= jnp.right_shift(i, CHUNK_SHIFT)
    ahead = jnp.maximum(dj - i, 0).astype(F32)
    tbl_ref[...] = jnp.where(allowed, (-2.0 * slope2) * ahead, NEG)

    m_ref[...] = jnp.full(m_ref.shape, NEG, F32)
    acc_ref[...] = jnp.zeros(acc_ref.shape, F32)
    _sweep_t(_augmented_query_tiles(q_ref, tq), kaug_ref, vt_ref, tbl_ref, m_ref, acc_ref)
    lam = _diff_lambda(lq1, lk1, lq2, lk2, lambda_init)
    for qi in range(nq):
        num = [acc_ref[qi, c, :hd2, :] / acc_ref[qi, c, hd2:hd2 + 1, :] for c in range(2)]
        ot = num[0] - lam * num[1]
        ms = jnp.mean(ot * ot, axis=0, keepdims=True)
        o = (ot * lax.rsqrt(ms + EPS)).T
        o_ref[qi * tq:(qi + 1) * tq, :] = ((o * g_ref[...]) * (1.0 - lambda_init)).astype(o_ref.dtype)


def _diff_prompt(qb, kb, vb, slopes2, lam_params, subln, n_batch, seq, lambda_init):
    d = qb.shape[1]
    tq = min(ATT_TQ, seq)
    tk = min(ATT_TK, tq)
    nq = seq // tq
    hd2 = 2 * HEAD_GROUP
    small = pl.BlockSpec((1, HEAD_GROUP), lambda b, h: (0, 0))
    whole_seq = pl.BlockSpec((seq, hd2), lambda b, h: (b, h))
    return pl.pallas_call(
        functools.partial(_diff_prompt_kernel, lambda_init),
        grid=(n_batch, DIFF_HEADS),
        in_specs=[pl.BlockSpec(memory_space=pltpu.SMEM), small, small, small, small,
                  whole_seq, whole_seq, whole_seq,
                  pl.BlockSpec((1, hd2), lambda b, h: (0, 0))],
        out_specs=whole_seq,
        out_shape=jax.ShapeDtypeStruct((n_batch * seq, d), BF16),
        scratch_shapes=[pltpu.VMEM((2, seq, hd2), BF16),
                        pltpu.VMEM((seq // tk, 1, hd2 + ONES_ROWS, tk), BF16),
                        pltpu.VMEM((nq, 2, 1, tq), F32),
                        pltpu.VMEM((nq, 2, hd2 + ONES_ROWS, tq), F32), pltpu.VMEM((tk, tq), F32)],
        compiler_params=_cparams(2),
        name="diff_attn_prompt",
    )(slopes2, *lam_params, qb, kb, vb, subln)


def _diff_sample_kernel(lambda_init, past, sl_ref, lq1, lk1, lq2, lk2, q_ref, kp_ref, vp_ref,
                        kn_ref, vn_ref, g_ref, o_ref, m_ref, l_ref, acc_ref):
    tq = q_ref.shape[0]
    n_heads = m_ref.shape[0]
    hd2 = kp_ref.shape[1]
    tk = kp_ref.shape[0] // n_heads
    step = pl.program_id(1)

    @pl.when(step == 0)
    def _():
        _init_state(m_ref, l_ref, acc_ref)

    lo, hi = _half_masks(tq)
    col = lax.broadcasted_iota(jnp.int32, (1, tk), 1) + (step * tk - past)

    def q_streams(h):
        q = q_ref[:, h * hd2:(h + 1) * hd2]
        zero = jnp.zeros_like(q)
        return jnp.where(lo, q, zero), jnp.where(hi, q, zero)

    scores = {}
    for h in range(n_heads):
        k_h = kp_ref[pl.ds(h, tk, stride=n_heads), :].astype(BF16)
        bias = sl_ref[h] * col.astype(F32)
        for c, q_c in enumerate(q_streams(h)):
            scores[h, c] = lax.dot_general(q_c, k_h, NT_DIMS, preferred_element_type=F32) + bias
    for h in range(n_heads):
        v_h = vp_ref[pl.ds(h, tk, stride=n_heads), :].astype(BF16)
        for c in range(2):
            _row_update((h, c), scores[h, c], lambda p: jnp.dot(p, v_h, preferred_element_type=F32),
                        m_ref, l_ref, acc_ref)

    @pl.when(step == pl.num_programs(1) - 1)
    def _():
        lam = _diff_lambda(lq1, lk1, lq2, lk2, lambda_init)
        new_scores = {}
        for h in range(n_heads):
            k_h = kn_ref[:, h * hd2:(h + 1) * hd2]
            bias = _alibi_table(sl_ref[h], tq, tq, past, past)
            for c, q_c in enumerate(q_streams(h)):
                new_scores[h, c] = lax.dot_general(q_c, k_h, NT_DIMS, preferred_element_type=F32) + bias
        for h in range(n_heads):
            cols = slice(h * hd2, (h + 1) * hd2)
            v_h = vn_ref[:, cols]
            for c in range(2):
                _row_update((h, c), new_scores[h, c], lambda p: jnp.dot(p, v_h, preferred_element_type=F32),
                            m_ref, l_ref, acc_ref)
            o = acc_ref[h, 0] / l_ref[h, 0] - lam * (acc_ref[h, 1] / l_ref[h, 1])
            ms = jnp.mean(o * o, axis=-1, keepdims=True)
            o_ref[:, cols] = ((o * lax.rsqrt(ms + EPS) * g_ref[...]) * (1.0 - lambda_init)).astype(o_ref.dtype)


def _diff_sample(qb, kb, vb, cache_k, cache_v, layer, slopes2, lam_params, subln, row0, n_batch, seq,
                 lambda_init):
    d = qb.shape[1]
    past, n_heads, hd2 = cache_k.shape[2:]
    tk = SAMPLE_TK if past % SAMPLE_TK == 0 else past
    blk0 = row0 // seq
    rows = lambda a: a.reshape(a.shape[:2] + (past * n_heads, hd2))
    cache_k, cache_v = rows(cache_k), rows(cache_v)
    small = pl.BlockSpec((1, HEAD_GROUP), lambda b, c: (0, 0))
    new = pl.BlockSpec((seq, d), lambda b, c: (blk0 + b, 0))
    old = pl.BlockSpec((None, None, tk * n_heads, hd2), lambda b, c: (layer, b, c, 0))
    return pl.pallas_call(
        functools.partial(_diff_sample_kernel, lambda_init, past),
        grid=(n_batch, past // tk),
        in_specs=[pl.BlockSpec(memory_space=pltpu.SMEM), small, small, small, small,
                  new, old, old, new, new, pl.BlockSpec((1, hd2), lambda b, c: (0, 0))],
        out_specs=pl.BlockSpec((seq, d), lambda b, c: (b, 0)),
        out_shape=jax.ShapeDtypeStruct((n_batch * seq, d), BF16),
        scratch_shapes=[pltpu.VMEM((n_heads, 2, seq, 1), F32), pltpu.VMEM((n_heads, 2, seq, 1), F32),
                        pltpu.VMEM((n_heads, 2, seq, hd2), F32)],
        compiler_params=_cparams(2),
        name="diff_attn_sample",
    )(slopes2, *lam_params, qb, cache_k, cache_v, kb, vb, subln)


def _causal_table(rows, cols):
    i = lax.broadcasted_iota(jnp.int32, (rows, cols), 0)
    j = lax.broadcasted_iota(jnp.int32, (rows, cols), 1)
    return jnp.where(j <= i, 0.0, NEG).astype(F32)


def _fox_prompt_kernel(q_ref, k_ref, v_ref, lf_ref, sg_ref, o_ref,
                       cum_ref, kaug_ref, vt_ref, m_ref, acc_ref, tbl_ref):
    nq, _, _, tq = m_ref.shape
    nblk, _, _, tk = vt_ref.shape
    pair = pl.program_id(1)

    @pl.when(pair == 0)
    def _():
        r = lax.broadcasted_iota(jnp.int32, (tk, tk), 0)
        c = lax.broadcasted_iota(jnp.int32, (tk, tk), 1)
        tril = jnp.where(c <= r, 1.0, 0.0).astype(BF16)
        carry = jnp.zeros((1, LANES), F32)
        for blk in range(nblk):
            x = lf_ref[blk * tk:(blk + 1) * tk, :]
            hi = x.astype(BF16)
            lo = (x - hi.astype(F32)).astype(BF16)
            part = (jnp.dot(tril, hi, preferred_element_type=F32)
                    + jnp.dot(tril, lo, preferred_element_type=F32)) + carry
            cum_ref[blk * tk:(blk + 1) * tk, :] = part * (-LOG2E)
            carry = carry + jnp.sum(x, axis=0, keepdims=True)

    lane = lax.broadcasted_iota(jnp.int32, (tk, LANES), 1)
    for blk in range(nblk):
        rows = slice(blk * tk, (blk + 1) * tk)
        cum = cum_ref[rows, :]
        for c in range(2):
            col = jnp.sum(jnp.where(lane == 2 * pair + c, cum, 0.0), axis=-1, keepdims=True)
            bias = jnp.broadcast_to(col, (tk, LANES))
            kaug_ref[c, rows, :] = _augment_keys(k_ref[rows, :], bias, c)
    hd = HEAD_GROUP
    _transpose_values(v_ref, vt_ref, (slice(0, hd), slice(hd, 2 * hd)))
    dj = lax.broadcasted_iota(jnp.int32, (tk, tq), 0)
    i = lax.broadcasted_iota(jnp.int32, (tk, tq), 1)
    tbl_ref[...] = jnp.where(dj <= i, 0.0, NEG).astype(F32)

    m_ref[...] = jnp.full(m_ref.shape, NEG, F32)
    acc_ref[...] = jnp.zeros(acc_ref.shape, F32)
    _sweep_t(_augmented_query_tiles(q_ref, tq), kaug_ref, vt_ref, tbl_ref, m_ref, acc_ref)
    for qi in range(nq):
        rows = slice(qi * tq, (qi + 1) * tq)
        ot = jnp.concatenate([acc_ref[qi, c, :hd, :] / acc_ref[qi, c, hd:hd + 1, :] for c in range(2)],
                             axis=0)
        o_ref[rows, :] = (ot.T * sg_ref[rows, :].astype(F32)).astype(o_ref.dtype)


def _fox_prompt(qb, kb, vb, lf, sg, n_batch, seq):
    d = qb.shape[1]
    tq = min(ATT_TQ, seq)
    tk = min(ATT_TK, tq)
    nq = seq // tq
    w = LANES
    whole_seq = pl.BlockSpec((seq, w), lambda b, p: (b, p))
    return pl.pallas_call(
        _fox_prompt_kernel,
        grid=(n_batch, d // w),
        in_specs=[whole_seq, whole_seq, whole_seq,
                  pl.BlockSpec((seq, LANES), lambda b, p: (b, 0)),
                  whole_seq],
        out_specs=whole_seq,
        out_shape=jax.ShapeDtypeStruct((n_batch * seq, d), BF16),
        scratch_shapes=[pltpu.VMEM((seq, LANES), F32),
                        pltpu.VMEM((2, seq, w), BF16),
                        pltpu.VMEM((seq // tk, 2, HEAD_GROUP + ONES_ROWS, tk), BF16),
                        pltpu.VMEM((nq, 2, 1, tq), F32),
                        pltpu.VMEM((nq, 2, HEAD_GROUP + ONES_ROWS, tq), F32), pltpu.VMEM((tk, tq), F32)],
        compiler_params=_cparams(2),
        name="fox_attn_prompt",
    )(qb, kb, vb, lf, sg)


def _fox_sample_kernel(q_ref, kp_ref, vp_ref, kn_ref, vn_ref, cp_ref, cn_ref, sg_ref,
                       o_ref, m_ref, l_ref, acc_ref):
    tq = q_ref.shape[0]
    n_heads, hd, tk = kp_ref.shape
    step = pl.program_id(1)

    @pl.when(step == 0)
    def _():
        _init_state(m_ref, l_ref, acc_ref)

    scores = []
    for h in range(n_heads):
        q_h = q_ref[:, h * hd:(h + 1) * hd]
        kt_h = kp_ref[h].astype(BF16)
        scores.append(jnp.dot(q_h, kt_h, preferred_element_type=F32) + cp_ref[h:h + 1, :])
    for h in range(n_heads):
        vt_h = vp_ref[h].astype(BF16)
        _row_update(h, scores[h], lambda p: lax.dot_general(p, vt_h, NT_DIMS, preferred_element_type=F32),
                    m_ref, l_ref, acc_ref)

    @pl.when(step == pl.num_programs(1) - 1)
    def _():
        tbl = _causal_table(tq, tq)
        new_scores = []
        for h in range(n_heads):
            cols = slice(h * hd, (h + 1) * hd)
            new_scores.append(lax.dot_general(q_ref[:, cols], kn_ref[:, cols], NT_DIMS,
                                              preferred_element_type=F32) + (tbl + cn_ref[h:h + 1, :]))
        for h in range(n_heads):
            cols = slice(h * hd, (h + 1) * hd)
            v_h = vn_ref[:, cols]
            _row_update(h, new_scores[h], lambda p: jnp.dot(p, v_h, preferred_element_type=F32),
                        m_ref, l_ref, acc_ref)
            o = acc_ref[h] / l_ref[h]
            o_ref[:, cols] = (o * sg_ref[:, cols].astype(F32)).astype(o_ref.dtype)


def _fox_sample(qb, kb, vb, cache_kt, cache_vt, layer, c_past, c_new, sg, row0, n_batch, seq):
    d = qb.shape[1]
    n_heads, hd, past = cache_kt.shape[2:]
    tk = SAMPLE_TK if past % SAMPLE_TK == 0 else past
    blk0 = row0 // seq
    new = pl.BlockSpec((seq, d), lambda b, c: (blk0 + b, 0))
    old = pl.BlockSpec((None, None, n_heads, hd, tk), lambda b, c: (layer, b, 0, 0, c))
    return pl.pallas_call(
        _fox_sample_kernel,
        grid=(n_batch, past // tk),
        in_specs=[new, old, old, new, new,
                  pl.BlockSpec((None, n_heads, tk), lambda b, c: (b, 0, c)),
                  pl.BlockSpec((None, n_heads, seq), lambda b, c: (b, 0, 0)),
                  new],
        out_specs=pl.BlockSpec((seq, d), lambda b, c: (b, 0)),
        out_shape=jax.ShapeDtypeStruct((n_batch * seq, d), BF16),
        scratch_shapes=[pltpu.VMEM((n_heads, seq, 1), F32), pltpu.VMEM((n_heads, seq, 1), F32),
                        pltpu.VMEM((n_heads, seq, hd), F32)],
        compiler_params=_cparams(2),
        name="fox_attn_sample",
    )(qb, cache_kt, cache_vt, kb, vb, c_past, c_new, sg)


def _split_dot(x, mat):
    hi = x.astype(BF16)
    lo = (x - hi.astype(F32)).astype(BF16)
    return (jnp.dot(hi, mat, preferred_element_type=F32) + jnp.dot(lo, mat, preferred_element_type=F32))


def _coff_kernel(suffix, tk, x_ref, o_ref):
    nblk = x_ref.shape[1] // tk
    r = lax.broadcasted_iota(jnp.int32, (tk, tk), 0)
    c = lax.broadcasted_iota(jnp.int32, (tk, tk), 1)
    mat = jnp.where((r > c) if suffix else (r <= c), 1.0, 0.0).astype(BF16)
    carry = jnp.zeros((x_ref.shape[0], 1), F32)
    order = range(nblk - 1, -1, -1) if suffix else range(nblk)
    for blk in order:
        cols = slice(blk * tk, (blk + 1) * tk)
        x = x_ref[:, cols]
        part = _split_dot(x, mat) + carry
        o_ref[:, cols] = (part if suffix else -part) * LOG2E
        carry = carry + jnp.sum(x, axis=-1, keepdims=True)


def _coff(xt, tk, suffix):
    n_batch, n_head, length = xt.shape
    spec = pl.BlockSpec((None, n_head, length), lambda b: (b, 0, 0))
    return pl.pallas_call(
        functools.partial(_coff_kernel, suffix, tk),
        grid=(n_batch,),
        in_specs=[spec],
        out_specs=spec,
        out_shape=jax.ShapeDtypeStruct((n_batch, n_head, length), F32),
        compiler_params=_cparams(1),
        name="fox_coff_suffix" if suffix else "fox_coff_prefix",
    )(xt)


def _outproj_kernel(tiles_p, op_ref, os_ref, x_ref, w_ref, gf_ref, wr_hi_ref, wr_lo_ref, br_ref,
                    xo_ref, meta_ref):
    o = jnp.where(pl.program_id(0) < tiles_p, op_ref[...], os_ref[...])
    xn = x_ref[...] + jnp.dot(o, w_ref[...], preferred_element_type=F32)
    xo_ref[...] = xn
    t = _rms_rows(xn, gf_ref[...])
    t_hi = t.astype(BF16)
    t_lo = (t - t_hi.astype(F32)).astype(BF16)
    logits = (jnp.dot(t_hi, wr_hi_ref[...], preferred_element_type=F32)
              + jnp.dot(t_lo, wr_hi_ref[...], preferred_element_type=F32)
              + jnp.dot(t_hi, wr_lo_ref[...], preferred_element_type=F32)) + br_ref[...]
    rows = logits.shape[0]
    lane = lax.broadcasted_iota(jnp.int32, (rows, LANES), 1).astype(F32)
    big = float(LANES)

    def first_argmax(vals, vmax):
        return jnp.min(jnp.where(vals == vmax, lane, big), axis=-1, keepdims=True)

    gl = jnp.where(lane < N_GROUPS, logits, NEG)
    gmax = jnp.max(gl, axis=-1, keepdims=True)
    gsum = jnp.sum(jnp.where(lane < N_GROUPS, jnp.exp(logits - gmax), 0.0), axis=-1, keepdims=True)
    p_top = 1.0 / gsum
    g_idx = first_argmax(gl, gmax)
    base = N_GROUPS + EXPERTS_PER_GROUP * g_idx
    el = jnp.where((lane >= base) & (lane < base + EXPERTS_PER_GROUP), logits, NEG)
    v1 = jnp.max(el, axis=-1, keepdims=True)
    i1 = first_argmax(el, v1)
    el2 = jnp.where(lane == i1, NEG, el)
    v2 = jnp.max(el2, axis=-1, keepdims=True)
    i2 = first_argmax(el2, v2)
    e2 = jnp.exp(v2 - v1)
    w1 = p_top / (1.0 + e2)
    w2 = p_top * e2 / (1.0 + e2)
    a1 = i1 - base
    a2 = i2 - base
    lo = jnp.minimum(a1, a2)
    hi = jnp.maximum(a1, a2)
    wa = jnp.where(a1 < a2, w1, w2)
    wb = jnp.where(a1 < a2, w2, w1)
    pair = jnp.where(lo == 0.0, hi - 1.0, jnp.where(lo == 1.0, hi + 1.0, 5.0))
    bucket = g_idx * N_PAIRS + pair
    meta_ref[...] = jnp.where(lane == 0.0, bucket, jnp.where(lane == 1.0, wa, jnp.where(lane == 2.0, wb, 0.0)))


def _outproj(o_p, o_s, x, w_all, layer, gf, wr_hi, wr_lo, br):
    n, d = x.shape
    tm = TOK_TILE
    tiles_p = o_p.shape[0] // tm
    row = lambda i: (i, 0)
    fixed = lambda i: (0, 0)
    big = pl.BlockSpec((tm, d), row)
    return pl.pallas_call(
        functools.partial(_outproj_kernel, tiles_p),
        grid=(n // tm,),
        in_specs=[pl.BlockSpec((tm, d), lambda i: (jnp.minimum(i, tiles_p - 1), 0)),
                  pl.BlockSpec((tm, d), lambda i: (jnp.maximum(i - tiles_p, 0), 0)),
                  big,
                  pl.BlockSpec((None, d, d), lambda i: (layer, 0, 0)),
                  pl.BlockSpec((1, d), fixed),
                  pl.BlockSpec((d, LANES), fixed), pl.BlockSpec((d, LANES), fixed),
                  pl.BlockSpec((1, LANES), fixed)],
        out_specs=[big, pl.BlockSpec((tm, LANES), row)],
        out_shape=[jax.ShapeDtypeStruct((n, d), F32), jax.ShapeDtypeStruct((n, LANES), F32)],
        compiler_params=_cparams(1),
        name="outproj_router",
    )(o_p, o_s, x, w_all, gf, wr_hi, wr_lo, br)


def _moe_kernel(tile_ref, ea_ref, eb_ref, lo_ref, hi_ref, first_ref,
                xs_ref, ms_ref, gf_ref, wg_a, wu_a, wd_a, wg_b, wu_b, wd_b, out_ref):
    s = pl.program_id(0)
    lo = lo_ref[s]
    hi = hi_ref[s]

    @pl.when(first_ref[s] == 1)
    def _():
        out_ref[...] = xs_ref[...]

    @pl.when(hi > lo)
    def _():
        t = _rms_rows(xs_ref[...], gf_ref[...]).astype(BF16)
        rows = t.shape[0]
        r = lax.broadcasted_iota(jnp.int32, (rows, 1), 0)
        inside = (r >= lo) & (r < hi)
        ms = ms_ref[...]
        wa = jnp.where(inside, ms[:, 1:2], 0.0)
        wb = jnp.where(inside, ms[:, 2:3], 0.0)

        def hidden(g, u):
            return (g / (1.0 + jnp.exp(-g)) * u).astype(BF16)

        g_a = jnp.dot(t, wg_a[...], preferred_element_type=F32)
        u_a = jnp.dot(t, wu_a[...], preferred_element_type=F32)
        g_b = jnp.dot(t, wg_b[...], preferred_element_type=F32)
        u_b = jnp.dot(t, wu_b[...], preferred_element_type=F32)
        y_a = jnp.dot(hidden(g_a, u_a), wd_a[...], preferred_element_type=F32)
        y_b = jnp.dot(hidden(g_b, u_b), wd_b[...], preferred_element_type=F32)
        out_ref[...] += wa * y_a + wb * y_b


def _moe(sched, xs, ms, gf, wg_all, wu_all, wd_all, layer):
    n, d = xs.shape
    de = wd_all.shape[2]
    n_steps = sched[0].shape[0]
    tile_map = lambda s, tile, ea, eb, lo, hi, first: (tile[s], 0)
    fixed = lambda s, *_: (0, 0)
    wa_map = lambda s, tile, ea, eb, lo, hi, first: (layer, ea[s], 0, 0)
    wb_map = lambda s, tile, ea, eb, lo, hi, first: (layer, eb[s], 0, 0)
    grid_spec = pltpu.PrefetchScalarGridSpec(
        num_scalar_prefetch=6,
        grid=(n_steps,),
        in_specs=[pl.BlockSpec((MOE_TILE, d), tile_map),
                  pl.BlockSpec((MOE_TILE, ms.shape[1]), tile_map),
                  pl.BlockSpec((1, d), fixed),
                  pl.BlockSpec((None, None, d, de), wa_map),
                  pl.BlockSpec((None, None, d, de), wa_map),
                  pl.BlockSpec((None, None, de, d), wa_map),
                  pl.BlockSpec((None, None, d, de), wb_map),
                  pl.BlockSpec((None, None, d, de), wb_map),
                  pl.BlockSpec((None, None, de, d), wb_map)],
        out_specs=pl.BlockSpec((MOE_TILE, d), tile_map),
    )
    return pl.pallas_call(
        _moe_kernel,
        grid_spec=grid_spec,
        out_shape=jax.ShapeDtypeStruct((n, d), F32),
        compiler_params=_cparams(1),
        name="moe_experts",
    )(*sched, xs, ms, gf, wg_all, wu_all, wd_all, wg_all, wu_all, wd_all)


def _moe_schedule(bucket, n):
    n_tiles = n // MOE_TILE
    n_steps = n_tiles + N_BUCKETS - 1
    perm = jnp.argsort(bucket, stable=True).astype(jnp.int32)
    sorted_b = bucket[perm]
    inv = jnp.argsort(perm).astype(jnp.int32)
    offs = jnp.searchsorted(sorted_b, jnp.arange(N_BUCKETS + 1, dtype=jnp.int32), side="left")
    offs = offs.astype(jnp.int32)
    fb = sorted_b[::MOE_TILE]
    lb = sorted_b[MOE_TILE - 1::MOE_TILE]
    per_tile = lb - fb + 1
    starts = jnp.cumsum(per_tile) - per_tile
    total = jnp.sum(per_tile)
    s = jnp.arange(n_steps, dtype=jnp.int32)
    tile = jnp.clip(jnp.searchsorted(starts, s, side="right").astype(jnp.int32) - 1, 0, n_tiles - 1)
    valid = s < total
    bkt = jnp.where(valid, fb[tile] + (s - starts[tile]), lb[n_tiles - 1])
    lo = jnp.clip(offs[bkt] - tile * MOE_TILE, 0, MOE_TILE)
    hi = jnp.clip(offs[bkt + 1] - tile * MOE_TILE, 0, MOE_TILE)
    lo = jnp.where(valid, lo, 0)
    hi = jnp.where(valid, hi, 0)
    first = (valid & (s == starts[tile])).astype(jnp.int32)
    grp = bkt // N_PAIRS
    ea = grp * EXPERTS_PER_GROUP + jnp.asarray(PAIR_A)[bkt % N_PAIRS]
    eb = grp * EXPERTS_PER_GROUP + jnp.asarray(PAIR_B)[bkt % N_PAIRS]
    return perm, inv, (tile, ea.astype(jnp.int32), eb.astype(jnp.int32), lo.astype(jnp.int32),
                       hi.astype(jnp.int32), first)


def _tile_gain(g, d):
    return jnp.tile(g.astype(F32), d // g.shape[0]).reshape(1, d)


def kernel(x_prompt, x_sample, cache_diff_k, cache_diff_v, cache_fox_k, cache_fox_v, cache_fox_logf, norm_mix, norm_ffn, diff_w_in, diff_w_out, diff_q_norm, diff_k_norm, diff_lambda_q1, diff_lambda_k1, diff_lambda_q2, diff_lambda_k2, diff_subln, fox_w_in, fox_b_f, fox_w_out, fox_q_norm, fox_k_norm, moe_w_group, moe_b_group, moe_w_expert, moe_b_expert, moe_w_gate, moe_w_up, moe_w_down):
    bp, sp, d = x_prompt.shape
    bs, ss, _ = x_sample.shape
    n_p = bp * sp
    n_s = bs * ss
    n = n_p + n_s
    depth = norm_mix.shape[0]
    past = cache_diff_k.shape[2]
    assert n % TOK_TILE == 0 and n % MOE_TILE == 0 and n_p % TOK_TILE == 0
    assert d % LANES == 0 and sp % min(ATT_TQ, sp) == 0 and n_p % ss == 0
    assert DIFF_HEADS * 2 * HEAD_GROUP == d and FOX_HEADS * HEAD_GROUP == d

    summ, expand = _group_mats(d)
    x = jnp.concatenate([x_prompt.reshape(n_p, d), x_sample.reshape(n_s, d)], axis=0)

    diff_w_in_b = diff_w_in.astype(BF16)
    diff_w_out_b = diff_w_out.astype(BF16)
    fox_w_main_b = fox_w_in[:, :, :4 * d].astype(BF16)
    fox_w_f_b = jnp.pad(fox_w_in[:, :, 4 * d:], ((0, 0), (0, 0), (0, LANES - FOX_HEADS))).astype(BF16)
    fox_w_out_b = fox_w_out.astype(BF16)
    wg_b = moe_w_gate.astype(BF16)
    wu_b = moe_w_up.astype(BF16)
    wd_b = moe_w_down.astype(BF16)
    w_router = jnp.pad(jnp.concatenate([moe_w_group, moe_w_expert], axis=-1),
                       ((0, 0), (0, 0), (0, LANES - N_GROUPS - N_EXPERTS)))
    wr_hi = w_router.astype(BF16)
    wr_lo = (w_router - wr_hi.astype(F32)).astype(BF16)
    b_router = jnp.pad(jnp.concatenate([moe_b_group, moe_b_expert], axis=-1),
                       ((0, 0), (0, LANES - N_GROUPS - N_EXPERTS)))
    slopes2 = jnp.asarray(2.0 ** (-8.0 * np.arange(1, DIFF_HEADS + 1) / DIFF_HEADS) * LOG2E, F32)
    cache_fkt = jnp.transpose(cache_fox_k, (0, 1, 3, 4, 2))
    cache_fvt = jnp.transpose(cache_fox_v, (0, 1, 3, 4, 2))
    cache_flt = jnp.transpose(cache_fox_logf.astype(F32), (0, 1, 3, 2))

    def from_feature_major(a, seq):
        return a.reshape(a.shape[0], FOX_HEADS, HEAD_GROUP, seq).transpose(0, 3, 1, 2)

    outs = {name: [] for name in ("dks", "dvs", "flp", "fks", "fvs", "fls")}
    n_diff, n_fox = diff_w_in.shape[0], fox_w_in.shape[0]
    diff_cache_p = tuple(jnp.zeros((n_diff, n_p * DIFF_HEADS, LANES), F32) for _ in range(2))
    fox_cache_p = tuple(jnp.zeros((n_fox, bp, d, sp), F32) for _ in range(2))
    for i in range(depth):
        j = i // 2
        gm = norm_mix[i].reshape(1, d)
        if i % 2 == 0:
            lambda_init = 0.8 - 0.6 * math.exp(-0.3 * i)
            qb, kb, vb, k_p, v_p, k_s, v_s = _inproj_diff(
                x, gm, diff_w_in_b, j, n_diff, _tile_gain(diff_q_norm[j], d), _tile_gain(diff_k_norm[j], d),
                summ, expand, n_p, diff_cache_p)
            diff_cache_p = (k_p, v_p)
            lam_params = [p[j].reshape(1, HEAD_GROUP).astype(F32) for p in
                          (diff_lambda_q1, diff_lambda_k1, diff_lambda_q2, diff_lambda_k2)]
            subln = diff_subln[j].reshape(1, 2 * HEAD_GROUP)
            o_p = _diff_prompt(qb, kb, vb, slopes2, lam_params, subln, bp, sp, lambda_init)
            o_s = _diff_sample(qb, kb, vb, cache_diff_k, cache_diff_v, j, slopes2, lam_params, subln,
                               n_p, bs, ss, lambda_init)
            w_out_b = diff_w_out_b
            outs["dks"].append(k_s.reshape(bs, ss, DIFF_HEADS, 2 * HEAD_GROUP))
            outs["dvs"].append(v_s.reshape(bs, ss, DIFF_HEADS, 2 * HEAD_GROUP))
        else:
            bf = jnp.pad(fox_b_f[j], (0, LANES - FOX_HEADS)).reshape(1, LANES)
            qb, kb, vb, sg, lf, kt_p, vt_p, lt_p, kt_s, vt_s, lt_s = _inproj_fox(
                x, gm, fox_w_main_b, fox_w_f_b, j, n_fox, bf, _tile_gain(fox_q_norm[j], d),
                _tile_gain(fox_k_norm[j], d), summ, expand, n_p, sp, ss, fox_cache_p)
            fox_cache_p = (kt_p, vt_p)
            tk_c = SAMPLE_TK if past % SAMPLE_TK == 0 else past
            c_new = _coff(lt_s, ss, False)
            c_past = _coff(cache_flt[j], tk_c, True)
            o_p = _fox_prompt(qb, kb, vb, lf, sg, bp, sp)
            o_s = _fox_sample(qb, kb, vb, cache_fkt, cache_fvt, j, c_past, c_new, sg, n_p, bs, ss)
            w_out_b = fox_w_out_b
            outs["flp"].append(lt_p.transpose(0, 2, 1))
            outs["fks"].append(from_feature_major(kt_s, ss))
            outs["fvs"].append(from_feature_major(vt_s, ss))
            outs["fls"].append(lt_s.transpose(0, 2, 1))
        gf = norm_ffn[i].reshape(1, d)
        x, meta = _outproj(o_p, o_s, x, w_out_b, j, gf, wr_hi[i], wr_lo[i], b_router[i].reshape(1, LANES))
        bucket = meta[:, 0].astype(jnp.int32)
        perm, inv, sched = _moe_schedule(bucket, n)
        xs = x[perm]
        ms = meta[:, :8][perm]
        ys = _moe(sched, xs, ms, gf, wg_b, wu_b, wd_b, i)
        if i + 1 < depth:
            x = ys[inv]

    y_prompt = ys[inv[:n_p]].reshape(bp, sp, d)
    y_sample = ys[inv[n_p:]].reshape(bs, ss, d)
    stk = lambda name: jnp.stack(outs[name])
    dkp, dvp = (a.reshape(n_diff, bp, sp, DIFF_HEADS, 2 * HEAD_GROUP) for a in diff_cache_p)
    fkp, fvp = (a.reshape(n_fox, bp, FOX_HEADS, HEAD_GROUP, sp).transpose(0, 1, 4, 2, 3)
                for a in fox_cache_p)
    return (y_prompt, y_sample, dkp, dvp, fkp, fvp, stk("flp"),
            stk("dks"), stk("dvs"), stk("fks"), stk("fvs"), stk("fls"))
```

```python
import functools
import math

import numpy as np
import jax
import jax.numpy as jnp
from jax import lax
from jax.experimental import pallas as pl
from jax.experimental.pallas import tpu as pltpu

F32 = jnp.float32
BF16 = jnp.bfloat16

CHUNK_SHIFT = 6
DIFF_HEADS = 8
FOX_HEADS = 16
HEAD_GROUP = 64
N_GROUPS = 4
EXPERTS_PER_GROUP = 4
N_EXPERTS = N_GROUPS * EXPERTS_PER_GROUP
N_PAIRS = 6
N_BUCKETS = N_GROUPS * N_PAIRS
EPS = 1e-6
NEG = -1e30
LOG2E = math.log2(math.e)
QSCALE = (HEAD_GROUP ** -0.5) * LOG2E

LANES = 128
TOK_TILE = 512
ATT_TQ = 512
ATT_TK = 512
DIAG_TK = 512
SAMPLE_TK = 2048
MOE_TILE = 256
VMEM_LIMIT = 56 * 1024 * 1024

PAIR_A = np.array([0, 0, 0, 1, 1, 2], np.int32)
PAIR_B = np.array([1, 2, 3, 2, 3, 3], np.int32)


def _cparams(n_axes):
    return pltpu.CompilerParams(dimension_semantics=("arbitrary",) * n_axes,
                                vmem_limit_bytes=VMEM_LIMIT)


def _rms_rows(x, gain):
    ms = jnp.mean(x * x, axis=-1, keepdims=True)
    return x * lax.rsqrt(ms + EPS) * gain


def _group_norm(z, gain, sum_ref, exp_ref):
    ms = jnp.dot((z * z).astype(BF16), sum_ref[...], preferred_element_type=F32)
    r = lax.rsqrt(ms + EPS)
    r_hi = r.astype(BF16)
    r_lo = (r - r_hi.astype(F32)).astype(BF16)
    rb = jnp.dot(jnp.concatenate([r_hi, r_lo], axis=-1), exp_ref[...], preferred_element_type=F32)
    return z * rb * gain


def _group_mats(d):
    n_g = d // HEAD_GROUP
    col = np.arange(d) // HEAD_GROUP
    summ = np.zeros((d, LANES), np.float32)
    summ[np.arange(d), col] = 1.0 / HEAD_GROUP
    expand = np.zeros((2 * LANES, d), np.float32)
    expand[col, np.arange(d)] = 1.0
    expand[LANES + col, np.arange(d)] = 1.0
    assert n_g <= LANES
    return jnp.asarray(summ, BF16), jnp.asarray(expand, BF16)


def _put_heads(ref, val):
    rows = val.shape[0]
    n_heads = ref.shape[0] // rows
    for h in range(n_heads):
        ref[pl.ds(h, rows, stride=n_heads), :] = val[:, h * LANES:(h + 1) * LANES]


def _inproj_diff_kernel(tiles_p, n_carried, x_ref, gm_ref, w_ref, qg_ref, kg_ref, sum_ref, exp_ref, *refs):
    qb_ref, kb_ref, vb_ref, kp_ref, vp_ref, ks_ref, vs_ref = refs[n_carried:]
    d = x_ref.shape[1]
    i = pl.program_id(0)
    h = _rms_rows(x_ref[...], gm_ref[...]).astype(BF16)
    q = jnp.dot(h, w_ref[:, 0:d], preferred_element_type=F32)
    k = jnp.dot(h, w_ref[:, d:2 * d], preferred_element_type=F32)
    v = jnp.dot(h, w_ref[:, 2 * d:3 * d], preferred_element_type=F32)
    qb_ref[...] = (_group_norm(q, qg_ref[...], sum_ref, exp_ref) * QSCALE).astype(BF16)
    kn = _group_norm(k, kg_ref[...], sum_ref, exp_ref)
    kb_ref[...] = kn.astype(BF16)
    vb_ref[...] = v.astype(BF16)

    @pl.when(i < tiles_p)
    def _():
        _put_heads(kp_ref, kn)
        _put_heads(vp_ref, v)

    @pl.when(i >= tiles_p)
    def _():
        _put_heads(ks_ref, kn)
        _put_heads(vs_ref, v)


def _carry_specs(carried, n_fixed_inputs, first_out):
    specs = [pl.BlockSpec(memory_space=pl.ANY)] * len(carried)
    aliases = {n_fixed_inputs + k: first_out + k for k in range(len(carried))}
    return specs, aliases


def _inproj_diff(x, gm, w_all, layer, n_layers, qg, kg, summ, expand, n_p, carried):
    n, d = x.shape
    tm = TOK_TILE
    tiles_p = n_p // tm
    row = lambda i: (i, 0)
    fixed = lambda i: (0, 0)
    big = pl.BlockSpec((tm, d), row)
    heads = pl.BlockSpec((None, tm * DIFF_HEADS, LANES), lambda i: (layer, jnp.minimum(i, tiles_p - 1), 0))
    heads_s = pl.BlockSpec((tm * DIFF_HEADS, LANES), lambda i: (jnp.maximum(i - tiles_p, 0), 0))
    cache_p = jax.ShapeDtypeStruct((n_layers, n_p * DIFF_HEADS, LANES), F32)
    cache_s = jax.ShapeDtypeStruct(((n - n_p) * DIFF_HEADS, LANES), F32)
    carry_specs, aliases = _carry_specs(carried, 7, 3)
    return pl.pallas_call(
        functools.partial(_inproj_diff_kernel, tiles_p, len(carried)),
        grid=(n // tm,),
        in_specs=[big,
                  pl.BlockSpec((1, d), fixed),
                  pl.BlockSpec((None, d, 3 * d), lambda i: (layer, 0, 0)),
                  pl.BlockSpec((1, d), fixed), pl.BlockSpec((1, d), fixed),
                  pl.BlockSpec(summ.shape, fixed), pl.BlockSpec(expand.shape, fixed)] + carry_specs,
        out_specs=[big, big, big, heads, heads, heads_s, heads_s],
        out_shape=[jax.ShapeDtypeStruct((n, d), BF16), jax.ShapeDtypeStruct((n, d), BF16),
                   jax.ShapeDtypeStruct((n, d), BF16), cache_p, cache_p, cache_s, cache_s],
        input_output_aliases=aliases,
        compiler_params=_cparams(1),
        name="inproj_diff",
    )(x, gm, w_all, qg, kg, summ, expand, *carried)


def _inproj_fox_kernel(tiles_p, n_carried, x_ref, gm_ref, w_ref, wf_ref, bf_ref, qg_ref, kg_ref,
                       sum_ref, exp_ref, *refs):
    (qb_ref, kb_ref, vb_ref, sg_ref, lf_ref,
     kp_ref, vp_ref, lp_ref, ks_ref, vs_ref, ls_ref) = refs[n_carried:]
    d = x_ref.shape[1]
    i = pl.program_id(0)
    h = _rms_rows(x_ref[...], gm_ref[...]).astype(BF16)
    q = jnp.dot(h, w_ref[:, 0:d], preferred_element_type=F32)
    qb_ref[...] = (_group_norm(q, qg_ref[...], sum_ref, exp_ref) * QSCALE).astype(BF16)
    k = jnp.dot(h, w_ref[:, d:2 * d], preferred_element_type=F32)
    kn = _group_norm(k, kg_ref[...], sum_ref, exp_ref)
    kb_ref[...] = kn.astype(BF16)
    v = jnp.dot(h, w_ref[:, 2 * d:3 * d], preferred_element_type=F32)
    vb_ref[...] = v.astype(BF16)
    g = jnp.dot(h, w_ref[:, 3 * d:4 * d], preferred_element_type=F32)
    sg_ref[...] = (1.0 / (1.0 + jnp.exp(-g))).astype(BF16)
    f = jnp.dot(h, wf_ref[...], preferred_element_type=F32) + bf_ref[...]
    lf = jnp.minimum(f, 0.0) - jnp.log(1.0 + jnp.exp(-jnp.abs(f)))
    lf_ref[...] = lf
    n_heads = lp_ref.shape[0]

    @pl.when(i < tiles_p)
    def _():
        kp_ref[...] = kn.T
        vp_ref[...] = v.T
        lp_ref[...] = lf.T[:n_heads, :]

    @pl.when(i >= tiles_p)
    def _():
        seq = ks_ref.shape[2]
        for bb in range(ks_ref.shape[0]):
            rows = slice(bb * seq, (bb + 1) * seq)
            ks_ref[bb] = kn[rows, :].T
            vs_ref[bb] = v[rows, :].T
            ls_ref[bb] = lf[rows, :].T[:n_heads, :]


def _inproj_fox(x, gm, w_all, wf_all, layer, n_layers, bf, qg, kg, summ, expand, n_p, seq_p, seq_s,
                carried):
    n, d = x.shape
    tm = TOK_TILE
    tiles_p = n_p // tm
    per_b = seq_p // tm
    grp = tm // seq_s
    row = lambda i: (i, 0)
    fixed = lambda i: (0, 0)
    big = pl.BlockSpec((tm, d), row)

    def p_map(i):
        ip = jnp.minimum(i, tiles_p - 1)
        return (ip // per_b, 0, ip % per_b)

    def stacked_map(i):
        return (layer,) + p_map(i)

    s_map = lambda i: (jnp.maximum(i - tiles_p, 0), 0, 0)
    f32 = lambda *shape: jax.ShapeDtypeStruct(shape, F32)
    bf16 = jax.ShapeDtypeStruct((n, d), BF16)
    carry_specs, aliases = _carry_specs(carried, 9, 5)
    return pl.pallas_call(
        functools.partial(_inproj_fox_kernel, tiles_p, len(carried)),
        grid=(n // tm,),
        in_specs=[big,
                  pl.BlockSpec((1, d), fixed),
                  pl.BlockSpec((None, d, 4 * d), lambda i: (layer, 0, 0)),
                  pl.BlockSpec((None, d, LANES), lambda i: (layer, 0, 0)),
                  pl.BlockSpec((1, LANES), fixed),
                  pl.BlockSpec((1, d), fixed), pl.BlockSpec((1, d), fixed),
                  pl.BlockSpec(summ.shape, fixed), pl.BlockSpec(expand.shape, fixed)] + carry_specs,
        out_specs=[big, big, big, big, pl.BlockSpec((tm, LANES), row),
                   pl.BlockSpec((None, None, d, tm), stacked_map),
                   pl.BlockSpec((None, None, d, tm), stacked_map),
                   pl.BlockSpec((None, FOX_HEADS, tm), p_map),
                   pl.BlockSpec((grp, d, seq_s), s_map), pl.BlockSpec((grp, d, seq_s), s_map),
                   pl.BlockSpec((grp, FOX_HEADS, seq_s), s_map)],
        out_shape=[bf16, bf16, bf16, bf16, f32(n, LANES),
                   f32(n_layers, n_p // seq_p, d, seq_p), f32(n_layers, n_p // seq_p, d, seq_p),
                   f32(n_p // seq_p, FOX_HEADS, seq_p),
                   f32((n - n_p) // seq_s, d, seq_s), f32((n - n_p) // seq_s, d, seq_s),
                   f32((n - n_p) // seq_s, FOX_HEADS, seq_s)],
        input_output_aliases=aliases,
        compiler_params=_cparams(1),
        name="inproj_fox",
    )(x, gm, w_all, wf_all, bf, qg, kg, summ, expand, *carried)


def _half_masks(rows):
    lane = lax.broadcasted_iota(jnp.int32, (rows, LANES), 1)
    return lane < HEAD_GROUP, lane >= HEAD_GROUP


NT_DIMS = (((1,), (1,)), ((), ()))


def _row_update(idx, s, pv, m_ref, l_ref, acc_ref):
    m_prev = m_ref[idx]
    m_new = jnp.maximum(m_prev, jnp.max(s, axis=-1, keepdims=True))
    alpha = jnp.exp2(m_prev - m_new)
    p = jnp.exp2(s - m_new)
    l_ref[idx] = alpha * l_ref[idx] + jnp.sum(p, axis=-1, keepdims=True)
    acc_ref[idx] = alpha * acc_ref[idx] + pv(p.astype(BF16))
    m_ref[idx] = m_new


def _init_state(m_ref, l_ref, acc_ref):
    m_ref[...] = jnp.full(m_ref.shape, NEG, F32)
    l_ref[...] = jnp.zeros(l_ref.shape, F32)
    acc_ref[...] = jnp.zeros(acc_ref.shape, F32)


def _diff_lambda(lq1, lk1, lq2, lk2, lambda_init):
    a = jnp.exp(jnp.sum(lq1[...] * lk1[...], axis=-1, keepdims=True))
    b = jnp.exp(jnp.sum(lq2[...] * lk2[...], axis=-1, keepdims=True))
    return a - b + lambda_init


def _alibi_table(slope2, rows, cols, q0, k0):
    i = lax.broadcasted_iota(jnp.int32, (rows, cols), 0)
    j = lax.broadcasted_iota(jnp.int32, (rows, cols), 1)
    qpos = i + q0
    kpos = j + k0
    allowed = jnp.right_shift(kpos, CHUNK_SHIFT) <= jnp.right_shift(qpos, CHUNK_SHIFT)
    bias = slope2 * (i - jnp.abs(qpos - kpos)).astype(F32)
    return jnp.where(allowed, bias, NEG)


BIAS_TERMS = 3


def _bias_lanes(c):
    return HEAD_GROUP * (1 - c)


def _own_lanes(lane, c):
    return (lane >= HEAD_GROUP * c) & (lane < HEAD_GROUP * (c + 1))


def _augment_keys(k_blk, bias, c):
    lane = lax.broadcasted_iota(jnp.int32, k_blk.shape, 1)
    b0 = _bias_lanes(c)
    out = jnp.where(_own_lanes(lane, c), k_blk, jnp.zeros_like(k_blk))
    rest = bias
    for t in range(BIAS_TERMS):
        term = rest.astype(BF16)
        out = jnp.where(lane == b0 + t, term, out)
        rest = rest - term.astype(F32)
    return out


def _augment_queries(q, c):
    lane = lax.broadcasted_iota(jnp.int32, q.shape, 1)
    b0 = _bias_lanes(c)
    ones = jnp.where((lane >= b0) & (lane < b0 + BIAS_TERMS), 1.0, 0.0).astype(q.dtype)
    return jnp.where(_own_lanes(lane, c), q, ones)


ONES_ROWS = 16


def _transpose_values(v_ref, vt_ref, v_cols):
    nblk, n_sets, rows, tk = vt_ref.shape
    r = lax.broadcasted_iota(jnp.int32, (ONES_ROWS, tk), 0)
    tail = jnp.where(r == 0, 1.0, 0.0).astype(vt_ref.dtype)
    for blk in range(nblk):
        vt = v_ref[blk * tk:(blk + 1) * tk, :].astype(F32).T.astype(vt_ref.dtype)
        for s in range(n_sets):
            vt_ref[blk, s] = jnp.concatenate([vt[v_cols[s], :], tail], axis=0)


def _scores_t(ka, qa, q_lo):
    return lax.dot_general(ka, qa[q_lo:, :], NT_DIMS, preferred_element_type=F32)


def _block_t(idx, st, vt, tbl, q_lo, m_ref, acc_ref):
    at = idx + (slice(None), slice(q_lo, None))
    if tbl is not None:
        st = st + tbl
    m_prev = m_ref[at]
    m_new = jnp.maximum(m_prev, jnp.max(st, axis=0, keepdims=True))
    alpha = jnp.exp2(m_prev - m_new)
    pt = jnp.exp2(st - m_new).astype(BF16)
    acc_ref[at] = alpha * acc_ref[at] + jnp.dot(vt, pt, preferred_element_type=F32)
    m_ref[at] = m_new


def _sweep_t(qa, kaug_ref, vt_ref, tbl_ref, m_ref, acc_ref):
    nblk, n_sets, _, tk = vt_ref.shape
    nq = len(qa)
    assert qa[0][0].shape[0] == tk and tk % DIAG_TK == 0

    def step(qi, ki, lo, width, q_lo, tbl):
        rows = slice(ki * tk + lo, ki * tk + lo + width)
        st = [_scores_t(kaug_ref[c, rows, :], qa[qi][c], q_lo) for c in range(2)]
        for c in range(2):
            vt = vt_ref[ki, c % n_sets][:, lo:lo + width]
            _block_t((qi, c), st[c], vt, tbl, q_lo, m_ref, acc_ref)

    for ki in range(nblk):
        for lo in range(0, tk, DIAG_TK):
            step(ki, ki, lo, DIAG_TK, lo, tbl_ref[lo:lo + DIAG_TK, lo:])
        for qi in range(ki + 1, nq):
            step(qi, ki, 0, tk, 0, None)


def _augmented_query_tiles(q_ref, tq):
    return [[_augment_queries(q_ref[qi * tq:(qi + 1) * tq, :], c) for c in range(2)]
            for qi in range(q_ref.shape[0] // tq)]


def _diff_prompt_kernel(lambda_init, sl_ref, lq1, lk1, lq2, lk2, q_ref, k_ref, v_ref, g_ref,
                        o_ref, kaug_ref, vt_ref, m_ref, acc_ref, tbl_ref):
    nq, _, _, tq = m_ref.shape
    nblk, _, _, tk = vt_ref.shape
    hd2 = v_ref.shape[1]
    slope2 = sl_ref[pl.program_id(1)]
    for blk in range(nblk):
        rows = slice(blk * tk, (blk + 1) * tk)
        kpos = lax.broadcasted_iota(jnp.int32, (tk, LANES), 0) + blk * tk
        bias = slope2 * kpos.astype(F32)
        for c in range(2):
            kaug_ref[c, rows, :] = _augment_keys(k_ref[rows, :], bias, c)
    _transpose_values(v_ref, vt_ref, (slice(None),))
    dj = lax.broadcasted_iota(jnp.int32, (tk, tq), 0)
    i = lax.broadcasted_iota(jnp.int32, (tk, tq), 1)
    allowed = jnp.right_shift(dj, CHUNK_SHIFT) <= jnp.right_shift(i, CHUNK_SHIFT)
    ahead = jnp.maximum(dj - i, 0).astype(F32)
    tbl_ref[...] = jnp.where(allowed, (-2.0 * slope2) * ahead, NEG)

    m_ref[...] = jnp.full(m_ref.shape, NEG, F32)
    acc_ref[...] = jnp.zeros(acc_ref.shape, F32)
    _sweep_t(_augmented_query_tiles(q_ref, tq), kaug_ref, vt_ref, tbl_ref, m_ref, acc_ref)
    lam = _diff_lambda(lq1, lk1, lq2, lk2, lambda_init)
    for qi in range(nq):
        num = [acc_ref[qi, c, :hd2, :] / acc_ref[qi, c, hd2:hd2 + 1, :] for c in range(2)]
        ot = num[0] - lam * num[1]
        ms = jnp.mean(ot * ot, axis=0, keepdims=True)
        o = (ot * lax.rsqrt(ms + EPS)).T
        o_ref[qi * tq:(qi + 1) * tq, :] = ((o * g_ref[...]) * (1.0 - lambda_init)).astype(o_ref.dtype)


def _diff_prompt(qb, kb, vb, slopes2, lam_params, subln, n_batch, seq, lambda_init):
    d = qb.shape[1]
    tq = min(ATT_TQ, seq)
    tk = min(ATT_TK, tq)
    nq = seq // tq
    hd2 = 2 * HEAD_GROUP
    small = pl.BlockSpec((1, HEAD_GROUP), lambda b, h: (0, 0))
    whole_seq = pl.BlockSpec((seq, hd2), lambda b, h: (b, h))
    return pl.pallas_call(
        functools.partial(_diff_prompt_kernel, lambda_init),
        grid=(n_batch, DIFF_HEADS),
        in_specs=[pl.BlockSpec(memory_space=pltpu.SMEM), small, small, small, small,
                  whole_seq, whole_seq, whole_seq,
                  pl.BlockSpec((1, hd2), lambda b, h: (0, 0))],
        out_specs=whole_seq,
        out_shape=jax.ShapeDtypeStruct((n_batch * seq, d), BF16),
        scratch_shapes=[pltpu.VMEM((2, seq, hd2), BF16),
                        pltpu.VMEM((seq // tk, 1, hd2 + ONES_ROWS, tk), BF16),
                        pltpu.VMEM((nq, 2, 1, tq), F32),
                        pltpu.VMEM((nq, 2, hd2 + ONES_ROWS, tq), F32), pltpu.VMEM((tk, tq), F32)],
        compiler_params=_cparams(2),
        name="diff_attn_prompt",
    )(slopes2, *lam_params, qb, kb, vb, subln)


def _diff_sample_kernel(lambda_init, past, sl_ref, lq1, lk1, lq2, lk2, q_ref, kp_ref, vp_ref,
                        kn_ref, vn_ref, g_ref, o_ref, m_ref, l_ref, acc_ref):
    tq = q_ref.shape[0]
    n_heads = m_ref.shape[0]
    hd2 = kp_ref.shape[1]
    tk = kp_ref.shape[0] // n_heads
    step = pl.program_id(1)

    @pl.when(step == 0)
    def _():
        _init_state(m_ref, l_ref, acc_ref)

    lo, hi = _half_masks(tq)
    col = lax.broadcasted_iota(jnp.int32, (1, tk), 1) + (step * tk - past)

    def q_streams(h):
        q = q_ref[:, h * hd2:(h + 1) * hd2]
        zero = jnp.zeros_like(q)
        return jnp.where(lo, q, zero), jnp.where(hi, q, zero)

    scores = {}
    for h in range(n_heads):
        k_h = kp_ref[pl.ds(h, tk, stride=n_heads), :].astype(BF16)
        bias = sl_ref[h] * col.astype(F32)
        for c, q_c in enumerate(q_streams(h)):
            scores[h, c] = lax.dot_general(q_c, k_h, NT_DIMS, preferred_element_type=F32) + bias
    for h in range(n_heads):
        v_h = vp_ref[pl.ds(h, tk, stride=n_heads), :].astype(BF16)
        for c in range(2):
            _row_update((h, c), scores[h, c], lambda p: jnp.dot(p, v_h, preferred_element_type=F32),
                        m_ref, l_ref, acc_ref)

    @pl.when(step == pl.num_programs(1) - 1)
    def _():
        lam = _diff_lambda(lq1, lk1, lq2, lk2, lambda_init)
        new_scores = {}
        for h in range(n_heads):
            k_h = kn_ref[:, h * hd2:(h + 1) * hd2]
            bias = _alibi_table(sl_ref[h], tq, tq, past, past)
            for c, q_c in enumerate(q_streams(h)):
                new_scores[h, c] = lax.dot_general(q_c, k_h, NT_DIMS, preferred_element_type=F32) + bias
        for h in range(n_heads):
            cols = slice(h * hd2, (h + 1) * hd2)
            v_h = vn_ref[:, cols]
            for c in range(2):
                _row_update((h, c), new_scores[h, c], lambda p: jnp.dot(p, v_h, preferred_element_type=F32),
                            m_ref, l_ref, acc_ref)
            o = acc_ref[h, 0] / l_ref[h, 0] - lam * (acc_ref[h, 1] / l_ref[h, 1])
            ms = jnp.mean(o * o, axis=-1, keepdims=True)
            o_ref[:, cols] = ((o * lax.rsqrt(ms + EPS) * g_ref[...]) * (1.0 - lambda_init)).astype(o_ref.dtype)


def _diff_sample(qb, kb, vb, cache_k, cache_v, layer, slopes2, lam_params, subln, row0, n_batch, seq,
                 lambda_init):
    d = qb.shape[1]
    past, n_heads, hd2 = cache_k.shape[2:]
    tk = SAMPLE_TK if past % SAMPLE_TK == 0 else past
    blk0 = row0 // seq
    rows = lambda a: a.reshape(a.shape[:2] + (past * n_heads, hd2))
    cache_k, cache_v = rows(cache_k), rows(cache_v)
    small = pl.BlockSpec((1, HEAD_GROUP), lambda b, c: (0, 0))
    new = pl.BlockSpec((seq, d), lambda b, c: (blk0 + b, 0))
    old = pl.BlockSpec((None, None, tk * n_heads, hd2), lambda b, c: (layer, b, c, 0))
    return pl.pallas_call(
        functools.partial(_diff_sample_kernel, lambda_init, past),
        grid=(n_batch, past // tk),
        in_specs=[pl.BlockSpec(memory_space=pltpu.SMEM), small, small, small, small,
                  new, old, old, new, new, pl.BlockSpec((1, hd2), lambda b, c: (0, 0))],
        out_specs=pl.BlockSpec((seq, d), lambda b, c: (b, 0)),
        out_shape=jax.ShapeDtypeStruct((n_batch * seq, d), BF16),
        scratch_shapes=[pltpu.VMEM((n_heads, 2, seq, 1), F32), pltpu.VMEM((n_heads, 2, seq, 1), F32),
                        pltpu.VMEM((n_heads, 2, seq, hd2), F32)],
        compiler_params=_cparams(2),
        name="diff_attn_sample",
    )(slopes2, *lam_params, qb, cache_k, cache_v, kb, vb, subln)


def _causal_table(rows, cols):
    i = lax.broadcasted_iota(jnp.int32, (rows, cols), 0)
    j = lax.broadcasted_iota(jnp.int32, (rows, cols), 1)
    return jnp.where(j <= i, 0.0, NEG).astype(F32)


def _fox_prompt_kernel(q_ref, k_ref, v_ref, lf_ref, sg_ref, o_ref,
                       cum_ref, kaug_ref, vt_ref, m_ref, acc_ref, tbl_ref):
    nq, _, _, tq = m_ref.shape
    nblk, _, _, tk = vt_ref.shape
    pair = pl.program_id(1)

    @pl.when(pair == 0)
    def _():
        r = lax.broadcasted_iota(jnp.int32, (tk, tk), 0)
        c = lax.broadcasted_iota(jnp.int32, (tk, tk), 1)
        tril = jnp.where(c <= r, 1.0, 0.0).astype(BF16)
        carry = jnp.zeros((1, LANES), F32)
        for blk in range(nblk):
            x = lf_ref[blk * tk:(blk + 1) * tk, :]
            hi = x.astype(BF16)
            lo = (x - hi.astype(F32)).astype(BF16)
            part = (jnp.dot(tril, hi, preferred_element_type=F32)
                    + jnp.dot(tril, lo, preferred_element_type=F32)) + carry
            cum_ref[blk * tk:(blk + 1) * tk, :] = part * (-LOG2E)
            carry = carry + jnp.sum(x, axis=0, keepdims=True)

    lane = lax.broadcasted_iota(jnp.int32, (tk, LANES), 1)
    for blk in range(nblk):
        rows = slice(blk * tk, (blk + 1) * tk)
        cum = cum_ref[rows, :]
        for c in range(2):
            col = jnp.sum(jnp.where(lane == 2 * pair + c, cum, 0.0), axis=-1, keepdims=True)
            bias = jnp.broadcast_to(col, (tk, LANES))
            kaug_ref[c, rows, :] = _augment_keys(k_ref[rows, :], bias, c)
    hd = HEAD_GROUP
    _transpose_values(v_ref, vt_ref, (slice(0, hd), slice(hd, 2 * hd)))
    dj = lax.broadcasted_iota(jnp.int32, (tk, tq), 0)
    i = lax.broadcasted_iota(jnp.int32, (tk, tq), 1)
    tbl_ref[...] = jnp.where(dj <= i, 0.0, NEG).astype(F32)

    m_ref[...] = jnp.full(m_ref.shape, NEG, F32)
    acc_ref[...] = jnp.zeros(acc_ref.shape, F32)
    _sweep_t(_augmented_query_tiles(q_ref, tq), kaug_ref, vt_ref, tbl_ref, m_ref, acc_ref)
    for qi in range(nq):
        rows = slice(qi * tq, (qi + 1) * tq)
        ot = jnp.concatenate([acc_ref[qi, c, :hd, :] / acc_ref[qi, c, hd:hd + 1, :] for c in range(2)],
                             axis=0)
        o_ref[rows, :] = (ot.T * sg_ref[rows, :].astype(F32)).astype(o_ref.dtype)


def _fox_prompt(qb, kb, vb, lf, sg, n_batch, seq):
    d = qb.shape[1]
    tq = min(ATT_TQ, seq)
    tk = min(ATT_TK, tq)
    nq = seq // tq
    w = LANES
    whole_seq = pl.BlockSpec((seq, w), lambda b, p: (b, p))
    return pl.pallas_call(
        _fox_prompt_kernel,
        grid=(n_batch, d // w),
        in_specs=[whole_seq, whole_seq, whole_seq,
                  pl.BlockSpec((seq, LANES), lambda b, p: (b, 0)),
                  whole_seq],
        out_specs=whole_seq,
        out_shape=jax.ShapeDtypeStruct((n_batch * seq, d), BF16),
        scratch_shapes=[pltpu.VMEM((seq, LANES), F32),
                        pltpu.VMEM((2, seq, w), BF16),
                        pltpu.VMEM((seq // tk, 2, HEAD_GROUP + ONES_ROWS, tk), BF16),
                        pltpu.VMEM((nq, 2, 1, tq), F32),
                        pltpu.VMEM((nq, 2, HEAD_GROUP + ONES_ROWS, tq), F32), pltpu.VMEM((tk, tq), F32)],
        compiler_params=_cparams(2),
        name="fox_attn_prompt",
    )(qb, kb, vb, lf, sg)


def _fox_sample_kernel(q_ref, kp_ref, vp_ref, kn_ref, vn_ref, cp_ref, cn_ref, sg_ref,
                       o_ref, m_ref, l_ref, acc_ref):
    tq = q_ref.shape[0]
    n_heads, hd, tk = kp_ref.shape
    step = pl.program_id(1)

    @pl.when(step == 0)
    def _():
        _init_state(m_ref, l_ref, acc_ref)

    scores = []
    for h in range(n_heads):
        q_h = q_ref[:, h * hd:(h + 1) * hd]
        kt_h = kp_ref[h].astype(BF16)
        scores.append(jnp.dot(q_h, kt_h, preferred_element_type=F32) + cp_ref[h:h + 1, :])
    for h in range(n_heads):
        vt_h = vp_ref[h].astype(BF16)
        _row_update(h, scores[h], lambda p: lax.dot_general(p, vt_h, NT_DIMS, preferred_element_type=F32),
                    m_ref, l_ref, acc_ref)

    @pl.when(step == pl.num_programs(1) - 1)
    def _():
        tbl = _causal_table(tq, tq)
        new_scores = []
        for h in range(n_heads):
            cols = slice(h * hd, (h + 1) * hd)
            new_scores.append(lax.dot_general(q_ref[:, cols], kn_ref[:, cols], NT_DIMS,
                                              preferred_element_type=F32) + (tbl + cn_ref[h:h + 1, :]))
        for h in range(n_heads):
            cols = slice(h * hd, (h + 1) * hd)
            v_h = vn_ref[:, cols]
            _row_update(h, new_scores[h], lambda p: jnp.dot(p, v_h, preferred_element_type=F32),
                        m_ref, l_ref, acc_ref)
            o = acc_ref[h] / l_ref[h]
            o_ref[:, cols] = (o * sg_ref[:, cols].astype(F32)).astype(o_ref.dtype)


def _fox_sample(qb, kb, vb, cache_kt, cache_vt, layer, c_past, c_new, sg, row0, n_batch, seq):
    d = qb.shape[1]
    n_heads, hd, past = cache_kt.shape[2:]
    tk = SAMPLE_TK if past % SAMPLE_TK == 0 else past
    blk0 = row0 // seq
    new = pl.BlockSpec((seq, d), lambda b, c: (blk0 + b, 0))
    old = pl.BlockSpec((None, None, n_heads, hd, tk), lambda b, c: (layer, b, 0, 0, c))
    return pl.pallas_call(
        _fox_sample_kernel,
        grid=(n_batch, past // tk),
        in_specs=[new, old, old, new, new,
                  pl.BlockSpec((None, n_heads, tk), lambda b, c: (b, 0, c)),
                  pl.BlockSpec((None, n_heads, seq), lambda b, c: (b, 0, 0)),
                  new],
        out_specs=pl.BlockSpec((seq, d), lambda b, c: (b, 0)),
        out_shape=jax.ShapeDtypeStruct((n_batch * seq, d), BF16),
        scratch_shapes=[pltpu.VMEM((n_heads, seq, 1), F32), pltpu.VMEM((n_heads, seq, 1), F32),
                        pltpu.VMEM((n_heads, seq, hd), F32)],
        compiler_params=_cparams(2),
        name="fox_attn_sample",
    )(qb, cache_kt, cache_vt, kb, vb, c_past, c_new, sg)


def _split_dot(x, mat):
    hi = x.astype(BF16)
    lo = (x - hi.astype(F32)).astype(BF16)
    return (jnp.dot(hi, mat, preferred_element_type=F32) + jnp.dot(lo, mat, preferred_element_type=F32))


def _coff_kernel(suffix, tk, x_ref, o_ref):
    nblk = x_ref.shape[1] // tk
    r = lax.broadcasted_iota(jnp.int32, (tk, tk), 0)
    c = lax.broadcasted_iota(jnp.int32, (tk, tk), 1)
    mat = jnp.where((r > c) if suffix else (r <= c), 1.0, 0.0).astype(BF16)
    carry = jnp.zeros((x_ref.shape[0], 1), F32)
    order = range(nblk - 1, -1, -1) if suffix else range(nblk)
    for blk in order:
        cols = slice(blk * tk, (blk + 1) * tk)
        x = x_ref[:, cols]
        part = _split_dot(x, mat) + carry
        o_ref[:, cols] = (part if suffix else -part) * LOG2E
        carry = carry + jnp.sum(x, axis=-1, keepdims=True)


def _coff(xt, tk, suffix):
    n_batch, n_head, length = xt.shape
    spec = pl.BlockSpec((None, n_head, length), lambda b: (b, 0, 0))
    return pl.pallas_call(
        functools.partial(_coff_kernel, suffix, tk),
        grid=(n_batch,),
        in_specs=[spec],
        out_specs=spec,
        out_shape=jax.ShapeDtypeStruct((n_batch, n_head, length), F32),
        compiler_params=_cparams(1),
        name="fox_coff_suffix" if suffix else "fox_coff_prefix",
    )(xt)


def _outproj_kernel(tiles_p, op_ref, os_ref, x_ref, w_ref, gf_ref, wr_hi_ref, wr_lo_ref, br_ref,
                    xo_ref, meta_ref):
    o = jnp.where(pl.program_id(0) < tiles_p, op_ref[...], os_ref[...])
    xn = x_ref[...] + jnp.dot(o, w_ref[...], preferred_element_type=F32)
    xo_ref[...] = xn
    t = _rms_rows(xn, gf_ref[...])
    t_hi = t.astype(BF16)
    t_lo = (t - t_hi.astype(F32)).astype(BF16)
    logits = (jnp.dot(t_hi, wr_hi_ref[...], preferred_element_type=F32)
              + jnp.dot(t_lo, wr_hi_ref[...], preferred_element_type=F32)
              + jnp.dot(t_hi, wr_lo_ref[...], preferred_element_type=F32)) + br_ref[...]
    rows = logits.shape[0]
    lane = lax.broadcasted_iota(jnp.int32, (rows, LANES), 1).astype(F32)
    big = float(LANES)

    def first_argmax(vals, vmax):
        return jnp.min(jnp.where(vals == vmax, lane, big), axis=-1, keepdims=True)

    gl = jnp.where(lane < N_GROUPS, logits, NEG)
    gmax = jnp.max(gl, axis=-1, keepdims=True)
    gsum = jnp.sum(jnp.where(lane < N_GROUPS, jnp.exp(logits - gmax), 0.0), axis=-1, keepdims=True)
    p_top = 1.0 / gsum
    g_idx = first_argmax(gl, gmax)
    base = N_GROUPS + EXPERTS_PER_GROUP * g_idx
    el = jnp.where((lane >= base) & (lane < base + EXPERTS_PER_GROUP), logits, NEG)
    v1 = jnp.max(el, axis=-1, keepdims=True)
    i1 = first_argmax(el, v1)
    el2 = jnp.where(lane == i1, NEG, el)
    v2 = jnp.max(el2, axis=-1, keepdims=True)
    i2 = first_argmax(el2, v2)
    e2 = jnp.exp(v2 - v1)
    w1 = p_top / (1.0 + e2)
    w2 = p_top * e2 / (1.0 + e2)
    a1 = i1 - base
    a2 = i2 - base
    lo = jnp.minimum(a1, a2)
    hi = jnp.maximum(a1, a2)
    wa = jnp.where(a1 < a2, w1, w2)
    wb = jnp.where(a1 < a2, w2, w1)
    pair = jnp.where(lo == 0.0, hi - 1.0, jnp.where(lo == 1.0, hi + 1.0, 5.0))
    bucket = g_idx * N_PAIRS + pair
    meta_ref[...] = jnp.where(lane == 0.0, bucket, jnp.where(lane == 1.0, wa, jnp.where(lane == 2.0, wb, 0.0)))


def _outproj(o_p, o_s, x, w_all, layer, gf, wr_hi, wr_lo, br):
    n, d = x.shape
    tm = TOK_TILE
    tiles_p = o_p.shape[0] // tm
    row = lambda i: (i, 0)
    fixed = lambda i: (0, 0)
    big = pl.BlockSpec((tm, d), row)
    return pl.pallas_call(
        functools.partial(_outproj_kernel, tiles_p),
        grid=(n // tm,),
        in_specs=[pl.BlockSpec((tm, d), lambda i: (jnp.minimum(i, tiles_p - 1), 0)),
                  pl.BlockSpec((tm, d), lambda i: (jnp.maximum(i - tiles_p, 0), 0)),
                  big,
                  pl.BlockSpec((None, d, d), lambda i: (layer, 0, 0)),
                  pl.BlockSpec((1, d), fixed),
                  pl.BlockSpec((d, LANES), fixed), pl.BlockSpec((d, LANES), fixed),
                  pl.BlockSpec((1, LANES), fixed)],
        out_specs=[big, pl.BlockSpec((tm, LANES), row)],
        out_shape=[jax.ShapeDtypeStruct((n, d), F32), jax.ShapeDtypeStruct((n, LANES), F32)],
        compiler_params=_cparams(1),
        name="outproj_router",
    )(o_p, o_s, x, w_all, gf, wr_hi, wr_lo, br)


def _moe_kernel(tile_ref, ea_ref, eb_ref, lo_ref, hi_ref, first_ref,
                xs_ref, ms_ref, gf_ref, wg_a, wu_a, wd_a, wg_b, wu_b, wd_b, out_ref):
    s = pl.program_id(0)
    lo = lo_ref[s]
    hi = hi_ref[s]

    @pl.when(first_ref[s] == 1)
    def _():
        out_ref[...] = xs_ref[...]

    @pl.when(hi > lo)
    def _():
        t = _rms_rows(xs_ref[...], gf_ref[...]).astype(BF16)
        rows = t.shape[0]
        r = lax.broadcasted_iota(jnp.int32, (rows, 1), 0)
        inside = (r >= lo) & (r < hi)
        ms = ms_ref[...]
        wa = jnp.where(inside, ms[:, 1:2], 0.0)
        wb = jnp.where(inside, ms[:, 2:3], 0.0)

        def hidden(g, u):
            return (g / (1.0 + jnp.exp(-g)) * u).astype(BF16)

        g_a = jnp.dot(t, wg_a[...], preferred_element_type=F32)
        u_a = jnp.dot(t, wu_a[...], preferred_element_type=F32)
        g_b = jnp.dot(t, wg_b[...], preferred_element_type=F32)
        u_b = jnp.dot(t, wu_b[...], preferred_element_type=F32)
        y_a = jnp.dot(hidden(g_a, u_a), wd_a[...], preferred_element_type=F32)
        y_b = jnp.dot(hidden(g_b, u_b), wd_b[...], preferred_element_type=F32)
        out_ref[...] += wa * y_a + wb * y_b


def _moe(sched, xs, ms, gf, wg_all, wu_all, wd_all, layer):
    n, d = xs.shape
    de = wd_all.shape[2]
    n_steps = sched[0].shape[0]
    tile_map = lambda s, tile, ea, eb, lo, hi, first: (tile[s], 0)
    fixed = lambda s, *_: (0, 0)
    wa_map = lambda s, tile, ea, eb, lo, hi, first: (layer, ea[s], 0, 0)
    wb_map = lambda s, tile, ea, eb, lo, hi, first: (layer, eb[s], 0, 0)
    grid_spec = pltpu.PrefetchScalarGridSpec(
        num_scalar_prefetch=6,
        grid=(n_steps,),
        in_specs=[pl.BlockSpec((MOE_TILE, d), tile_map),
                  pl.BlockSpec((MOE_TILE, ms.shape[1]), tile_map),
                  pl.BlockSpec((1, d), fixed),
                  pl.BlockSpec((None, None, d, de), wa_map),
                  pl.BlockSpec((None, None, d, de), wa_map),
                  pl.BlockSpec((None, None, de, d), wa_map),
                  pl.BlockSpec((None, None, d, de), wb_map),
                  pl.BlockSpec((None, None, d, de), wb_map),
                  pl.BlockSpec((None, None, de, d), wb_map)],
        out_specs=pl.BlockSpec((MOE_TILE, d), tile_map),
    )
    return pl.pallas_call(
        _moe_kernel,
        grid_spec=grid_spec,
        out_shape=jax.ShapeDtypeStruct((n, d), F32),
        compiler_params=_cparams(1),
        name="moe_experts",
    )(*sched, xs, ms, gf, wg_all, wu_all, wd_all, wg_all, wu_all, wd_all)


def _moe_schedule(bucket, n):
    n_tiles = n // MOE_TILE
    n_steps = n_tiles + N_BUCKETS - 1
    perm = jnp.argsort(bucket, stable=True).astype(jnp.int32)
    sorted_b = bucket[perm]
    inv = jnp.argsort(perm).astype(jnp.int32)
    offs = jnp.searchsorted(sorted_b, jnp.arange(N_BUCKETS + 1, dtype=jnp.int32), side="left")
    offs = offs.astype(jnp.int32)
    fb = sorted_b[::MOE_TILE]
    lb = sorted_b[MOE_TILE - 1::MOE_TILE]
    per_tile = lb - fb + 1
    starts = jnp.cumsum(per_tile) - per_tile
    total = jnp.sum(per_tile)
    s = jnp.arange(n_steps, dtype=jnp.int32)
    tile = jnp.clip(jnp.searchsorted(starts, s, side="right").astype(jnp.int32) - 1, 0, n_tiles - 1)
    valid = s < total
    bkt = jnp.where(valid, fb[tile] + (s - starts[tile]), lb[n_tiles - 1])
    lo = jnp.clip(offs[bkt] - tile * MOE_TILE, 0, MOE_TILE)
    hi = jnp.clip(offs[bkt + 1] - tile * MOE_TILE, 0, MOE_TILE)
    lo = jnp.where(valid, lo, 0)
    hi = jnp.where(valid, hi, 0)
    first = (valid & (s == starts[tile])).astype(jnp.int32)
    grp = bkt // N_PAIRS
    ea = grp * EXPERTS_PER_GROUP + jnp.asarray(PAIR_A)[bkt % N_PAIRS]
    eb = grp * EXPERTS_PER_GROUP + jnp.asarray(PAIR_B)[bkt % N_PAIRS]
    return perm, inv, (tile, ea.astype(jnp.int32), eb.astype(jnp.int32), lo.astype(jnp.int32),
                       hi.astype(jnp.int32), first)


def _tile_gain(g, d):
    return jnp.tile(g.astype(F32), d // g.shape[0]).reshape(1, d)


def kernel(x_prompt, x_sample, cache_diff_k, cache_diff_v, cache_fox_k, cache_fox_v, cache_fox_logf, norm_mix, norm_ffn, diff_w_in, diff_w_out, diff_q_norm, diff_k_norm, diff_lambda_q1, diff_lambda_k1, diff_lambda_q2, diff_lambda_k2, diff_subln, fox_w_in, fox_b_f, fox_w_out, fox_q_norm, fox_k_norm, moe_w_group, moe_b_group, moe_w_expert, moe_b_expert, moe_w_gate, moe_w_up, moe_w_down):
    bp, sp, d = x_prompt.shape
    bs, ss, _ = x_sample.shape
    n_p = bp * sp
    n_s = bs * ss
    n = n_p + n_s
    depth = norm_mix.shape[0]
    past = cache_diff_k.shape[2]
    assert n % TOK_TILE == 0 and n % MOE_TILE == 0 and n_p % TOK_TILE == 0
    assert d % LANES == 0 and sp % min(ATT_TQ, sp) == 0 and n_p % ss == 0
    assert DIFF_HEADS * 2 * HEAD_GROUP == d and FOX_HEADS * HEAD_GROUP == d

    summ, expand = _group_mats(d)
    x = jnp.concatenate([x_prompt.reshape(n_p, d), x_sample.reshape(n_s, d)], axis=0)

    diff_w_in_b = diff_w_in.astype(BF16)
    diff_w_out_b = diff_w_out.astype(BF16)
    fox_w_main_b = fox_w_in[:, :, :4 * d].astype(BF16)
    fox_w_f_b = jnp.pad(fox_w_in[:, :, 4 * d:], ((0, 0), (0, 0), (0, LANES - FOX_HEADS))).astype(BF16)
    fox_w_out_b = fox_w_out.astype(BF16)
    wg_b = moe_w_gate.astype(BF16)
    wu_b = moe_w_up.astype(BF16)
    wd_b = moe_w_down.astype(BF16)
    w_router = jnp.pad(jnp.concatenate([moe_w_group, moe_w_expert], axis=-1),
                       ((0, 0), (0, 0), (0, LANES - N_GROUPS - N_EXPERTS)))
    wr_hi = w_router.astype(BF16)
    wr_lo = (w_router - wr_hi.astype(F32)).astype(BF16)
    b_router = jnp.pad(jnp.concatenate([moe_b_group, moe_b_expert], axis=-1),
                       ((0, 0), (0, LANES - N_GROUPS - N_EXPERTS)))
    slopes2 = jnp.asarray(2.0 ** (-8.0 * np.arange(1, DIFF_HEADS + 1) / DIFF_HEADS) * LOG2E, F32)
    cache_fkt = jnp.transpose(cache_fox_k, (0, 1, 3, 4, 2))
    cache_fvt = jnp.transpose(cache_fox_v, (0, 1, 3, 4, 2))
    cache_flt = jnp.transpose(cache_fox_logf.astype(F32), (0, 1, 3, 2))

    def from_feature_major(a, seq):
        return a.reshape(a.shape[0], FOX_HEADS, HEAD_GROUP, seq).transpose(0, 3, 1, 2)

    outs = {name: [] for name in ("dks", "dvs", "flp", "fks", "fvs", "fls")}
    n_diff, n_fox = diff_w_in.shape[0], fox_w_in.shape[0]
    diff_cache_p = tuple(jnp.zeros((n_diff, n_p * DIFF_HEADS, LANES), F32) for _ in range(2))
    fox_cache_p = tuple(jnp.zeros((n_fox, bp, d, sp), F32) for _ in range(2))
    for i in range(depth):
        j = i // 2
        gm = norm_mix[i].reshape(1, d)
        if i % 2 == 0:
            lambda_init = 0.8 - 0.6 * math.exp(-0.3 * i)
            qb, kb, vb, k_p, v_p, k_s, v_s = _inproj_diff(
                x, gm, diff_w_in_b, j, n_diff, _tile_gain(diff_q_norm[j], d), _tile_gain(diff_k_norm[j], d),
                summ, expand, n_p, diff_cache_p)
            diff_cache_p = (k_p, v_p)
            lam_params = [p[j].reshape(1, HEAD_GROUP).astype(F32) for p in
                          (diff_lambda_q1, diff_lambda_k1, diff_lambda_q2, diff_lambda_k2)]
            subln = diff_subln[j].reshape(1, 2 * HEAD_GROUP)
            o_p = _diff_prompt(qb, kb, vb, slopes2, lam_params, subln, bp, sp, lambda_init)
            o_s = _diff_sample(qb, kb, vb, cache_diff_k, cache_diff_v, j, slopes2, lam_params, subln,
                               n_p, bs, ss, lambda_init)
            w_out_b = diff_w_out_b
            outs["dks"].append(k_s.reshape(bs, ss, DIFF_HEADS, 2 * HEAD_GROUP))
            outs["dvs"].append(v_s.reshape(bs, ss, DIFF_HEADS, 2 * HEAD_GROUP))
        else:
            bf = jnp.pad(fox_b_f[j], (0, LANES - FOX_HEADS)).reshape(1, LANES)
            qb, kb, vb, sg, lf, kt_p, vt_p, lt_p, kt_s, vt_s, lt_s = _inproj_fox(
                x, gm, fox_w_main_b, fox_w_f_b, j, n_fox, bf, _tile_gain(fox_q_norm[j], d),
                _tile_gain(fox_k_norm[j], d), summ, expand, n_p, sp, ss, fox_cache_p)
            fox_cache_p = (kt_p, vt_p)
            tk_c = SAMPLE_TK if past % SAMPLE_TK == 0 else past
            c_new = _coff(lt_s, ss, False)
            c_past = _coff(cache_flt[j], tk_c, True)
            o_p = _fox_prompt(qb, kb, vb, lf, sg, bp, sp)
            o_s = _fox_sample(qb, kb, vb, cache_fkt, cache_fvt, j, c_past, c_new, sg, n_p, bs, ss)
            w_out_b = fox_w_out_b
            outs["flp"].append(lt_p.transpose(0, 2, 1))
            outs["fks"].append(from_feature_major(kt_s, ss))
            outs["fvs"].append(from_feature_major(vt_s, ss))
            outs["fls"].append(lt_s.transpose(0, 2, 1))
        gf = norm_ffn[i].reshape(1, d)
        x, meta = _outproj(o_p, o_s, x, w_out_b, j, gf, wr_hi[i], wr_lo[i], b_router[i].reshape(1, LANES))
        bucket = meta[:, 0].astype(jnp.int32)
        perm, inv, sched = _moe_schedule(bucket, n)
        xs = x[perm]
        ms = meta[:, :8][perm]
        ys = _moe(sched, xs, ms, gf, wg_b, wu_b, wd_b, i)
        if i + 1 < depth:
            x = ys[inv]

    y_prompt = ys[inv[:n_p]].reshape(bp, sp, d)
    y_sample = ys[inv[n_p:]].reshape(bs, ss, d)
    stk = lambda name: jnp.stack(outs[name])
    dkp, dvp = (a.reshape(n_diff, bp, sp, DIFF_HEADS, 2 * HEAD_GROUP) for a in diff_cache_p)
    fkp, fvp = (a.reshape(n_fox, bp, FOX_HEADS, HEAD_GROUP, sp).transpose(0, 1, 4, 2, 3)
                for a in fox_cache_p)
    return (y_prompt, y_sample, dkp, dvp, fkp, fvp, stk("flp"),
            stk("dks"), stk("dvs"), stk("fks"), stk("fvs"), stk("fls"))
```

```python
import functools
import math

import numpy as np
import jax
import jax.numpy as jnp
from jax import lax
from jax.experimental import pallas as pl
from jax.experimental.pallas import tpu as pltpu

F32 = jnp.float32
BF16 = jnp.bfloat16

CHUNK_SHIFT = 6
DIFF_HEADS = 8
FOX_HEADS = 16
HEAD_GROUP = 64
N_GROUPS = 4
EXPERTS_PER_GROUP = 4
N_EXPERTS = N_GROUPS * EXPERTS_PER_GROUP
N_PAIRS = 6
N_BUCKETS = N_GROUPS * N_PAIRS
EPS = 1e-6
NEG = -1e30
LOG2E = math.log2(math.e)
QSCALE = (HEAD_GROUP ** -0.5) * LOG2E

LANES = 128
TOK_TILE = 512
ATT_TQ = 512
ATT_TK = 512
DIAG_TK = 512
SAMPLE_TK = 2048
MOE_TILE = 512
VMEM_LIMIT = 56 * 1024 * 1024

PAIR_A = np.array([0, 0, 0, 1, 1, 2], np.int32)
PAIR_B = np.array([1, 2, 3, 2, 3, 3], np.int32)


def _cparams(n_axes):
    return pltpu.CompilerParams(dimension_semantics=("arbitrary",) * n_axes,
                                vmem_limit_bytes=VMEM_LIMIT)


def _rms_rows(x, gain):
    ms = jnp.mean(x * x, axis=-1, keepdims=True)
    return x * lax.rsqrt(ms + EPS) * gain


def _group_norm(z, gain, sum_ref, exp_ref):
    ms = jnp.dot((z * z).astype(BF16), sum_ref[...], preferred_element_type=F32)
    r = lax.rsqrt(ms + EPS)
    r_hi = r.astype(BF16)
    r_lo = (r - r_hi.astype(F32)).astype(BF16)
    rb = jnp.dot(jnp.concatenate([r_hi, r_lo], axis=-1), exp_ref[...], preferred_element_type=F32)
    return z * rb * gain


def _group_mats(d):
    n_g = d // HEAD_GROUP
    col = np.arange(d) // HEAD_GROUP
    summ = np.zeros((d, LANES), np.float32)
    summ[np.arange(d), col] = 1.0 / HEAD_GROUP
    expand = np.zeros((2 * LANES, d), np.float32)
    expand[col, np.arange(d)] = 1.0
    expand[LANES + col, np.arange(d)] = 1.0
    assert n_g <= LANES
    return jnp.asarray(summ, BF16), jnp.asarray(expand, BF16)


def _put_heads(ref, val):
    rows = val.shape[0]
    n_heads = ref.shape[0] // rows
    for h in range(n_heads):
        ref[pl.ds(h, rows, stride=n_heads), :] = val[:, h * LANES:(h + 1) * LANES]


def _inproj_diff_kernel(tiles_p, n_carried, x_ref, gm_ref, w_ref, qg_ref, kg_ref, sum_ref, exp_ref, *refs):
    qb_ref, kb_ref, vb_ref, kp_ref, vp_ref, ks_ref, vs_ref = refs[n_carried:]
    d = x_ref.shape[1]
    i = pl.program_id(0)
    h = _rms_rows(x_ref[...], gm_ref[...]).astype(BF16)
    q = jnp.dot(h, w_ref[:, 0:d], preferred_element_type=F32)
    k = jnp.dot(h, w_ref[:, d:2 * d], preferred_element_type=F32)
    v = jnp.dot(h, w_ref[:, 2 * d:3 * d], preferred_element_type=F32)
    qb_ref[...] = (_group_norm(q, qg_ref[...], sum_ref, exp_ref) * QSCALE).astype(BF16)
    kn = _group_norm(k, kg_ref[...], sum_ref, exp_ref)
    kb_ref[...] = kn.astype(BF16)
    vb_ref[...] = v.astype(BF16)

    @pl.when(i < tiles_p)
    def _():
        _put_heads(kp_ref, kn)
        _put_heads(vp_ref, v)

    @pl.when(i >= tiles_p)
    def _():
        _put_heads(ks_ref, kn)
        _put_heads(vs_ref, v)


def _carry_specs(carried, n_fixed_inputs, first_out):
    specs = [pl.BlockSpec(memory_space=pl.ANY)] * len(carried)
    aliases = {n_fixed_inputs + k: first_out + k for k in range(len(carried))}
    return specs, aliases


def _inproj_diff(x, gm, w_all, layer, n_layers, qg, kg, summ, expand, n_p, carried):
    n, d = x.shape
    tm = TOK_TILE
    tiles_p = n_p // tm
    row = lambda i: (i, 0)
    fixed = lambda i: (0, 0)
    big = pl.BlockSpec((tm, d), row)
    heads = pl.BlockSpec((None, tm * DIFF_HEADS, LANES), lambda i: (layer, jnp.minimum(i, tiles_p - 1), 0))
    heads_s = pl.BlockSpec((tm * DIFF_HEADS, LANES), lambda i: (jnp.maximum(i - tiles_p, 0), 0))
    cache_p = jax.ShapeDtypeStruct((n_layers, n_p * DIFF_HEADS, LANES), F32)
    cache_s = jax.ShapeDtypeStruct(((n - n_p) * DIFF_HEADS, LANES), F32)
    carry_specs, aliases = _carry_specs(carried, 7, 3)
    return pl.pallas_call(
        functools.partial(_inproj_diff_kernel, tiles_p, len(carried)),
        grid=(n // tm,),
        in_specs=[big,
                  pl.BlockSpec((1, d), fixed),
                  pl.BlockSpec((None, d, 3 * d), lambda i: (layer, 0, 0)),
                  pl.BlockSpec((1, d), fixed), pl.BlockSpec((1, d), fixed),
                  pl.BlockSpec(summ.shape, fixed), pl.BlockSpec(expand.shape, fixed)] + carry_specs,
        out_specs=[big, big, big, heads, heads, heads_s, heads_s],
        out_shape=[jax.ShapeDtypeStruct((n, d), BF16), jax.ShapeDtypeStruct((n, d), BF16),
                   jax.ShapeDtypeStruct((n, d), BF16), cache_p, cache_p, cache_s, cache_s],
        input_output_aliases=aliases,
        compiler_params=_cparams(1),
        name="inproj_diff",
    )(x, gm, w_all, qg, kg, summ, expand, *carried)


def _inproj_fox_kernel(tiles_p, n_carried, x_ref, gm_ref, w_ref, wf_ref, bf_ref, qg_ref, kg_ref,
                       sum_ref, exp_ref, *refs):
    (qb_ref, kb_ref, vb_ref, sg_ref, lf_ref,
     kp_ref, vp_ref, lp_ref, ks_ref, vs_ref, ls_ref) = refs[n_carried:]
    d = x_ref.shape[1]
    i = pl.program_id(0)
    h = _rms_rows(x_ref[...], gm_ref[...]).astype(BF16)
    q = jnp.dot(h, w_ref[:, 0:d], preferred_element_type=F32)
    qb_ref[...] = (_group_norm(q, qg_ref[...], sum_ref, exp_ref) * QSCALE).astype(BF16)
    k = jnp.dot(h, w_ref[:, d:2 * d], preferred_element_type=F32)
    kn = _group_norm(k, kg_ref[...], sum_ref, exp_ref)
    kb_ref[...] = kn.astype(BF16)
    v = jnp.dot(h, w_ref[:, 2 * d:3 * d], preferred_element_type=F32)
    vb_ref[...] = v.astype(BF16)
    g = jnp.dot(h, w_ref[:, 3 * d:4 * d], preferred_element_type=F32)
    sg_ref[...] = (1.0 / (1.0 + jnp.exp(-g))).astype(BF16)
    f = jnp.dot(h, wf_ref[...], preferred_element_type=F32) + bf_ref[...]
    lf = jnp.minimum(f, 0.0) - jnp.log(1.0 + jnp.exp(-jnp.abs(f)))
    lf_ref[...] = lf
    n_heads = lp_ref.shape[0]

    @pl.when(i < tiles_p)
    def _():
        kp_ref[...] = kn.T
        vp_ref[...] = v.T
        lp_ref[...] = lf.T[:n_heads, :]

    @pl.when(i >= tiles_p)
    def _():
        seq = ks_ref.shape[2]
        for bb in range(ks_ref.shape[0]):
            rows = slice(bb * seq, (bb + 1) * seq)
            ks_ref[bb] = kn[rows, :].T
            vs_ref[bb] = v[rows, :].T
            ls_ref[bb] = lf[rows, :].T[:n_heads, :]


def _inproj_fox(x, gm, w_all, wf_all, layer, n_layers, bf, qg, kg, summ, expand, n_p, seq_p, seq_s,
                carried):
    n, d = x.shape
    tm = TOK_TILE
    tiles_p = n_p // tm
    per_b = seq_p // tm
    grp = tm // seq_s
    row = lambda i: (i, 0)
    fixed = lambda i: (0, 0)
    big = pl.BlockSpec((tm, d), row)

    def p_map(i):
        ip = jnp.minimum(i, tiles_p - 1)
        return (ip // per_b, 0, ip % per_b)

    def stacked_map(i):
        return (layer,) + p_map(i)

    s_map = lambda i: (jnp.maximum(i - tiles_p, 0), 0, 0)
    f32 = lambda *shape: jax.ShapeDtypeStruct(shape, F32)
    bf16 = jax.ShapeDtypeStruct((n, d), BF16)
    carry_specs, aliases = _carry_specs(carried, 9, 5)
    return pl.pallas_call(
        functools.partial(_inproj_fox_kernel, tiles_p, len(carried)),
        grid=(n // tm,),
        in_specs=[big,
                  pl.BlockSpec((1, d), fixed),
                  pl.BlockSpec((None, d, 4 * d), lambda i: (layer, 0, 0)),
                  pl.BlockSpec((None, d, LANES), lambda i: (layer, 0, 0)),
                  pl.BlockSpec((1, LANES), fixed),
                  pl.BlockSpec((1, d), fixed), pl.BlockSpec((1, d), fixed),
                  pl.BlockSpec(summ.shape, fixed), pl.BlockSpec(expand.shape, fixed)] + carry_specs,
        out_specs=[big, big, big, big, pl.BlockSpec((tm, LANES), row),
                   pl.BlockSpec((None, None, d, tm), stacked_map),
                   pl.BlockSpec((None, None, d, tm), stacked_map),
                   pl.BlockSpec((None, FOX_HEADS, tm), p_map),
                   pl.BlockSpec((grp, d, seq_s), s_map), pl.BlockSpec((grp, d, seq_s), s_map),
                   pl.BlockSpec((grp, FOX_HEADS, seq_s), s_map)],
        out_shape=[bf16, bf16, bf16, bf16, f32(n, LANES),
                   f32(n_layers, n_p // seq_p, d, seq_p), f32(n_layers, n_p // seq_p, d, seq_p),
                   f32(n_p // seq_p, FOX_HEADS, seq_p),
                   f32((n - n_p) // seq_s, d, seq_s), f32((n - n_p) // seq_s, d, seq_s),
                   f32((n - n_p) // seq_s, FOX_HEADS, seq_s)],
        input_output_aliases=aliases,
        compiler_params=_cparams(1),
        name="inproj_fox",
    )(x, gm, w_all, wf_all, bf, qg, kg, summ, expand, *carried)


def _half_masks(rows):
    lane = lax.broadcasted_iota(jnp.int32, (rows, LANES), 1)
    return lane < HEAD_GROUP, lane >= HEAD_GROUP


NT_DIMS = (((1,), (1,)), ((), ()))


def _row_update(idx, s, pv, m_ref, l_ref, acc_ref):
    m_prev = m_ref[idx]
    m_new = jnp.maximum(m_prev, jnp.max(s, axis=-1, keepdims=True))
    alpha = jnp.exp2(m_prev - m_new)
    p = jnp.exp2(s - m_new)
    l_ref[idx] = alpha * l_ref[idx] + jnp.sum(p, axis=-1, keepdims=True)
    acc_ref[idx] = alpha * acc_ref[idx] + pv(p.astype(BF16))
    m_ref[idx] = m_new


def _init_state(m_ref, l_ref, acc_ref):
    m_ref[...] = jnp.full(m_ref.shape, NEG, F32)
    l_ref[...] = jnp.zeros(l_ref.shape, F32)
    acc_ref[...] = jnp.zeros(acc_ref.shape, F32)


def _diff_lambda(lq1, lk1, lq2, lk2, lambda_init):
    a = jnp.exp(jnp.sum(lq1[...] * lk1[...], axis=-1, keepdims=True))
    b = jnp.exp(jnp.sum(lq2[...] * lk2[...], axis=-1, keepdims=True))
    return a - b + lambda_init


def _alibi_table(slope2, rows, cols, q0, k0):
    i = lax.broadcasted_iota(jnp.int32, (rows, cols), 0)
    j = lax.broadcasted_iota(jnp.int32, (rows, cols), 1)
    qpos = i + q0
    kpos = j + k0
    allowed = jnp.right_shift(kpos, CHUNK_SHIFT) <= jnp.right_shift(qpos, CHUNK_SHIFT)
    bias = slope2 * (i - jnp.abs(qpos - kpos)).astype(F32)
    return jnp.where(allowed, bias, NEG)


BIAS_TERMS = 3


def _bias_lanes(c):
    return HEAD_GROUP * (1 - c)


def _own_lanes(lane, c):
    return (lane >= HEAD_GROUP * c) & (lane < HEAD_GROUP * (c + 1))


def _augment_keys(k_blk, bias, c):
    lane = lax.broadcasted_iota(jnp.int32, k_blk.shape, 1)
    b0 = _bias_lanes(c)
    out = jnp.where(_own_lanes(lane, c), k_blk, jnp.zeros_like(k_blk))
    rest = bias
    for t in range(BIAS_TERMS):
        term = rest.astype(BF16)
        out = jnp.where(lane == b0 + t, term, out)
        rest = rest - term.astype(F32)
    return out


def _augment_queries(q, c):
    lane = lax.broadcasted_iota(jnp.int32, q.shape, 1)
    b0 = _bias_lanes(c)
    ones = jnp.where((lane >= b0) & (lane < b0 + BIAS_TERMS), 1.0, 0.0).astype(q.dtype)
    return jnp.where(_own_lanes(lane, c), q, ones)


ONES_ROWS = 16


def _transpose_values(v_ref, vt_ref, v_cols):
    nblk, n_sets, rows, tk = vt_ref.shape
    r = lax.broadcasted_iota(jnp.int32, (ONES_ROWS, tk), 0)
    tail = jnp.where(r == 0, 1.0, 0.0).astype(vt_ref.dtype)
    for blk in range(nblk):
        vt = v_ref[blk * tk:(blk + 1) * tk, :].astype(F32).T.astype(vt_ref.dtype)
        for s in range(n_sets):
            vt_ref[blk, s] = jnp.concatenate([vt[v_cols[s], :], tail], axis=0)


def _scores_t(ka, qa, q_lo):
    return lax.dot_general(ka, qa[q_lo:, :], NT_DIMS, preferred_element_type=F32)


def _block_t(idx, st, vt, tbl, q_lo, m_ref, acc_ref):
    at = idx + (slice(None), slice(q_lo, None))
    if tbl is not None:
        st = st + tbl
    m_prev = m_ref[at]
    m_new = jnp.maximum(m_prev, jnp.max(st, axis=0, keepdims=True))
    alpha = jnp.exp2(m_prev - m_new)
    pt = jnp.exp2(st - m_new).astype(BF16)
    acc_ref[at] = alpha * acc_ref[at] + jnp.dot(vt, pt, preferred_element_type=F32)
    m_ref[at] = m_new


def _sweep_t(qa, kaug_ref, vt_ref, tbl_ref, m_ref, acc_ref):
    nblk, n_sets, _, tk = vt_ref.shape
    nq = len(qa)
    assert qa[0][0].shape[0] == tk and tk % DIAG_TK == 0

    def step(qi, ki, lo, width, q_lo, tbl):
        rows = slice(ki * tk + lo, ki * tk + lo + width)
        st = [_scores_t(kaug_ref[c, rows, :], qa[qi][c], q_lo) for c in range(2)]
        for c in range(2):
            vt = vt_ref[ki, c % n_sets][:, lo:lo + width]
            _block_t((qi, c), st[c], vt, tbl, q_lo, m_ref, acc_ref)

    for ki in range(nblk):
        for lo in range(0, tk, DIAG_TK):
            step(ki, ki, lo, DIAG_TK, lo, tbl_ref[lo:lo + DIAG_TK, lo:])
        for qi in range(ki + 1, nq):
            step(qi, ki, 0, tk, 0, None)


def _augmented_query_tiles(q_ref, tq):
    return [[_augment_queries(q_ref[qi * tq:(qi + 1) * tq, :], c) for c in range(2)]
            for qi in range(q_ref.shape[0] // tq)]


def _diff_prompt_kernel(lambda_init, sl_ref, lq1, lk1, lq2, lk2, q_ref, k_ref, v_ref, g_ref,
                        o_ref, kaug_ref, vt_ref, m_ref, acc_ref, tbl_ref):
    nq, _, _, tq = m_ref.shape
    nblk, _, _, tk = vt_ref.shape
    hd2 = v_ref.shape[1]
    slope2 = sl_ref[pl.program_id(1)]
    for blk in range(nblk):
        rows = slice(blk * tk, (blk + 1) * tk)
        kpos = lax.broadcasted_iota(jnp.int32, (tk, LANES), 0) + blk * tk
        bias = slope2 * kpos.astype(F32)
        for c in range(2):
            kaug_ref[c, rows, :] = _augment_keys(k_ref[rows, :], bias, c)
    _transpose_values(v_ref, vt_ref, (slice(None),))
    dj = lax.broadcasted_iota(jnp.int32, (tk, tq), 0)
    i = lax.broadcasted_iota(jnp.int32, (tk, tq), 1)
    allowed = jnp.right_shift(dj, CHUNK_SHIFT) <= jnp.right_shift(i, CHUNK_SHIFT)
    ahead = jnp.maximum(dj - i, 0).astype(F32)
    tbl_ref[...] = jnp.where(allowed, (-2.0 * slope2) * ahead, NEG)

    m_ref[...] = jnp.full(m_ref.shape, NEG, F32)
    acc_ref[...] = jnp.zeros(acc_ref.shape, F32)
    _sweep_t(_augmented_query_tiles(q_ref, tq), kaug_ref, vt_ref, tbl_ref, m_ref, acc_ref)
    lam = _diff_lambda(lq1, lk1, lq2, lk2, lambda_init)
    for qi in range(nq):
        num = [acc_ref[qi, c, :hd2, :] / acc_ref[qi, c, hd2:hd2 + 1, :] for c in range(2)]
        ot = num[0] - lam * num[1]
        ms = jnp.mean(ot * ot, axis=0, keepdims=True)
        o = (ot * lax.rsqrt(ms + EPS)).T
        o_ref[qi * tq:(qi + 1) * tq, :] = ((o * g_ref[...]) * (1.0 - lambda_init)).astype(o_ref.dtype)


def _diff_prompt(qb, kb, vb, slopes2, lam_params, subln, n_batch, seq, lambda_init):
    d = qb.shape[1]
    tq = min(ATT_TQ, seq)
    tk = min(ATT_TK, tq)
    nq = seq // tq
    hd2 = 2 * HEAD_GROUP
    small = pl.BlockSpec((1, HEAD_GROUP), lambda b, h: (0, 0))
    whole_seq = pl.BlockSpec((seq, hd2), lambda b, h: (b, h))
    return pl.pallas_call(
        functools.partial(_diff_prompt_kernel, lambda_init),
        grid=(n_batch, DIFF_HEADS),
        in_specs=[pl.BlockSpec(memory_space=pltpu.SMEM), small, small, small, small,
                  whole_seq, whole_seq, whole_seq,
                  pl.BlockSpec((1, hd2), lambda b, h: (0, 0))],
        out_specs=whole_seq,
        out_shape=jax.ShapeDtypeStruct((n_batch * seq, d), BF16),
        scratch_shapes=[pltpu.VMEM((2, seq, hd2), BF16),
                        pltpu.VMEM((seq // tk, 1, hd2 + ONES_ROWS, tk), BF16),
                        pltpu.VMEM((nq, 2, 1, tq), F32),
                        pltpu.VMEM((nq, 2, hd2 + ONES_ROWS, tq), F32), pltpu.VMEM((tk, tq), F32)],
        compiler_params=_cparams(2),
        name="diff_attn_prompt",
    )(slopes2, *lam_params, qb, kb, vb, subln)


def _diff_sample_kernel(lambda_init, past, sl_ref, lq1, lk1, lq2, lk2, q_ref, kp_ref, vp_ref,
                        kn_ref, vn_ref, g_ref, o_ref, m_ref, l_ref, acc_ref):
    tq = q_ref.shape[0]
    n_heads = m_ref.shape[0]
    hd2 = kp_ref.shape[1]
    tk = kp_ref.shape[0] // n_heads
    step = pl.program_id(1)

    @pl.when(step == 0)
    def _():
        _init_state(m_ref, l_ref, acc_ref)

    lo, hi = _half_masks(tq)
    col = lax.broadcasted_iota(jnp.int32, (1, tk), 1) + (step * tk - past)

    def q_streams(h):
        q = q_ref[:, h * hd2:(h + 1) * hd2]
        zero = jnp.zeros_like(q)
        return jnp.where(lo, q, zero), jnp.where(hi, q, zero)

    scores = {}
    for h in range(n_heads):
        k_h = kp_ref[pl.ds(h, tk, stride=n_heads), :].astype(BF16)
        bias = sl_ref[h] * col.astype(F32)
        for c, q_c in enumerate(q_streams(h)):
            scores[h, c] = lax.dot_general(q_c, k_h, NT_DIMS, preferred_element_type=F32) + bias
    for h in range(n_heads):
        v_h = vp_ref[pl.ds(h, tk, stride=n_heads), :].astype(BF16)
        for c in range(2):
            _row_update((h, c), scores[h, c], lambda p: jnp.dot(p, v_h, preferred_element_type=F32),
                        m_ref, l_ref, acc_ref)

    @pl.when(step == pl.num_programs(1) - 1)
    def _():
        lam = _diff_lambda(lq1, lk1, lq2, lk2, lambda_init)
        new_scores = {}
        for h in range(n_heads):
            k_h = kn_ref[:, h * hd2:(h + 1) * hd2]
            bias = _alibi_table(sl_ref[h], tq, tq, past, past)
            for c, q_c in enumerate(q_streams(h)):
                new_scores[h, c] = lax.dot_general(q_c, k_h, NT_DIMS, preferred_element_type=F32) + bias
        for h in range(n_heads):
            cols = slice(h * hd2, (h + 1) * hd2)
            v_h = vn_ref[:, cols]
            for c in range(2):
                _row_update((h, c), new_scores[h, c], lambda p: jnp.dot(p, v_h, preferred_element_type=F32),
                            m_ref, l_ref, acc_ref)
            o = acc_ref[h, 0] / l_ref[h, 0] - lam * (acc_ref[h, 1] / l_ref[h, 1])
            ms = jnp.mean(o * o, axis=-1, keepdims=True)
            o_ref[:, cols] = ((o * lax.rsqrt(ms + EPS) * g_ref[...]) * (1.0 - lambda_init)).astype(o_ref.dtype)


def _diff_sample(qb, kb, vb, cache_k, cache_v, layer, slopes2, lam_params, subln, row0, n_batch, seq,
                 lambda_init):
    d = qb.shape[1]
    past, n_heads, hd2 = cache_k.shape[2:]
    tk = SAMPLE_TK if past % SAMPLE_TK == 0 else past
    blk0 = row0 // seq
    rows = lambda a: a.reshape(a.shape[:2] + (past * n_heads, hd2))
    cache_k, cache_v = rows(cache_k), rows(cache_v)
    small = pl.BlockSpec((1, HEAD_GROUP), lambda b, c: (0, 0))
    new = pl.BlockSpec((seq, d), lambda b, c: (blk0 + b, 0))
    old = pl.BlockSpec((None, None, tk * n_heads, hd2), lambda b, c: (layer, b, c, 0))
    return pl.pallas_call(
        functools.partial(_diff_sample_kernel, lambda_init, past),
        grid=(n_batch, past // tk),
        in_specs=[pl.BlockSpec(memory_space=pltpu.SMEM), small, small, small, small,
                  new, old, old, new, new, pl.BlockSpec((1, hd2), lambda b, c: (0, 0))],
        out_specs=pl.BlockSpec((seq, d), lambda b, c: (b, 0)),
        out_shape=jax.ShapeDtypeStruct((n_batch * seq, d), BF16),
        scratch_shapes=[pltpu.VMEM((n_heads, 2, seq, 1), F32), pltpu.VMEM((n_heads, 2, seq, 1), F32),
                        pltpu.VMEM((n_heads, 2, seq, hd2), F32)],
        compiler_params=_cparams(2),
        name="diff_attn_sample",
    )(slopes2, *lam_params, qb, cache_k, cache_v, kb, vb, subln)


def _causal_table(rows, cols):
    i = lax.broadcasted_iota(jnp.int32, (rows, cols), 0)
    j = lax.broadcasted_iota(jnp.int32, (rows, cols), 1)
    return jnp.where(j <= i, 0.0, NEG).astype(F32)


def _fox_prompt_kernel(q_ref, k_ref, v_ref, lf_ref, sg_ref, o_ref,
                       cum_ref, kaug_ref, vt_ref, m_ref, acc_ref, tbl_ref):
    nq, _, _, tq = m_ref.shape
    nblk, _, _, tk = vt_ref.shape
    pair = pl.program_id(1)

    @pl.when(pair == 0)
    def _():
        r = lax.broadcasted_iota(jnp.int32, (tk, tk), 0)
        c = lax.broadcasted_iota(jnp.int32, (tk, tk), 1)
        tril = jnp.where(c <= r, 1.0, 0.0).astype(BF16)
        carry = jnp.zeros((1, LANES), F32)
        for blk in range(nblk):
            x = lf_ref[blk * tk:(blk + 1) * tk, :]
            hi = x.astype(BF16)
            lo = (x - hi.astype(F32)).astype(BF16)
            part = (jnp.dot(tril, hi, preferred_element_type=F32)
                    + jnp.dot(tril, lo, preferred_element_type=F32)) + carry
            cum_ref[blk * tk:(blk + 1) * tk, :] = part * (-LOG2E)
            carry = carry + jnp.sum(x, axis=0, keepdims=True)

    lane = lax.broadcasted_iota(jnp.int32, (tk, LANES), 1)
    for blk in range(nblk):
        rows = slice(blk * tk, (blk + 1) * tk)
        cum = cum_ref[rows, :]
        for c in range(2):
            col = jnp.sum(jnp.where(lane == 2 * pair + c, cum, 0.0), axis=-1, keepdims=True)
            bias = jnp.broadcast_to(col, (tk, LANES))
            kaug_ref[c, rows, :] = _augment_keys(k_ref[rows, :], bias, c)
    hd = HEAD_GROUP
    _transpose_values(v_ref, vt_ref, (slice(0, hd), slice(hd, 2 * hd)))
    dj = lax.broadcasted_iota(jnp.int32, (tk, tq), 0)
    i = lax.broadcasted_iota(jnp.int32, (tk, tq), 1)
    tbl_ref[...] = jnp.where(dj <= i, 0.0, NEG).astype(F32)

    m_ref[...] = jnp.full(m_ref.shape, NEG, F32)
    acc_ref[...] = jnp.zeros(acc_ref.shape, F32)
    _sweep_t(_augmented_query_tiles(q_ref, tq), kaug_ref, vt_ref, tbl_ref, m_ref, acc_ref)
    for qi in range(nq):
        rows = slice(qi * tq, (qi + 1) * tq)
        ot = jnp.concatenate([acc_ref[qi, c, :hd, :] / acc_ref[qi, c, hd:hd + 1, :] for c in range(2)],
                             axis=0)
        o_ref[rows, :] = (ot.T * sg_ref[rows, :].astype(F32)).astype(o_ref.dtype)


def _fox_prompt(qb, kb, vb, lf, sg, n_batch, seq):
    d = qb.shape[1]
    tq = min(ATT_TQ, seq)
    tk = min(ATT_TK, tq)
    nq = seq // tq
    w = LANES
    whole_seq = pl.BlockSpec((seq, w), lambda b, p: (b, p))
    return pl.pallas_call(
        _fox_prompt_kernel,
        grid=(n_batch, d // w),
        in_specs=[whole_seq, whole_seq, whole_seq,
                  pl.BlockSpec((seq, LANES), lambda b, p: (b, 0)),
                  whole_seq],
        out_specs=whole_seq,
        out_shape=jax.ShapeDtypeStruct((n_batch * seq, d), BF16),
        scratch_shapes=[pltpu.VMEM((seq, LANES), F32),
                        pltpu.VMEM((2, seq, w), BF16),
                        pltpu.VMEM((seq // tk, 2, HEAD_GROUP + ONES_ROWS, tk), BF16),
                        pltpu.VMEM((nq, 2, 1, tq), F32),
                        pltpu.VMEM((nq, 2, HEAD_GROUP + ONES_ROWS, tq), F32), pltpu.VMEM((tk, tq), F32)],
        compiler_params=_cparams(2),
        name="fox_attn_prompt",
    )(qb, kb, vb, lf, sg)


def _fox_sample_kernel(q_ref, kp_ref, vp_ref, kn_ref, vn_ref, cp_ref, cn_ref, sg_ref,
                       o_ref, m_ref, l_ref, acc_ref):
    tq = q_ref.shape[0]
    n_heads, hd, tk = kp_ref.shape
    step = pl.program_id(1)

    @pl.when(step == 0)
    def _():
        _init_state(m_ref, l_ref, acc_ref)

    scores = []
    for h in range(n_heads):
        q_h = q_ref[:, h * hd:(h + 1) * hd]
        kt_h = kp_ref[h].astype(BF16)
        scores.append(jnp.dot(q_h, kt_h, preferred_element_type=F32) + cp_ref[h:h + 1, :])
    for h in range(n_heads):
        vt_h = vp_ref[h].astype(BF16)
        _row_update(h, scores[h], lambda p: lax.dot_general(p, vt_h, NT_DIMS, preferred_element_type=F32),
                    m_ref, l_ref, acc_ref)

    @pl.when(step == pl.num_programs(1) - 1)
    def _():
        tbl = _causal_table(tq, tq)
        new_scores = []
        for h in range(n_heads):
            cols = slice(h * hd, (h + 1) * hd)
            new_scores.append(lax.dot_general(q_ref[:, cols], kn_ref[:, cols], NT_DIMS,
                                              preferred_element_type=F32) + (tbl + cn_ref[h:h + 1, :]))
        for h in range(n_heads):
            cols = slice(h * hd, (h + 1) * hd)
            v_h = vn_ref[:, cols]
            _row_update(h, new_scores[h], lambda p: jnp.dot(p, v_h, preferred_element_type=F32),
                        m_ref, l_ref, acc_ref)
            o = acc_ref[h] / l_ref[h]
            o_ref[:, cols] = (o * sg_ref[:, cols].astype(F32)).astype(o_ref.dtype)


def _fox_sample(qb, kb, vb, cache_kt, cache_vt, layer, c_past, c_new, sg, row0, n_batch, seq):
    d = qb.shape[1]
    n_heads, hd, past = cache_kt.shape[2:]
    tk = SAMPLE_TK if past % SAMPLE_TK == 0 else past
    blk0 = row0 // seq
    new = pl.BlockSpec((seq, d), lambda b, c: (blk0 + b, 0))
    old = pl.BlockSpec((None, None, n_heads, hd, tk), lambda b, c: (layer, b, 0, 0, c))
    return pl.pallas_call(
        _fox_sample_kernel,
        grid=(n_batch, past // tk),
        in_specs=[new, old, old, new, new,
                  pl.BlockSpec((None, n_heads, tk), lambda b, c: (b, 0, c)),
                  pl.BlockSpec((None, n_heads, seq), lambda b, c: (b, 0, 0)),
                  new],
        out_specs=pl.BlockSpec((seq, d), lambda b, c: (b, 0)),
        out_shape=jax.ShapeDtypeStruct((n_batch * seq, d), BF16),
        scratch_shapes=[pltpu.VMEM((n_heads, seq, 1), F32), pltpu.VMEM((n_heads, seq, 1), F32),
                        pltpu.VMEM((n_heads, seq, hd), F32)],
        compiler_params=_cparams(2),
        name="fox_attn_sample",
    )(qb, cache_kt, cache_vt, kb, vb, c_past, c_new, sg)


def _split_dot(x, mat):
    hi = x.astype(BF16)
    lo = (x - hi.astype(F32)).astype(BF16)
    return (jnp.dot(hi, mat, preferred_element_type=F32) + jnp.dot(lo, mat, preferred_element_type=F32))


def _coff_kernel(suffix, tk, x_ref, o_ref):
    nblk = x_ref.shape[1] // tk
    r = lax.broadcasted_iota(jnp.int32, (tk, tk), 0)
    c = lax.broadcasted_iota(jnp.int32, (tk, tk), 1)
    mat = jnp.where((r > c) if suffix else (r <= c), 1.0, 0.0).astype(BF16)
    carry = jnp.zeros((x_ref.shape[0], 1), F32)
    order = range(nblk - 1, -1, -1) if suffix else range(nblk)
    for blk in order:
        cols = slice(blk * tk, (blk + 1) * tk)
        x = x_ref[:, cols]
        part = _split_dot(x, mat) + carry
        o_ref[:, cols] = (part if suffix else -part) * LOG2E
        carry = carry + jnp.sum(x, axis=-1, keepdims=True)


def _coff(xt, tk, suffix):
    n_batch, n_head, length = xt.shape
    spec = pl.BlockSpec((None, n_head, length), lambda b: (b, 0, 0))
    return pl.pallas_call(
        functools.partial(_coff_kernel, suffix, tk),
        grid=(n_batch,),
        in_specs=[spec],
        out_specs=spec,
        out_shape=jax.ShapeDtypeStruct((n_batch, n_head, length), F32),
        compiler_params=_cparams(1),
        name="fox_coff_suffix" if suffix else "fox_coff_prefix",
    )(xt)


def _outproj_kernel(tiles_p, op_ref, os_ref, x_ref, w_ref, gf_ref, wr_hi_ref, wr_lo_ref, br_ref,
                    xo_ref, meta_ref):
    o = jnp.where(pl.program_id(0) < tiles_p, op_ref[...], os_ref[...])
    xn = x_ref[...] + jnp.dot(o, w_ref[...], preferred_element_type=F32)
    xo_ref[...] = xn
    t = _rms_rows(xn, gf_ref[...])
    t_hi = t.astype(BF16)
    t_lo = (t - t_hi.astype(F32)).astype(BF16)
    logits = (jnp.dot(t_hi, wr_hi_ref[...], preferred_element_type=F32)
              + jnp.dot(t_lo, wr_hi_ref[...], preferred_element_type=F32)
              + jnp.dot(t_hi, wr_lo_ref[...], preferred_element_type=F32)) + br_ref[...]
    rows = logits.shape[0]
    lane = lax.broadcasted_iota(jnp.int32, (rows, LANES), 1).astype(F32)
    big = float(LANES)

    def first_argmax(vals, vmax):
        return jnp.min(jnp.where(vals == vmax, lane, big), axis=-1, keepdims=True)

    gl = jnp.where(lane < N_GROUPS, logits, NEG)
    gmax = jnp.max(gl, axis=-1, keepdims=True)
    gsum = jnp.sum(jnp.where(lane < N_GROUPS, jnp.exp(logits - gmax), 0.0), axis=-1, keepdims=True)
    p_top = 1.0 / gsum
    g_idx = first_argmax(gl, gmax)
    base = N_GROUPS + EXPERTS_PER_GROUP * g_idx
    el = jnp.where((lane >= base) & (lane < base + EXPERTS_PER_GROUP), logits, NEG)
    v1 = jnp.max(el, axis=-1, keepdims=True)
    i1 = first_argmax(el, v1)
    el2 = jnp.where(lane == i1, NEG, el)
    v2 = jnp.max(el2, axis=-1, keepdims=True)
    i2 = first_argmax(el2, v2)
    e2 = jnp.exp(v2 - v1)
    w1 = p_top / (1.0 + e2)
    w2 = p_top * e2 / (1.0 + e2)
    a1 = i1 - base
    a2 = i2 - base
    lo = jnp.minimum(a1, a2)
    hi = jnp.maximum(a1, a2)
    wa = jnp.where(a1 < a2, w1, w2)
    wb = jnp.where(a1 < a2, w2, w1)
    pair = jnp.where(lo == 0.0, hi - 1.0, jnp.where(lo == 1.0, hi + 1.0, 5.0))
    bucket = g_idx * N_PAIRS + pair
    meta_ref[...] = jnp.where(lane == 0.0, bucket, jnp.where(lane == 1.0, wa, jnp.where(lane == 2.0, wb, 0.0)))


def _outproj(o_p, o_s, x, w_all, layer, gf, wr_hi, wr_lo, br):
    n, d = x.shape
    tm = TOK_TILE
    tiles_p = o_p.shape[0] // tm
    row = lambda i: (i, 0)
    fixed = lambda i: (0, 0)
    big = pl.BlockSpec((tm, d), row)
    return pl.pallas_call(
        functools.partial(_outproj_kernel, tiles_p),
        grid=(n // tm,),
        in_specs=[pl.BlockSpec((tm, d), lambda i: (jnp.minimum(i, tiles_p - 1), 0)),
                  pl.BlockSpec((tm, d), lambda i: (jnp.maximum(i - tiles_p, 0), 0)),
                  big,
                  pl.BlockSpec((None, d, d), lambda i: (layer, 0, 0)),
                  pl.BlockSpec((1, d), fixed),
                  pl.BlockSpec((d, LANES), fixed), pl.BlockSpec((d, LANES), fixed),
                  pl.BlockSpec((1, LANES), fixed)],
        out_specs=[big, pl.BlockSpec((tm, LANES), row)],
        out_shape=[jax.ShapeDtypeStruct((n, d), F32), jax.ShapeDtypeStruct((n, LANES), F32)],
        compiler_params=_cparams(1),
        name="outproj_router",
    )(o_p, o_s, x, w_all, gf, wr_hi, wr_lo, br)


def _moe_kernel(tile_ref, ea_ref, eb_ref, lo_ref, hi_ref, first_ref,
                xs_ref, ms_ref, gf_ref, wg_a, wu_a, wd_a, wg_b, wu_b, wd_b, out_ref):
    s = pl.program_id(0)
    lo = lo_ref[s]
    hi = hi_ref[s]

    @pl.when(first_ref[s] == 1)
    def _():
        out_ref[...] = xs_ref[...]

    @pl.when(hi > lo)
    def _():
        t = _rms_rows(xs_ref[...], gf_ref[...]).astype(BF16)
        rows = t.shape[0]
        r = lax.broadcasted_iota(jnp.int32, (rows, 1), 0)
        inside = (r >= lo) & (r < hi)
        ms = ms_ref[...]
        wa = jnp.where(inside, ms[:, 1:2], 0.0)
        wb = jnp.where(inside, ms[:, 2:3], 0.0)

        def hidden(g, u):
            return (g / (1.0 + jnp.exp(-g)) * u).astype(BF16)

        g_a = jnp.dot(t, wg_a[...], preferred_element_type=F32)
        u_a = jnp.dot(t, wu_a[...], preferred_element_type=F32)
        g_b = jnp.dot(t, wg_b[...], preferred_element_type=F32)
        u_b = jnp.dot(t, wu_b[...], preferred_element_type=F32)
        y_a = jnp.dot(hidden(g_a, u_a), wd_a[...], preferred_element_type=F32)
        y_b = jnp.dot(hidden(g_b, u_b), wd_b[...], preferred_element_type=F32)
        out_ref[...] += wa * y_a + wb * y_b


def _moe(sched, xs, ms, gf, wg_all, wu_all, wd_all, layer):
    n, d = xs.shape
    de = wd_all.shape[2]
    n_steps = sched[0].shape[0]
    tile_map = lambda s, tile, ea, eb, lo, hi, first: (tile[s], 0)
    fixed = lambda s, *_: (0, 0)
    wa_map = lambda s, tile, ea, eb, lo, hi, first: (layer, ea[s], 0, 0)
    wb_map = lambda s, tile, ea, eb, lo, hi, first: (layer, eb[s], 0, 0)
    grid_spec = pltpu.PrefetchScalarGridSpec(
        num_scalar_prefetch=6,
        grid=(n_steps,),
        in_specs=[pl.BlockSpec((MOE_TILE, d), tile_map),
                  pl.BlockSpec((MOE_TILE, ms.shape[1]), tile_map),
                  pl.BlockSpec((1, d), fixed),
                  pl.BlockSpec((None, None, d, de), wa_map),
                  pl.BlockSpec((None, None, d, de), wa_map),
                  pl.BlockSpec((None, None, de, d), wa_map),
                  pl.BlockSpec((None, None, d, de), wb_map),
                  pl.BlockSpec((None, None, d, de), wb_map),
                  pl.BlockSpec((None, None, de, d), wb_map)],
        out_specs=pl.BlockSpec((MOE_TILE, d), tile_map),
    )
    return pl.pallas_call(
        _moe_kernel,
        grid_spec=grid_spec,
        out_shape=jax.ShapeDtypeStruct((n, d), F32),
        compiler_params=_cparams(1),
        name="moe_experts",
    )(*sched, xs, ms, gf, wg_all, wu_all, wd_all, wg_all, wu_all, wd_all)


def _moe_schedule(bucket, n):
    n_tiles = n // MOE_TILE
    n_steps = n_tiles + N_BUCKETS - 1
    perm = jnp.argsort(bucket, stable=True).astype(jnp.int32)
    sorted_b = bucket[perm]
    inv = jnp.argsort(perm).astype(jnp.int32)
    offs = jnp.searchsorted(sorted_b, jnp.arange(N_BUCKETS + 1, dtype=jnp.int32), side="left")
    offs = offs.astype(jnp.int32)
    fb = sorted_b[::MOE_TILE]
    lb = sorted_b[MOE_TILE - 1::MOE_TILE]
    per_tile = lb - fb + 1
    starts = jnp.cumsum(per_tile) - per_tile
    total = jnp.sum(per_tile)
    s = jnp.arange(n_steps, dtype=jnp.int32)
    tile = jnp.clip(jnp.searchsorted(starts, s, side="right").astype(jnp.int32) - 1, 0, n_tiles - 1)
    valid = s < total
    bkt = jnp.where(valid, fb[tile] + (s - starts[tile]), lb[n_tiles - 1])
    lo = jnp.clip(offs[bkt] - tile * MOE_TILE, 0, MOE_TILE)
    hi = jnp.clip(offs[bkt + 1] - tile * MOE_TILE, 0, MOE_TILE)
    lo = jnp.where(valid, lo, 0)
    hi = jnp.where(valid, hi, 0)
    first = (valid & (s == starts[tile])).astype(jnp.int32)
    grp = bkt // N_PAIRS
    ea = grp * EXPERTS_PER_GROUP + jnp.asarray(PAIR_A)[bkt % N_PAIRS]
    eb = grp * EXPERTS_PER_GROUP + jnp.asarray(PAIR_B)[bkt % N_PAIRS]
    return perm, inv, (tile, ea.astype(jnp.int32), eb.astype(jnp.int32), lo.astype(jnp.int32),
                       hi.astype(jnp.int32), first)


def _tile_gain(g, d):
    return jnp.tile(g.astype(F32), d // g.shape[0]).reshape(1, d)


def kernel(x_prompt, x_sample, cache_diff_k, cache_diff_v, cache_fox_k, cache_fox_v, cache_fox_logf, norm_mix, norm_ffn, diff_w_in, diff_w_out, diff_q_norm, diff_k_norm, diff_lambda_q1, diff_lambda_k1, diff_lambda_q2, diff_lambda_k2, diff_subln, fox_w_in, fox_b_f, fox_w_out, fox_q_norm, fox_k_norm, moe_w_group, moe_b_group, moe_w_expert, moe_b_expert, moe_w_gate, moe_w_up, moe_w_down):
    bp, sp, d = x_prompt.shape
    bs, ss, _ = x_sample.shape
    n_p = bp * sp
    n_s = bs * ss
    n = n_p + n_s
    depth = norm_mix.shape[0]
    past = cache_diff_k.shape[2]
    assert n % TOK_TILE == 0 and n % MOE_TILE == 0 and n_p % TOK_TILE == 0
    assert d % LANES == 0 and sp % min(ATT_TQ, sp) == 0 and n_p % ss == 0
    assert DIFF_HEADS * 2 * HEAD_GROUP == d and FOX_HEADS * HEAD_GROUP == d

    summ, expand = _group_mats(d)
    x = jnp.concatenate([x_prompt.reshape(n_p, d), x_sample.reshape(n_s, d)], axis=0)

    diff_w_in_b = diff_w_in.astype(BF16)
    diff_w_out_b = diff_w_out.astype(BF16)
    fox_w_main_b = fox_w_in[:, :, :4 * d].astype(BF16)
    fox_w_f_b = jnp.pad(fox_w_in[:, :, 4 * d:], ((0, 0), (0, 0), (0, LANES - FOX_HEADS))).astype(BF16)
    fox_w_out_b = fox_w_out.astype(BF16)
    wg_b = moe_w_gate.astype(BF16)
    wu_b = moe_w_up.astype(BF16)
    wd_b = moe_w_down.astype(BF16)
    w_router = jnp.pad(jnp.concatenate([moe_w_group, moe_w_expert], axis=-1),
                       ((0, 0), (0, 0), (0, LANES - N_GROUPS - N_EXPERTS)))
    wr_hi = w_router.astype(BF16)
    wr_lo = (w_router - wr_hi.astype(F32)).astype(BF16)
    b_router = jnp.pad(jnp.concatenate([moe_b_group, moe_b_expert], axis=-1),
                       ((0, 0), (0, LANES - N_GROUPS - N_EXPERTS)))
    slopes2 = jnp.asarray(2.0 ** (-8.0 * np.arange(1, DIFF_HEADS + 1) / DIFF_HEADS) * LOG2E, F32)
    cache_fkt = jnp.transpose(cache_fox_k, (0, 1, 3, 4, 2))
    cache_fvt = jnp.transpose(cache_fox_v, (0, 1, 3, 4, 2))
    cache_flt = jnp.transpose(cache_fox_logf.astype(F32), (0, 1, 3, 2))

    def from_feature_major(a, seq):
        return a.reshape(a.shape[0], FOX_HEADS, HEAD_GROUP, seq).transpose(0, 3, 1, 2)

    outs = {name: [] for name in ("dks", "dvs", "flp", "fks", "fvs", "fls")}
    n_diff, n_fox = diff_w_in.shape[0], fox_w_in.shape[0]
    diff_cache_p = tuple(jnp.zeros((n_diff, n_p * DIFF_HEADS, LANES), F32) for _ in range(2))
    fox_cache_p = tuple(jnp.zeros((n_fox, bp, d, sp), F32) for _ in range(2))
    for i in range(depth):
        j = i // 2
        gm = norm_mix[i].reshape(1, d)
        if i % 2 == 0:
            lambda_init = 0.8 - 0.6 * math.exp(-0.3 * i)
            qb, kb, vb, k_p, v_p, k_s, v_s = _inproj_diff(
                x, gm, diff_w_in_b, j, n_diff, _tile_gain(diff_q_norm[j], d), _tile_gain(diff_k_norm[j], d),
                summ, expand, n_p, diff_cache_p)
            diff_cache_p = (k_p, v_p)
            lam_params = [p[j].reshape(1, HEAD_GROUP).astype(F32) for p in
                          (diff_lambda_q1, diff_lambda_k1, diff_lambda_q2, diff_lambda_k2)]
            subln = diff_subln[j].reshape(1, 2 * HEAD_GROUP)
            o_p = _diff_prompt(qb, kb, vb, slopes2, lam_params, subln, bp, sp, lambda_init)
            o_s = _diff_sample(qb, kb, vb, cache_diff_k, cache_diff_v, j, slopes2, lam_params, subln,
                               n_p, bs, ss, lambda_init)
            w_out_b = diff_w_out_b
            outs["dks"].append(k_s.reshape(bs, ss, DIFF_HEADS, 2 * HEAD_GROUP))
            outs["dvs"].append(v_s.reshape(bs, ss, DIFF_HEADS, 2 * HEAD_GROUP))
        else:
            bf = jnp.pad(fox_b_f[j], (0, LANES - FOX_HEADS)).reshape(1, LANES)
            qb, kb, vb, sg, lf, kt_p, vt_p, lt_p, kt_s, vt_s, lt_s = _inproj_fox(
                x, gm, fox_w_main_b, fox_w_f_b, j, n_fox, bf, _tile_gain(fox_q_norm[j], d),
                _tile_gain(fox_k_norm[j], d), summ, expand, n_p, sp, ss, fox_cache_p)
            fox_cache_p = (kt_p, vt_p)
            tk_c = SAMPLE_TK if past % SAMPLE_TK == 0 else past
            c_new = _coff(lt_s, ss, False)
            c_past = _coff(cache_flt[j], tk_c, True)
            o_p = _fox_prompt(qb, kb, vb, lf, sg, bp, sp)
            o_s = _fox_sample(qb, kb, vb, cache_fkt, cache_fvt, j, c_past, c_new, sg, n_p, bs, ss)
            w_out_b = fox_w_out_b
            outs["flp"].append(lt_p.transpose(0, 2, 1))
            outs["fks"].append(from_feature_major(kt_s, ss))
            outs["fvs"].append(from_feature_major(vt_s, ss))
            outs["fls"].append(lt_s.transpose(0, 2, 1))
        gf = norm_ffn[i].reshape(1, d)
        x, meta = _outproj(o_p, o_s, x, w_out_b, j, gf, wr_hi[i], wr_lo[i], b_router[i].reshape(1, LANES))
        bucket = meta[:, 0].astype(jnp.int32)
        perm, inv, sched = _moe_schedule(bucket, n)
        xs = x[perm]
        ms = meta[:, :8][perm]
        ys = _moe(sched, xs, ms, gf, wg_b, wu_b, wd_b, i)
        if i + 1 < depth:
            x = ys[inv]

    y_prompt = ys[inv[:n_p]].reshape(bp, sp, d)
    y_sample = ys[inv[n_p:]].reshape(bs, ss, d)
    stk = lambda name: jnp.stack(outs[name])
    dkp, dvp = (a.reshape(n_diff, bp, sp, DIFF_HEADS, 2 * HEAD_GROUP) for a in diff_cache_p)
    fkp, fvp = (a.reshape(n_fox, bp, FOX_HEADS, HEAD_GROUP, sp).transpose(0, 1, 4, 2, 3)
                for a in fox_cache_p)
    return (y_prompt, y_sample, dkp, dvp, fkp, fvp, stk("flp"),
            stk("dks"), stk("dvs"), stk("fks"), stk("fvs"), stk("fls"))
```
